```python
import jax, jax.numpy as jnp
from jax import lax
import numpy as np

D_MODEL = 2048
BATCH = 4
SEQ = 2048
DEPTH = 4
DEC_BATCH = 128
DEC_SEQ = 1
PAST_LEN = 16384
PAGE_SIZE = 128

MIX_W = D_MODEL
RWKV_W = MIX_W // 2
GMLP_W = MIX_W - RWKV_W
HEAD_DIM = 64
N_RWKV_HEADS = RWKV_W // HEAD_DIM
CHUNK = 128
N_GMLP_GROUPS = 8
GMLP_GROUP_W = GMLP_W // N_GMLP_GROUPS
DECAY_LORA = 96
ICLR_LORA = 96
VRES_LORA = 64
SHIFT_COLS = 3 * RWKV_W + DECAY_LORA + ICLR_LORA
IN_COLS = SHIFT_COLS + RWKV_W + 3 * GMLP_W
RMS_EPS = 1e-6
LN_EPS = 1e-5
LNX_EPS = 64e-5

kernel_name = "hymba_rwkv7_chunkgmlp_step"


def _f32(t):
    return t.astype(jnp.float32)


def rms_norm(x, g):
    xf = _f32(x)
    y = xf * lax.rsqrt(jnp.mean(xf * xf, axis=-1, keepdims=True) + RMS_EPS)
    return (y * _f32(g)).astype(x.dtype)


def token_shift(p, prev, mu):
    shifted = jnp.concatenate([_f32(prev)[:, None], p[:, :-1]], axis=1)
    return p + (shifted - p) * _f32(mu), p[:, -1]


def wkv7_scan(S0, r, w, k, v, a, b):
    def step(S, inp):
        r_t, w_t, k_t, v_t, a_t, b_t = inp
        sa = jnp.einsum('nhvk,nhk->nhv', S, a_t)
        S = (S * w_t[:, :, None, :] + sa[..., None] * b_t[:, :, None, :]
             + v_t[..., None] * k_t[:, :, None, :])
        return S, jnp.einsum('nhvk,nhk->nhv', S, r_t)
    xs = tuple(jnp.swapaxes(t, 0, 1) for t in (r, w, k, v, a, b))
    S, ys = lax.scan(step, S0, xs)
    return jnp.swapaxes(ys, 0, 1), S


def rwkv7_branch(ps, z, S0, v_first, l, w0, w2, a0, a2, k_k, k_a, r_k, lnx_g, lnx_b, v0, v1, v2):
    N, T, _ = ps.shape
    r, k, v, w_lo, a_lo = jnp.split(
        ps, [RWKV_W, 2 * RWKV_W, 3 * RWKV_W, 3 * RWKV_W + DECAY_LORA], axis=-1)
    w = -jax.nn.softplus(-(_f32(w0[l]) + jnp.tanh(w_lo) @ _f32(w2[l]))) - 0.5
    decay = jnp.exp(-jnp.exp(w))
    a = jax.nn.sigmoid(_f32(a0[l]) + a_lo @ _f32(a2[l]))
    if l == 0:
        v_first = v
    else:
        v = v + (v_first - v) * jax.nn.sigmoid(_f32(v0[l - 1]) + (v @ _f32(v1[l - 1])) @ _f32(v2[l - 1]))
    hd = lambda t: t.reshape(N, T, N_RWKV_HEADS, HEAD_DIM)
    kk = hd(k * _f32(k_k[l]))
    kk = kk / jnp.maximum(jnp.sqrt(jnp.sum(kk * kk, axis=-1, keepdims=True)), 1e-12)
    ah = hd(a)
    k = k * (1.0 + (a - 1.0) * _f32(k_a[l]))
    rh, kh, vh = hd(r), hd(k), hd(v)
    y, S = wkv7_scan(S0, rh, hd(decay), kh, vh, -kk, kk * ah)
    mu = jnp.mean(y, axis=-1, keepdims=True)
    var = jnp.mean((y - mu) ** 2, axis=-1, keepdims=True)
    y = ((y - mu) * lax.rsqrt(var + LNX_EPS)).reshape(N, T, RWKV_W) * _f32(lnx_g[l]) + _f32(lnx_b[l])
    bonus = jnp.sum(rh * kh * _f32(r_k[l]), axis=-1, keepdims=True) * vh
    out = (y + bonus.reshape(N, T, RWKV_W)) * jax.nn.silu(z)
    return out, S, v_first


def chunk_gmlp_branch(u, vg, z, l, gln_g, gln_b, w_s, b_s):
    N, T, _ = u.shape
    mu = jnp.mean(vg, axis=-1, keepdims=True)
    var = jnp.mean((vg - mu) ** 2, axis=-1, keepdims=True)
    vn = (vg - mu) * lax.rsqrt(var + LN_EPS) * _f32(gln_g[l]) + _f32(gln_b[l])
    tc = min(T, CHUNK)
    mask = jnp.tril(jnp.ones((tc, tc), dtype=bool))
    W = jnp.where(mask[None], _f32(w_s[l])[:, :tc, :tc], 0.0)
    vc = vn.reshape(N, T // tc, tc, N_GMLP_GROUPS, GMLP_GROUP_W)
    bias = jnp.swapaxes(_f32(b_s[l])[:, :tc], 0, 1)[None, None, :, :, None]
    mixed = jnp.einsum('gij,ncjgd->ncigd', W, vc) + bias
    out = u * mixed.reshape(N, T, GMLP_W) * jax.nn.silu(z)
    return out, vn


def trunk(x, S_init, shift_init, norm_g, w_in, mu_shift, w0, w2, a0, a2, k_k, k_a, r_k,
          lnx_g, lnx_b, v0, v1, v2, gln_g, gln_b, w_s, b_s, w_out, final_g):
    h = x
    v_first = None
    S_out, shift_out, vn_out = [], [], []
    splits = [SHIFT_COLS, SHIFT_COLS + RWKV_W, SHIFT_COLS + RWKV_W + GMLP_W,
              SHIFT_COLS + RWKV_W + 2 * GMLP_W]
    for l in range(DEPTH):
        xn = rms_norm(h, norm_g[l])
        p = _f32(xn @ w_in[l])
        ps, z_r, u, vg, z_g = jnp.split(p, splits, axis=-1)
        ps, last = token_shift(ps, shift_init[l], mu_shift[l])
        o_r, S, v_first = rwkv7_branch(ps, z_r, _f32(S_init[l]), v_first, l, w0, w2, a0, a2,
                                       k_k, k_a, r_k, lnx_g, lnx_b, v0, v1, v2)
        o_g, vn = chunk_gmlp_branch(u, vg, z_g, l, gln_g, gln_b, w_s, b_s)
        o = jnp.concatenate([o_r, o_g], axis=-1).astype(x.dtype)
        h = h + o @ w_out[l]
        S_out.append(S)
        shift_out.append(last)
        vn_out.append(vn)
    return rms_norm(h, final_g), S_out, shift_out, vn_out


def setup_inputs(seed: int = 0) -> dict:
    key = jax.random.key(seed)
    ks = jax.random.split(key, 26)
    n = lambda i, shape: jax.random.normal(ks[i], shape, dtype=jnp.float32)
    return {
        "x_prompt": n(0, (BATCH, SEQ, D_MODEL)),
        "x_sample": n(1, (DEC_BATCH, DEC_SEQ, D_MODEL)),
        "state_wkv": 0.1 * n(2, (DEPTH, DEC_BATCH, N_RWKV_HEADS, HEAD_DIM, HEAD_DIM)),
        "state_shift": n(3, (DEPTH, DEC_BATCH, SHIFT_COLS)),
        "norm_g": 1.0 + 0.02 * n(4, (DEPTH, D_MODEL)),
        "w_in": n(5, (DEPTH, D_MODEL, IN_COLS)) * D_MODEL ** -0.5,
        "mu_shift": jax.random.uniform(ks[6], (DEPTH, SHIFT_COLS), dtype=jnp.float32),
        "w0": -1.0 + 0.5 * n(7, (DEPTH, RWKV_W)),
        "w2": 0.5 * n(8, (DEPTH, DECAY_LORA, RWKV_W)) * DECAY_LORA ** -0.5,
        "a0": 0.1 * n(9, (DEPTH, RWKV_W)),
        "a2": 0.5 * n(10, (DEPTH, ICLR_LORA, RWKV_W)) * ICLR_LORA ** -0.5,
        "k_k": 0.85 + 0.02 * n(11, (DEPTH, RWKV_W)),
        "k_a": 1.0 + 0.02 * n(12, (DEPTH, RWKV_W)),
        "r_k": 0.1 * n(13, (DEPTH, N_RWKV_HEADS, HEAD_DIM)),
        "lnx_g": 1.0 + 0.02 * n(14, (DEPTH, RWKV_W)),
        "lnx_b": 0.02 * n(15, (DEPTH, RWKV_W)),
        "v0": 0.1 * n(16, (DEPTH - 1, RWKV_W)),
        "v1": n(17, (DEPTH - 1, RWKV_W, VRES_LORA)) * RWKV_W ** -0.5,
        "v2": 0.5 * n(18, (DEPTH - 1, VRES_LORA, RWKV_W)) * VRES_LORA ** -0.5,
        "gln_g": 1.0 + 0.02 * n(19, (DEPTH, GMLP_W)),
        "gln_b": 0.02 * n(20, (DEPTH, GMLP_W)),
        "w_s": n(21, (DEPTH, N_GMLP_GROUPS, CHUNK, CHUNK)) * CHUNK ** -0.5,
        "b_s": 1.0 + 0.02 * n(22, (DEPTH, N_GMLP_GROUPS, CHUNK)),
        "w_out": n(23, (DEPTH, MIX_W, D_MODEL)) * (MIX_W ** -0.5) * (DEPTH ** -0.5),
        "final_g": 1.0 + 0.02 * n(24, (D_MODEL,)),
    }


def reference(x_prompt, x_sample, state_wkv, state_shift, norm_g, w_in, mu_shift, w0, w2, a0, a2,
              k_k, k_a, r_k, lnx_g, lnx_b, v0, v1, v2, gln_g, gln_b, w_s, b_s, w_out, final_g):
    B = x_prompt.shape[0]
    S0_p = jnp.zeros((DEPTH, B, N_RWKV_HEADS, HEAD_DIM, HEAD_DIM), jnp.float32)
    sh0_p = jnp.zeros((DEPTH, B, SHIFT_COLS), jnp.float32)
    y_prompt, S_p, sh_p, _ = trunk(x_prompt, S0_p, sh0_p, norm_g, w_in, mu_shift, w0, w2, a0, a2,
                                   k_k, k_a, r_k, lnx_g, lnx_b, v0, v1, v2, gln_g, gln_b, w_s, b_s,
                                   w_out, final_g)
    y_sample, S_s, sh_s, vn_s = trunk(x_sample, state_wkv, state_shift, norm_g, w_in, mu_shift, w0,
                                      w2, a0, a2, k_k, k_a, r_k, lnx_g, lnx_b, v0, v1, v2, gln_g,
                                      gln_b, w_s, b_s, w_out, final_g)
    wkv_prompt = jnp.stack(S_p).astype(state_wkv.dtype)
    shift_prompt = jnp.stack(sh_p).astype(state_shift.dtype)
    wkv_sample = jnp.stack(S_s).astype(state_wkv.dtype)
    shift_sample = jnp.stack(sh_s).astype(state_shift.dtype)
    gmlp_v_sample = jnp.stack(vn_s).astype(x_sample.dtype)
    return (y_prompt, y_sample, wkv_prompt, shift_prompt, wkv_sample, shift_sample, gmlp_v_sample)
```

```python
import functools

import jax
import jax.numpy as jnp
from jax import lax
from jax.experimental import pallas as pl
from jax.experimental.pallas import tpu as pltpu

F32 = jnp.float32
BF16 = jnp.bfloat16

HEAD = 64
LANES = 128
PAIR = 2 * HEAD
WKV_CHUNK = HEAD
GMLP_CHUNK = 128
GMLP_GROUPS = 8
LORA_PAD = 128
RMS_EPS = 1e-6
LN_EPS = 1e-5
LNX_EPS = 64e-5
VMEM_LIMIT = 48 * 1024 * 1024

_NT = (((1,), (1,)), ((), ()))
_TN = (((0,), (0,)), ((), ()))


def _dot(a, b):
    return jnp.dot(a, b, preferred_element_type=F32)


def _dot_nt(a, b):
    return lax.dot_general(a, b, _NT, preferred_element_type=F32)


def _dot_tn(a, b):
    return lax.dot_general(a, b, _TN, preferred_element_type=F32)


def _cparams(sem):
    return pltpu.CompilerParams(dimension_semantics=sem, vmem_limit_bytes=VMEM_LIMIT)


def _head_ones():
    r = lax.broadcasted_iota(jnp.int32, (LANES, LANES), 0) // HEAD
    c = lax.broadcasted_iota(jnp.int32, (LANES, LANES), 1) // HEAD
    return jnp.where(r == c, 1.0, 0.0).astype(BF16)


def _head_sum(x, ones_bd):
    outs = []
    for j in range(x.shape[1] // LANES):
        xs = x[:, j * LANES:(j + 1) * LANES]
        hi = xs.astype(BF16)
        lo = (xs - hi.astype(F32)).astype(BF16)
        outs.append(_dot(hi, ones_bd) + _dot(lo, ones_bd))
    return jnp.concatenate(outs, axis=1)


def _in_proj_kernel(h_ref, g_ref, w_ref, o_ref, xn_ref):
    @pl.when(pl.program_id(1) == 0)
    def _():
        x = h_ref[...]
        ms = jnp.mean(x * x, axis=-1, keepdims=True)
        xn_ref[...] = (x * lax.rsqrt(ms + RMS_EPS) * g_ref[...]).astype(BF16)

    o_ref[...] = _dot(xn_ref[...], w_ref[...])


def _in_proj(h, g, w, tm, tn):
    m, d = h.shape
    n = w.shape[1]
    return pl.pallas_call(
        _in_proj_kernel,
        grid=(m // tm, n // tn),
        in_specs=[
            pl.BlockSpec((tm, d), lambda i, j: (i, 0)),
            pl.BlockSpec((1, d), lambda i, j: (0, 0)),
            pl.BlockSpec((d, tn), lambda i, j: (0, j)),
        ],
        out_specs=pl.BlockSpec((tm, tn), lambda i, j: (i, j)),
        out_shape=jax.ShapeDtypeStruct((m, n), F32),
        scratch_shapes=[pltpu.VMEM((tm, d), BF16)],
        compiler_params=_cparams(("parallel", "arbitrary")),
        name="in_proj",
    )(h, g, w)


def _shift_seq(x, mu, init_row, carry_ref, first):
    prev0 = jnp.where(first, init_row, carry_ref[...])
    sh = pltpu.roll(x, 1, 0)
    rows = lax.broadcasted_iota(jnp.int32, x.shape, 0)
    sh = jnp.where(rows == 0, prev0, sh)
    carry_ref[...] = x[x.shape[0] - 1:, :]
    return x + (sh - x) * mu


def _rwkv_prep_kernel(*refs, seq, has_vres, tiles_per_seq):
    (r_ref, k_ref, v_ref, lo_ref, pr_ref, pk_ref, pv_ref, plo_ref,
     mur_ref, muk_ref, muv_ref, mulo_ref, w0_ref, w2_ref, a0_ref, a2_ref,
     kk_ref, ka_ref) = refs[:18]
    pos = 18
    if has_vres:
        vf_ref, v0_ref, v1_ref, v2_ref = refs[pos:pos + 4]
        pos += 4
    (ro_ref, lwo_ref, ko_ref, vo_ref, ao_ref, bo_ref) = refs[pos:pos + 6]
    pos += 6

    if seq:
        cr_ref, ck_ref, cv_ref, clo_ref = refs[pos:pos + 4]
        first = (pl.program_id(0) % tiles_per_seq) == 0
        r = _shift_seq(r_ref[...], mur_ref[...], pr_ref[...], cr_ref, first)
        k = _shift_seq(k_ref[...], muk_ref[...], pk_ref[...], ck_ref, first)
        v = _shift_seq(v_ref[...], muv_ref[...], pv_ref[...], cv_ref, first)
        lo = _shift_seq(lo_ref[...], mulo_ref[...], plo_ref[...], clo_ref, first)
    else:
        def mix(x_ref, p_ref, mu_ref):
            x = x_ref[...]
            return x + (p_ref[...] - x) * mu_ref[...]
        r = mix(r_ref, pr_ref, mur_ref)
        k = mix(k_ref, pk_ref, muk_ref)
        v = mix(v_ref, pv_ref, muv_ref)
        lo = mix(lo_ref, plo_ref, mulo_ref)

    w_lo = lo[:, :LORA_PAD]
    a_lo = lo[:, LORA_PAD:]
    wraw = w0_ref[...] + _dot(jnp.tanh(w_lo).astype(BF16), w2_ref[...])
    w = -jax.nn.softplus(-wraw) - 0.5
    lw = -jnp.exp(w)
    a = jax.nn.sigmoid(a0_ref[...] + _dot(a_lo.astype(BF16), a2_ref[...]))
    if has_vres:
        vv = _dot(_dot(v.astype(BF16), v1_ref[...]).astype(BF16), v2_ref[...])
        v = v + (vf_ref[...] - v) * jax.nn.sigmoid(v0_ref[...] + vv)
    ones_bd = _head_ones()
    kk = k * kk_ref[...]
    kk = kk / jnp.maximum(jnp.sqrt(_head_sum(kk * kk, ones_bd)), 1e-12)
    ro_ref[...] = r
    lwo_ref[...] = lw
    ko_ref[...] = k * (1.0 + (a - 1.0) * ka_ref[...])
    vo_ref[...] = v
    ao_ref[...] = -kk
    bo_ref[...] = kk * a


def _rwkv_prep(p, prev_rkv, prev_lo, mu_rkv, mu_lo, w0, w2, a0, a2, k_k, k_a,
               vres, *, seq, seq_len, tm, rw, lo_col):
    m = p.shape[0]
    lo_w = 2 * LORA_PAD
    lo_blk = lo_col // lo_w
    row = lambda i: (i, 0)
    col = lambda c: (lambda i: (i, c))
    const = lambda i: (0, 0)
    in_specs = [pl.BlockSpec((tm, rw), col(0)), pl.BlockSpec((tm, rw), col(1)),
                pl.BlockSpec((tm, rw), col(2)), pl.BlockSpec((tm, lo_w), col(lo_blk))]
    args = [p, p, p, p]
    if seq:
        tps = seq_len // tm
        for c in range(3):
            in_specs.append(pl.BlockSpec((None, 1, rw), lambda i, c=c: (i // tps, 0, c)))
        in_specs.append(pl.BlockSpec((None, 1, lo_w), lambda i: (i // tps, 0, 0)))
    else:
        tps = 1
        for c in range(3):
            in_specs.append(pl.BlockSpec((tm, rw), col(c)))
        in_specs.append(pl.BlockSpec((tm, lo_w), row))
    args += [prev_rkv, prev_rkv, prev_rkv, prev_lo]
    for c in range(3):
        in_specs.append(pl.BlockSpec((1, rw), lambda i, c=c: (0, c)))
    in_specs.append(pl.BlockSpec((1, lo_w), const))
    args += [mu_rkv, mu_rkv, mu_rkv, mu_lo]
    in_specs += [pl.BlockSpec((1, rw), const), pl.BlockSpec((LORA_PAD, rw), const),
                 pl.BlockSpec((1, rw), const), pl.BlockSpec((LORA_PAD, rw), const),
                 pl.BlockSpec((1, rw), const), pl.BlockSpec((1, rw), const)]
    args += [w0, w2, a0, a2, k_k, k_a]
    if vres is not None:
        v_first, v0, v1, v2 = vres
        in_specs += [pl.BlockSpec((tm, rw), row), pl.BlockSpec((1, rw), const),
                     pl.BlockSpec((rw, LORA_PAD), const), pl.BlockSpec((LORA_PAD, rw), const)]
        args += [v_first, v0, v1, v2]
    out = jax.ShapeDtypeStruct((m, rw), F32)
    scratch = [pltpu.VMEM((1, rw), F32)] * 3 + [pltpu.VMEM((1, lo_w), F32)] if seq else []
    return pl.pallas_call(
        functools.partial(_rwkv_prep_kernel, seq=seq, has_vres=vres is not None,
                          tiles_per_seq=tps),
        grid=(m // tm,),
        in_specs=in_specs,
        out_specs=[pl.BlockSpec((tm, rw), row)] * 6,
        out_shape=[out] * 6,
        scratch_shapes=scratch,
        compiler_params=_cparams(("arbitrary",)),
        name="rwkv_prep",
    )(*args)


def _wkv_chunk_kernel(r_ref, lw_ref, k_ref, v_ref, a_ref, b_ref, s0_ref, y_ref, s_ref):
    c_len = r_ref.shape[0]
    n_pairs = r_ref.shape[1] // PAIR

    @pl.when(pl.program_id(1) == 0)
    def _():
        s_ref[...] = s0_ref[...]

    lw = lw_ref[...]
    ti = lax.broadcasted_iota(jnp.int32, (c_len, c_len), 0)
    si = lax.broadcasted_iota(jnp.int32, (c_len, c_len), 1)
    tri = jnp.where(si <= ti, 1.0, 0.0).astype(BF16)
    x1 = lw.astype(BF16)
    r1 = lw - x1.astype(F32)
    x2 = r1.astype(BF16)
    x3 = (r1 - x2.astype(F32)).astype(BF16)
    lp = _dot(tri, x1) + _dot(tri, x2) + _dot(tri, x3)
    lp_last = lp[c_len - 1:, :]

    p = jnp.exp(lp)
    pinv = jnp.exp(-lp)
    a_t = (a_ref[...] * jnp.exp(lp - lw)).astype(BF16)
    r_t = (r_ref[...] * p).astype(BF16)
    b_t = b_ref[...] * pinv
    k_t = k_ref[...] * pinv
    pend = jnp.exp(lp_last - lp)
    b_e = (b_ref[...] * pend).astype(BF16)
    k_e = (k_ref[...] * pend).astype(BF16)
    p_last = p[c_len - 1:, :]
    v = v_ref[...]

    lane = lax.broadcasted_iota(jnp.int32, (c_len, PAIR), 1)
    rowi = lax.broadcasted_iota(jnp.int32, (c_len, PAIR), 0)
    lo_half = lane < HEAD
    scol = lane % HEAD
    strict = scol < rowi
    incl = scol <= rowi
    r2 = lax.broadcasted_iota(jnp.int32, (PAIR, PAIR), 0)
    c2 = lax.broadcasted_iota(jnp.int32, (PAIR, PAIR), 1)
    same_head = (r2 // HEAD) == (c2 // HEAD)
    eye = jnp.where(r2 == c2, 1.0, 0.0).astype(F32)

    def bd(z):
        return jnp.concatenate([jnp.where(lo_half, z, 0.0), jnp.where(lo_half, 0.0, z)], axis=0)

    for pi in range(n_pairs):
        sl = slice(pi * PAIR, (pi + 1) * PAIR)
        x = jnp.concatenate([a_t[:, sl], r_t[:, sl]], axis=0)
        gb = _dot_nt(x, bd(b_t[:, sl]).astype(BF16))
        gk = _dot_nt(x, bd(k_t[:, sl]).astype(BF16))
        s_old = s_ref[pi]
        xs = _dot_nt(x, s_old.astype(BF16))
        a_ab = jnp.where(strict, gb[:c_len], 0.0)
        q_b = jnp.where(incl, gb[c_len:], 0.0)
        gkm = jnp.concatenate([jnp.where(strict, gk[:c_len], 0.0),
                               jnp.where(incl, gk[c_len:], 0.0)], axis=0)
        v_p = v[:, sl]
        mv = _dot(gkm.astype(BF16), bd(v_p).astype(BF16))

        lpow = bd(a_ab)
        tinv = eye + lpow
        for _ in range(5):
            lb = lpow.astype(BF16)
            lpow = _dot(lb, lb)
            tinv = tinv + _dot(tinv.astype(BF16), lpow.astype(BF16))
        t_row = tinv[:c_len] + tinv[c_len:]

        u = _dot(t_row.astype(BF16), bd(xs[:c_len] + mv[:c_len]).astype(BF16))
        y = xs[c_len:] + mv[c_len:] + _dot(q_b.astype(BF16), bd(u).astype(BF16))
        y_ref[:, sl] = y

        uv = jnp.concatenate([u, v_p], axis=0).astype(BF16)
        bk = jnp.concatenate([b_e[:, sl], k_e[:, sl]], axis=0)
        upd = _dot_tn(uv, bk)
        s_ref[pi] = s_old * p_last[:, sl] + jnp.where(same_head, upd, 0.0)


def _wkv_chunk(r, lw, k, v, a, b, s0_bd, n_seq, seq_len):
    m, rw = r.shape
    n_chunks = seq_len // WKV_CHUNK
    n_pairs = rw // PAIR
    blk = pl.BlockSpec((WKV_CHUNK, rw), lambda n, c: (n * n_chunks + c, 0))
    sblk = pl.BlockSpec((None, n_pairs, PAIR, PAIR), lambda n, c: (n, 0, 0, 0))
    return pl.pallas_call(
        _wkv_chunk_kernel,
        grid=(n_seq, n_chunks),
        in_specs=[blk] * 6 + [sblk],
        out_specs=[blk, sblk],
        out_shape=[jax.ShapeDtypeStruct((m, rw), F32),
                   jax.ShapeDtypeStruct((n_seq, n_pairs, PAIR, PAIR), F32)],
        compiler_params=_cparams(("parallel", "arbitrary")),
        name="wkv_chunk",
    )(r, lw, k, v, a, b, s0_bd)


_STEP_ROWS = 16


def _wkv_step_kernel(r_ref, lw_ref, k_ref, v_ref, a_ref, b_ref, s_ref, y_ref, so_ref):
    nb = r_ref.shape[0]
    n_heads = r_ref.shape[1] // HEAD
    rows = lax.broadcasted_iota(jnp.int32, (_STEP_ROWS, HEAD), 0)

    def body(n, carry):
        def row(ref):
            return ref[pl.ds(n, 1), :]
        r, k, v, a, b = row(r_ref), row(k_ref), row(v_ref), row(a_ref), row(b_ref)
        w = jnp.exp(row(lw_ref))
        ys = []
        for h in range(n_heads):
            sl = slice(h * HEAD, (h + 1) * HEAD)
            s_old = s_ref[n, h]
            a16 = jnp.broadcast_to(a[:, sl], (_STEP_ROWS, HEAD)).astype(BF16)
            sa = _dot_nt(a16, s_old.astype(BF16))
            z = jnp.where(rows == 0, v[:, sl], jnp.where(rows == 1, sa, 0.0)).astype(BF16)
            wm = jnp.where(rows == 0, k[:, sl], jnp.where(rows == 1, b[:, sl], 0.0)).astype(BF16)
            s_new = s_old * w[:, sl] + _dot_tn(z, wm)
            so_ref[n, h] = s_new
            r16 = jnp.broadcast_to(r[:, sl], (_STEP_ROWS, HEAD)).astype(BF16)
            ys.append(_dot_nt(r16, s_new.astype(BF16))[:1])
        y_ref[pl.ds(n, 1), :] = jnp.concatenate(ys, axis=1)
        return carry

    lax.fori_loop(0, nb, body, 0)


def _wkv_step(r, lw, k, v, a, b, s, nb):
    m, rw = r.shape
    n_heads = rw // HEAD
    blk = pl.BlockSpec((nb, rw), lambda i: (i, 0))
    sblk = pl.BlockSpec((nb, n_heads, HEAD, HEAD), lambda i: (i, 0, 0, 0))
    return pl.pallas_call(
        _wkv_step_kernel,
        grid=(m // nb,),
        in_specs=[blk] * 6 + [sblk],
        out_specs=[blk, sblk],
        out_shape=[jax.ShapeDtypeStruct((m, rw), F32),
                   jax.ShapeDtypeStruct(s.shape, F32)],
        compiler_params=_cparams(("parallel",)),
        name="wkv_step",
    )(r, lw, k, v, a, b, s)


def _rwkv_post_kernel(y_ref, r_ref, k_ref, v_ref, z_ref, g_ref, b_ref, rk_ref, o_ref):
    ones_bd = _head_ones()
    y = y_ref[...]
    mu = _head_sum(y, ones_bd) * (1.0 / HEAD)
    d = y - mu
    var = _head_sum(d * d, ones_bd) * (1.0 / HEAD)
    yn = d * lax.rsqrt(var + LNX_EPS) * g_ref[...] + b_ref[...]
    bonus = _head_sum(r_ref[...] * k_ref[...] * rk_ref[...], ones_bd) * v_ref[...]
    z = z_ref[...]
    o_ref[...] = ((yn + bonus) * (z * jax.nn.sigmoid(z))).astype(o_ref.dtype)


def _rwkv_post(y, r, k, v, p, lnx_g, lnx_b, r_k, tm, z_blk):
    m, rw = y.shape
    row = pl.BlockSpec((tm, rw), lambda i: (i, 0))
    const = pl.BlockSpec((1, rw), lambda i: (0, 0))
    return pl.pallas_call(
        _rwkv_post_kernel,
        grid=(m // tm,),
        in_specs=[row, row, row, row, pl.BlockSpec((tm, rw), lambda i: (i, z_blk)),
                  const, const, const],
        out_specs=row,
        out_shape=jax.ShapeDtypeStruct((m, rw), BF16),
        compiler_params=_cparams(("parallel",)),
        name="rwkv_post",
    )(y, r, k, v, p, lnx_g, lnx_b, r_k)


def _gmlp_kernel(u_ref, vg_ref, z_ref, g_ref, b_ref, ws_ref, bias_ref, o_ref, vn_ref, *, seq):
    vg = vg_ref[...]
    mu = jnp.mean(vg, axis=-1, keepdims=True)
    d = vg - mu
    var = jnp.mean(d * d, axis=-1, keepdims=True)
    vn = d * lax.rsqrt(var + LN_EPS) * g_ref[...] + b_ref[...]
    vn_ref[...] = vn
    if seq:
        tm, gw = vg.shape[0], vg.shape[1] // GMLP_GROUPS
        ri = lax.broadcasted_iota(jnp.int32, (GMLP_CHUNK, GMLP_CHUNK), 0)
        ci = lax.broadcasted_iota(jnp.int32, (GMLP_CHUNK, GMLP_CHUNK), 1)
        vb = vn.astype(BF16)
        cols = []
        for g in range(GMLP_GROUPS):
            wg = jnp.where(ci <= ri, ws_ref[g], 0.0).astype(BF16)
            rows = [_dot(wg, vb[c * GMLP_CHUNK:(c + 1) * GMLP_CHUNK, g * gw:(g + 1) * gw])
                    for c in range(tm // GMLP_CHUNK)]
            cols.append(jnp.concatenate(rows, axis=0))
        mixed = jnp.concatenate(cols, axis=1)
        bias = jnp.concatenate([bias_ref[...]] * (tm // GMLP_CHUNK), axis=0)
        mixed = mixed + bias
    else:
        mixed = vn * ws_ref[...] + bias_ref[...]
    z = z_ref[...]
    o_ref[...] = (u_ref[...] * mixed * (z * jax.nn.sigmoid(z))).astype(o_ref.dtype)


def _gmlp(p, gln_g, gln_b, ws, bias, *, seq, tm, gw, u_blk):
    m = p.shape[0]
    col = lambda c: pl.BlockSpec((tm, gw), lambda i, c=c: (i, c))
    const = pl.BlockSpec((1, gw), lambda i: (0, 0))
    if seq:
        ws_spec = pl.BlockSpec(ws.shape, lambda i: (0, 0, 0))
        bias_spec = pl.BlockSpec(bias.shape, lambda i: (0, 0))
    else:
        ws_spec, bias_spec = const, const
    row = pl.BlockSpec((tm, gw), lambda i: (i, 0))
    return pl.pallas_call(
        functools.partial(_gmlp_kernel, seq=seq),
        grid=(m // tm,),
        in_specs=[col(u_blk), col(u_blk + 1), col(u_blk + 2), const, const, ws_spec, bias_spec],
        out_specs=[row, row],
        out_shape=[jax.ShapeDtypeStruct((m, gw), BF16), jax.ShapeDtypeStruct((m, gw), F32)],
        compiler_params=_cparams(("parallel",)),
        name="gmlp",
    )(p, p, p, gln_g, gln_b, ws, bias)


def _out_proj_kernel(or_ref, og_ref, w_ref, h_ref, *rest, final):
    rw = or_ref.shape[1]
    h = h_ref[...] + _dot(or_ref[...], w_ref[:rw, :]) + _dot(og_ref[...], w_ref[rw:, :])
    if final:
        fg_ref, o_ref = rest
        ms = jnp.mean(h * h, axis=-1, keepdims=True)
        o_ref[...] = h * lax.rsqrt(ms + RMS_EPS) * fg_ref[...]
    else:
        (o_ref,) = rest
        o_ref[...] = h


def _out_proj(o_r, o_g, w, h, final_g, tm):
    m, d = h.shape
    rw, gw = o_r.shape[1], o_g.shape[1]
    in_specs = [pl.BlockSpec((tm, rw), lambda i: (i, 0)),
                pl.BlockSpec((tm, gw), lambda i: (i, 0)),
                pl.BlockSpec((rw + gw, d), lambda i: (0, 0)),
                pl.BlockSpec((tm, d), lambda i: (i, 0))]
    args = [o_r, o_g, w, h]
    if final_g is not None:
        in_specs.append(pl.BlockSpec((1, d), lambda i: (0, 0)))
        args.append(final_g)
    return pl.pallas_call(
        functools.partial(_out_proj_kernel, final=final_g is not None),
        grid=(m // tm,),
        in_specs=in_specs,
        out_specs=pl.BlockSpec((tm, d), lambda i: (i, 0)),
        out_shape=jax.ShapeDtypeStruct((m, d), F32),
        compiler_params=_cparams(("parallel",)),
        name="out_proj",
    )(*args)


def _pad_last(x, width):
    return jnp.pad(x, [(0, 0)] * (x.ndim - 1) + [(0, width - x.shape[-1])])


def _pad_rows(x, rows):
    return jnp.pad(x, [(0, 0)] * (x.ndim - 2) + [(0, rows - x.shape[-2]), (0, 0)])


def _pick_tile(m, pref, mult):
    t = min(m, pref)
    while m % t or t % mult:
        t -= mult
    return t


def _trunk(x, s_init, shift_init, prm, *, seq):
    n_seq, seq_len, d = x.shape
    m = n_seq * seq_len
    depth = prm["w_in"].shape[0]
    rw, gw = prm["rw"], prm["gw"]
    dl, il = prm["dl"], prm["il"]
    n_heads = rw // HEAD
    lo_col = 3 * rw + rw + 3 * gw

    tm_mm = _pick_tile(m, 1024, 8)
    tn = 256
    tm_ew = _pick_tile(m, 256, GMLP_CHUNK if seq else 8)
    tm_out = _pick_tile(m, 512, 8)

    h = x.reshape(m, d)
    s_out, shift_out, vn_out = [], [], []
    v_first = None
    for l in range(depth):
        p = _in_proj(h, prm["norm_g"][l], prm["w_in"][l], tm_mm, tn)

        sh = shift_init[l]
        prev_rkv = sh[:, :3 * rw]
        prev_lo = jnp.concatenate([_pad_last(sh[:, 3 * rw:3 * rw + dl], LORA_PAD),
                                   _pad_last(sh[:, 3 * rw + dl:], LORA_PAD)], axis=-1)
        if seq:
            prev_rkv, prev_lo = prev_rkv[:, None, :], prev_lo[:, None, :]
        vres = None
        if l > 0:
            vres = (v_first, prm["v0"][l - 1], prm["v1"][l - 1], prm["v2"][l - 1])
        r, lw, k, v, a, b = _rwkv_prep(
            p, prev_rkv, prev_lo, prm["mu_rkv"][l], prm["mu_lo"][l], prm["w0"][l], prm["w2"][l],
            prm["a0"][l], prm["a2"][l], prm["k_k"][l], prm["k_a"][l], vres,
            seq=seq, seq_len=seq_len, tm=tm_ew, rw=rw, lo_col=lo_col)
        if l == 0:
            v_first = v

        if seq:
            s0 = s_init[l].reshape(n_seq, n_heads // 2, 2, HEAD, HEAD)
            z = jnp.zeros_like(s0[:, :, 0])
            s0_bd = jnp.concatenate([jnp.concatenate([s0[:, :, 0], z], axis=-1),
                                     jnp.concatenate([z, s0[:, :, 1]], axis=-1)], axis=-2)
            y, s_bd = _wkv_chunk(r, lw, k, v, a, b, s0_bd, n_seq, seq_len)
            s_new = jnp.stack([s_bd[:, :, :HEAD, :HEAD], s_bd[:, :, HEAD:, HEAD:]], axis=2)
            s_new = s_new.reshape(n_seq, n_heads, HEAD, HEAD)
        else:
            y, s_new = _wkv_step(r, lw, k, v, a, b, s_init[l], 8)
        s_out.append(s_new)

        o_r = _rwkv_post(y, r, k, v, p, prm["lnx_g"][l], prm["lnx_b"][l], prm["r_k"][l],
                         tm_ew, 3)
        if seq:
            ws, bias = prm["w_s"][l], prm["bias_seq"][l]
        else:
            ws, bias = prm["w_s0"][l], prm["bias0"][l]
        o_g, vn = _gmlp(p, prm["gln_g"][l], prm["gln_b"][l], ws, bias,
                        seq=seq, tm=tm_ew, gw=gw, u_blk=4)
        vn_out.append(vn.reshape(n_seq, seq_len, gw))

        last = p.reshape(n_seq, seq_len, -1)[:, -1]
        shift_out.append(jnp.concatenate(
            [last[:, :3 * rw], last[:, lo_col:lo_col + dl],
             last[:, lo_col + LORA_PAD:lo_col + LORA_PAD + il]], axis=-1))

        h = _out_proj(o_r, o_g, prm["w_out"][l], h,
                      prm["final_g"] if l == depth - 1 else None, tm_out)
    return h.reshape(n_seq, seq_len, d), s_out, shift_out, vn_out


def kernel(x_prompt, x_sample, state_wkv, state_shift, norm_g, w_in, mu_shift, w0, w2, a0, a2,
           k_k, k_a, r_k, lnx_g, lnx_b, v0, v1, v2, gln_g, gln_b, w_s, b_s, w_out, final_g):
    depth, d, _ = w_in.shape
    rw = w0.shape[1]
    gw = gln_g.shape[1]
    dl, il = w2.shape[1], a2.shape[1]
    assert rw % PAIR == 0 and gw == rw and w_s.shape[2] == GMLP_CHUNK
    assert gw // GMLP_GROUPS == LANES and dl <= LORA_PAD and il <= LORA_PAD
    sc = 3 * rw + dl + il

    def relayout_cols(t):
        return jnp.concatenate(
            [t[..., :3 * rw], t[..., sc:],
             _pad_last(t[..., 3 * rw:3 * rw + dl], LORA_PAD),
             _pad_last(t[..., 3 * rw + dl:sc], LORA_PAD)], axis=-1)

    row3 = lambda t: t[:, None, :]
    prm = dict(
        rw=rw, gw=gw, dl=dl, il=il,
        norm_g=row3(norm_g),
        w_in=relayout_cols(w_in).astype(BF16),
        mu_rkv=row3(mu_shift[:, :3 * rw]),
        mu_lo=row3(jnp.concatenate([_pad_last(mu_shift[:, 3 * rw:3 * rw + dl], LORA_PAD),
                                    _pad_last(mu_shift[:, 3 * rw + dl:], LORA_PAD)], axis=-1)),
        w0=row3(w0), w2=_pad_rows(w2, LORA_PAD).astype(BF16),
        a0=row3(a0), a2=_pad_rows(a2, LORA_PAD).astype(BF16),
        k_k=row3(k_k), k_a=row3(k_a), r_k=row3(r_k.reshape(depth, rw)),
        lnx_g=row3(lnx_g), lnx_b=row3(lnx_b),
        v0=row3(v0), v1=_pad_last(v1, LORA_PAD).astype(BF16), v2=_pad_rows(v2, LORA_PAD).astype(BF16),
        gln_g=row3(gln_g), gln_b=row3(gln_b),
        w_s=w_s,
        bias_seq=jnp.repeat(jnp.swapaxes(b_s, 1, 2), gw // GMLP_GROUPS, axis=2),
        w_s0=row3(jnp.repeat(w_s[:, :, 0, 0], gw // GMLP_GROUPS, axis=1)),
        bias0=row3(jnp.repeat(b_s[:, :, 0], gw // GMLP_GROUPS, axis=1)),
        w_out=w_out.astype(BF16),
        final_g=final_g[None, :],
    )

    nb = x_prompt.shape[0]
    n_heads = rw // HEAD
    s0_p = jnp.zeros((depth, nb, n_heads, HEAD, HEAD), F32)
    sh0_p = jnp.zeros((depth, nb, sc), F32)
    y_p, s_p, sh_p, _ = _trunk(x_prompt, s0_p, sh0_p, prm, seq=True)
    y_s, s_s, sh_s, vn_s = _trunk(x_sample, state_wkv, state_shift, prm, seq=False)
    return (y_p, y_s, jnp.stack(s_p), jnp.stack(sh_p), jnp.stack(s_s), jnp.stack(sh_s),
            jnp.stack(vn_s))
```

```python
import functools

import jax
import jax.numpy as jnp
from jax import lax
from jax.experimental import pallas as pl
from jax.experimental.pallas import tpu as pltpu

F32 = jnp.float32
BF16 = jnp.bfloat16

HEAD = 64
LANES = 128
PAIR = 2 * HEAD
WKV_CHUNK = HEAD
GMLP_CHUNK = 128
GMLP_GROUPS = 8
LORA_PAD = 128
RMS_EPS = 1e-6
LN_EPS = 1e-5
LNX_EPS = 64e-5
VMEM_LIMIT = 48 * 1024 * 1024

_NT = (((1,), (1,)), ((), ()))
_TN = (((0,), (0,)), ((), ()))


def _dot(a, b):
    return jnp.dot(a, b, preferred_element_type=F32)


def _dot_nt(a, b):
    return lax.dot_general(a, b, _NT, preferred_element_type=F32)


def _dot_tn(a, b):
    return lax.dot_general(a, b, _TN, preferred_element_type=F32)


def _cparams(sem):
    return pltpu.CompilerParams(dimension_semantics=sem, vmem_limit_bytes=VMEM_LIMIT)


def _head_ones():
    r = lax.broadcasted_iota(jnp.int32, (LANES, LANES), 0) // HEAD
    c = lax.broadcasted_iota(jnp.int32, (LANES, LANES), 1) // HEAD
    return jnp.where(r == c, 1.0, 0.0).astype(BF16)


def _head_sum(x, ones_bd):
    outs = []
    for j in range(x.shape[1] // LANES):
        xs = x[:, j * LANES:(j + 1) * LANES]
        hi = xs.astype(BF16)
        lo = (xs - hi.astype(F32)).astype(BF16)
        outs.append(_dot(hi, ones_bd) + _dot(lo, ones_bd))
    return jnp.concatenate(outs, axis=1)


def _in_proj_kernel(h_ref, g_ref, w_ref, o_ref, xn_ref):
    @pl.when(pl.program_id(1) == 0)
    def _():
        x = h_ref[...]
        ms = jnp.mean(x * x, axis=-1, keepdims=True)
        xn_ref[...] = (x * lax.rsqrt(ms + RMS_EPS) * g_ref[...]).astype(BF16)

    o_ref[...] = _dot_nt(xn_ref[...], w_ref[...])


_ROW_ALIGN = 16


def _in_proj(h, g, wt_rows, layer, tm, tn, rw, gw, lora_w):
    m, d = h.shape
    in_cols = 3 * rw + lora_w + rw + 3 * gw
    n_rkv = 3 * rw // tn
    n_main = n_rkv + (rw + 3 * gw) // tn
    assert (3 * rw) % tn == 0 and (rw + 3 * gw) % tn == 0 and lora_w <= tn
    assert in_cols % _ROW_ALIGN == 0 and tn % _ROW_ALIGN == 0 and lora_w % _ROW_ALIGN == 0
    base = layer * in_cols // _ROW_ALIGN
    step = tn // _ROW_ALIGN
    z0 = (3 * rw + lora_w) // _ROW_ALIGN
    lo0 = 3 * rw // _ROW_ALIGN

    def w_index(i, j):
        start = jnp.where(j < n_rkv, step * j,
                          jnp.where(j < n_main, z0 + step * (j - n_rkv), lo0))
        return ((base + start) * _ROW_ALIGN, 0)

    return pl.pallas_call(
        _in_proj_kernel,
        grid=(m // tm, n_main + 1),
        in_specs=[
            pl.BlockSpec((tm, d), lambda i, j: (i, 0)),
            pl.BlockSpec((1, d), lambda i, j: (0, 0)),
            pl.BlockSpec((pl.Element(tn), pl.Element(d)), w_index),
        ],
        out_specs=pl.BlockSpec((tm, tn), lambda i, j: (i, j)),
        out_shape=jax.ShapeDtypeStruct((m, (n_main + 1) * tn), F32),
        scratch_shapes=[pltpu.VMEM((tm, d), BF16)],
        compiler_params=_cparams(("parallel", "arbitrary")),
        name="in_proj",
    )(h, g, wt_rows)


def _shift_seq(x, mu, init_row, carry_ref, first):
    prev0 = jnp.where(first, init_row, carry_ref[...])
    sh = pltpu.roll(x, 1, 0)
    rows = lax.broadcasted_iota(jnp.int32, x.shape, 0)
    sh = jnp.where(rows == 0, prev0, sh)
    carry_ref[...] = x[x.shape[0] - 1:, :]
    return x + (sh - x) * mu


def _rwkv_prep_kernel(*refs, seq, has_vres, tiles_per_seq):
    (r_ref, k_ref, v_ref, lo_ref, pr_ref, pk_ref, pv_ref, plo_ref,
     mur_ref, muk_ref, muv_ref, mulo_ref, w0_ref, w2_ref, a0_ref, a2_ref,
     kk_ref, ka_ref) = refs[:18]
    pos = 18
    if has_vres:
        vf_ref, v0_ref, v1_ref, v2_ref = refs[pos:pos + 4]
        pos += 4
    (ro_ref, lwo_ref, ko_ref, vo_ref, ao_ref, bo_ref) = refs[pos:pos + 6]
    pos += 6

    if seq:
        cr_ref, ck_ref, cv_ref, clo_ref = refs[pos:pos + 4]
        first = (pl.program_id(0) % tiles_per_seq) == 0
        r = _shift_seq(r_ref[...], mur_ref[...], pr_ref[...], cr_ref, first)
        k = _shift_seq(k_ref[...], muk_ref[...], pk_ref[...], ck_ref, first)
        v = _shift_seq(v_ref[...], muv_ref[...], pv_ref[...], cv_ref, first)
        lo = _shift_seq(lo_ref[...], mulo_ref[...], plo_ref[...], clo_ref, first)
    else:
        def mix(x_ref, p_ref, mu_ref):
            x = x_ref[...]
            return x + (p_ref[...] - x) * mu_ref[...]
        r = mix(r_ref, pr_ref, mur_ref)
        k = mix(k_ref, pk_ref, muk_ref)
        v = mix(v_ref, pv_ref, muv_ref)
        lo = mix(lo_ref, plo_ref, mulo_ref)

    wraw = w0_ref[...] + _dot(jnp.tanh(lo).astype(BF16), w2_ref[...])
    w = -jax.nn.softplus(-wraw) - 0.5
    lw = -jnp.exp(w)
    a = jax.nn.sigmoid(a0_ref[...] + _dot(lo.astype(BF16), a2_ref[...]))
    if has_vres:
        vv = _dot(_dot(v.astype(BF16), v1_ref[...]).astype(BF16), v2_ref[...])
        v = v + (vf_ref[...] - v) * jax.nn.sigmoid(v0_ref[...] + vv)
    ones_bd = _head_ones()
    kk = k * kk_ref[...]
    kk = kk / jnp.maximum(jnp.sqrt(_head_sum(kk * kk, ones_bd)), 1e-12)
    ro_ref[...] = r
    lwo_ref[...] = lw
    ko_ref[...] = k * (1.0 + (a - 1.0) * ka_ref[...])
    vo_ref[...] = v
    ao_ref[...] = -kk
    bo_ref[...] = kk * a


def _rwkv_prep(p, prev_rkv, prev_lo, mu_rkv, mu_lo, w0, w2, a0, a2, k_k, k_a,
               vres, *, seq, seq_len, tm, rw, lo_col, lo_w):
    m = p.shape[0]
    lo_blk = lo_col // lo_w
    row = lambda i: (i, 0)
    col = lambda c: (lambda i: (i, c))
    const = lambda i: (0, 0)
    in_specs = [pl.BlockSpec((tm, rw), col(0)), pl.BlockSpec((tm, rw), col(1)),
                pl.BlockSpec((tm, rw), col(2)), pl.BlockSpec((tm, lo_w), col(lo_blk))]
    args = [p, p, p, p]
    if seq:
        tps = seq_len // tm
        for c in range(3):
            in_specs.append(pl.BlockSpec((None, 1, rw), lambda i, c=c: (i // tps, 0, c)))
        in_specs.append(pl.BlockSpec((None, 1, lo_w), lambda i: (i // tps, 0, 0)))
    else:
        tps = 1
        for c in range(3):
            in_specs.append(pl.BlockSpec((tm, rw), col(c)))
        in_specs.append(pl.BlockSpec((tm, lo_w), row))
    args += [prev_rkv, prev_rkv, prev_rkv, prev_lo]
    for c in range(3):
        in_specs.append(pl.BlockSpec((1, rw), lambda i, c=c: (0, c)))
    in_specs.append(pl.BlockSpec((1, lo_w), const))
    args += [mu_rkv, mu_rkv, mu_rkv, mu_lo]
    in_specs += [pl.BlockSpec((1, rw), const), pl.BlockSpec((lo_w, rw), const),
                 pl.BlockSpec((1, rw), const), pl.BlockSpec((lo_w, rw), const),
                 pl.BlockSpec((1, rw), const), pl.BlockSpec((1, rw), const)]
    args += [w0, w2, a0, a2, k_k, k_a]
    if vres is not None:
        v_first, v0, v1, v2 = vres
        in_specs += [pl.BlockSpec((tm, rw), row), pl.BlockSpec((1, rw), const),
                     pl.BlockSpec((rw, LORA_PAD), const), pl.BlockSpec((LORA_PAD, rw), const)]
        args += [v_first, v0, v1, v2]
    out = jax.ShapeDtypeStruct((m, rw), F32)
    scratch = [pltpu.VMEM((1, rw), F32)] * 3 + [pltpu.VMEM((1, lo_w), F32)] if seq else []
    return pl.pallas_call(
        functools.partial(_rwkv_prep_kernel, seq=seq, has_vres=vres is not None,
                          tiles_per_seq=tps),
        grid=(m // tm,),
        in_specs=in_specs,
        out_specs=[pl.BlockSpec((tm, rw), row)] * 6,
        out_shape=[out] * 6,
        scratch_shapes=scratch,
        compiler_params=_cparams(("arbitrary",)),
        name="rwkv_prep",
    )(*args)


def _wkv_chunk_kernel(r_ref, lw_ref, k_ref, v_ref, a_ref, b_ref, s0_ref, y_ref, s_ref):
    c_len = r_ref.shape[0]
    n_pairs = r_ref.shape[1] // PAIR

    @pl.when(pl.program_id(1) == 0)
    def _():
        s_ref[...] = s0_ref[...]

    lw = lw_ref[...]
    ti = lax.broadcasted_iota(jnp.int32, (c_len, c_len), 0)
    si = lax.broadcasted_iota(jnp.int32, (c_len, c_len), 1)
    tri = jnp.where(si <= ti, 1.0, 0.0).astype(BF16)
    x1 = lw.astype(BF16)
    r1 = lw - x1.astype(F32)
    x2 = r1.astype(BF16)
    x3 = (r1 - x2.astype(F32)).astype(BF16)
    lp = _dot(tri, x1) + _dot(tri, x2) + _dot(tri, x3)
    lp_last = lp[c_len - 1:, :]

    p = jnp.exp(lp)
    pinv = jnp.exp(-lp)
    a_t = (a_ref[...] * jnp.exp(lp - lw)).astype(BF16)
    r_t = (r_ref[...] * p).astype(BF16)
    b_t = b_ref[...] * pinv
    k_t = k_ref[...] * pinv
    pend = jnp.exp(lp_last - lp)
    b_e = (b_ref[...] * pend).astype(BF16)
    k_e = (k_ref[...] * pend).astype(BF16)
    p_last = p[c_len - 1:, :]
    v = v_ref[...]

    lane = lax.broadcasted_iota(jnp.int32, (c_len, PAIR), 1)
    rowi = lax.broadcasted_iota(jnp.int32, (c_len, PAIR), 0)
    lo_half = lane < HEAD
    scol = lane % HEAD
    strict = scol < rowi
    incl = scol <= rowi
    r2 = lax.broadcasted_iota(jnp.int32, (PAIR, PAIR), 0)
    c2 = lax.broadcasted_iota(jnp.int32, (PAIR, PAIR), 1)
    same_head = (r2 // HEAD) == (c2 // HEAD)
    eye = jnp.where(r2 == c2, 1.0, 0.0).astype(F32)

    def bd(z):
        return jnp.concatenate([jnp.where(lo_half, z, 0.0), jnp.where(lo_half, 0.0, z)], axis=0)

    pairs = range(n_pairs)
    sls = [slice(pi * PAIR, (pi + 1) * PAIR) for pi in pairs]
    s_old = [s_ref[pi] for pi in pairs]
    g = []
    for pi in pairs:
        sl = sls[pi]
        x = jnp.concatenate([a_t[:, sl], r_t[:, sl]], axis=0)
        rhs = jnp.concatenate([bd(b_t[:, sl]).astype(BF16), bd(k_t[:, sl]).astype(BF16),
                               s_old[pi].astype(BF16)], axis=0)
        g.append(_dot_nt(x, rhs))
    xs = [gi[:, 2 * PAIR:] for gi in g]
    q_b = [jnp.where(incl, gi[c_len:, :PAIR], 0.0) for gi in g]
    mv = []
    for pi in pairs:
        gk = g[pi][:, PAIR:2 * PAIR]
        gkm = jnp.concatenate([jnp.where(strict, gk[:c_len], 0.0),
                               jnp.where(incl, gk[c_len:], 0.0)], axis=0)
        mv.append(_dot(gkm.astype(BF16), bd(v[:, sls[pi]]).astype(BF16)))

    pw = [bd(jnp.where(strict, gi[:c_len, :PAIR], 0.0)) for gi in g]
    tinv = [eye + pj for pj in pw]
    pw = [_dot(pj.astype(BF16), pj.astype(BF16)) for pj in pw]
    n_sq = (c_len - 1).bit_length() - 1
    for j in range(n_sq):
        last = j == n_sq - 1
        for pi in pairs:
            pb = pw[pi].astype(BF16)
            if last:
                tinv[pi] = tinv[pi] + _dot(pb, tinv[pi].astype(BF16))
            else:
                res = _dot(pb, jnp.concatenate([tinv[pi].astype(BF16), pb], axis=1))
                tinv[pi] = tinv[pi] + res[:, :PAIR]
                pw[pi] = res[:, PAIR:]

    u = []
    for pi in pairs:
        t_row = tinv[pi][:c_len] + tinv[pi][c_len:]
        u.append(_dot(t_row.astype(BF16), bd(xs[pi][:c_len] + mv[pi][:c_len]).astype(BF16)))
    for pi in pairs:
        sl = sls[pi]
        y_ref[:, sl] = (xs[pi][c_len:] + mv[pi][c_len:]
                        + _dot(q_b[pi].astype(BF16), bd(u[pi]).astype(BF16)))
        uv = jnp.concatenate([u[pi], v[:, sl]], axis=0).astype(BF16)
        bk = jnp.concatenate([b_e[:, sl], k_e[:, sl]], axis=0)
        upd = _dot_tn(uv, bk)
        s_ref[pi] = s_old[pi] * p_last[:, sl] + jnp.where(same_head, upd, 0.0)


def _wkv_chunk(r, lw, k, v, a, b, s0_bd, n_seq, seq_len):
    m, rw = r.shape
    n_chunks = seq_len // WKV_CHUNK
    n_pairs = rw // PAIR
    blk = pl.BlockSpec((WKV_CHUNK, rw), lambda n, c: (n * n_chunks + c, 0))
    sblk = pl.BlockSpec((None, n_pairs, PAIR, PAIR), lambda n, c: (n, 0, 0, 0))
    return pl.pallas_call(
        _wkv_chunk_kernel,
        grid=(n_seq, n_chunks),
        in_specs=[blk] * 6 + [sblk],
        out_specs=[blk, sblk],
        out_shape=[jax.ShapeDtypeStruct((m, rw), F32),
                   jax.ShapeDtypeStruct((n_seq, n_pairs, PAIR, PAIR), F32)],
        compiler_params=_cparams(("parallel", "arbitrary")),
        name="wkv_chunk",
    )(r, lw, k, v, a, b, s0_bd)


def _wkv_step_kernel(r_ref, lw_ref, k_ref, v_ref, a_ref, b_ref, s_ref, _, y_ref, so_ref,
                     vt_ref, yt_ref):
    r_t, k_t, a_t, b_t = r_ref[...].T, k_ref[...].T, a_ref[...].T, b_ref[...].T
    w_t = jnp.exp(lw_ref[...]).T
    vt_ref[...] = v_ref[...].T
    for hh in range(PAIR // HEAD):
        sl = slice(hh * HEAD, (hh + 1) * HEAD)
        r, k, a, b, w = r_t[sl], k_t[sl], a_t[sl], b_t[sl], w_t[sl]

        def body(vi, carry, hh=hh, r=r, k=k, a=a, b=b, w=w):
            s = s_ref[hh, vi]
            sa = jnp.sum(s * a, axis=0, keepdims=True)
            s_new = s * w + sa * b + vt_ref[pl.ds(hh * HEAD + vi, 1), :] * k
            so_ref[hh, vi] = s_new
            yt_ref[pl.ds(hh * HEAD + vi, 1), :] = jnp.sum(s_new * r, axis=0, keepdims=True)
            return carry

        lax.fori_loop(0, HEAD, body, 0, unroll=8)
    y_ref[...] = yt_ref[...].T


def _wkv_step(r, lw, k, v, a, b, s_all, s_out_all, layer):
    m, rw = r.shape
    hp = PAIR // HEAD
    assert m % LANES == 0
    blk = pl.BlockSpec((LANES, PAIR), lambda pi, j: (j, pi))
    sblk = pl.BlockSpec((None, hp, HEAD, HEAD, LANES), lambda pi, j: (layer, pi, 0, 0, j))
    return pl.pallas_call(
        _wkv_step_kernel,
        grid=(rw // PAIR, m // LANES),
        in_specs=[blk] * 6 + [sblk, pl.BlockSpec(memory_space=pl.ANY)],
        out_specs=[blk, sblk],
        out_shape=[jax.ShapeDtypeStruct((m, rw), F32),
                   jax.ShapeDtypeStruct(s_out_all.shape, F32)],
        scratch_shapes=[pltpu.VMEM((PAIR, LANES), F32), pltpu.VMEM((PAIR, LANES), F32)],
        input_output_aliases={7: 1},
        compiler_params=_cparams(("parallel", "parallel")),
        name="wkv_step",
    )(r, lw, k, v, a, b, s_all, s_out_all)


def _rwkv_post_kernel(y_ref, r_ref, k_ref, v_ref, z_ref, g_ref, b_ref, rk_ref, o_ref):
    ones_bd = _head_ones()
    y = y_ref[...]
    mu = _head_sum(y, ones_bd) * (1.0 / HEAD)
    d = y - mu
    var = _head_sum(d * d, ones_bd) * (1.0 / HEAD)
    yn = d * lax.rsqrt(var + LNX_EPS) * g_ref[...] + b_ref[...]
    bonus = _head_sum(r_ref[...] * k_ref[...] * rk_ref[...], ones_bd) * v_ref[...]
    z = z_ref[...]
    o_ref[...] = ((yn + bonus) * (z * jax.nn.sigmoid(z))).astype(o_ref.dtype)


def _rwkv_post(y, r, k, v, p, lnx_g, lnx_b, r_k, tm, z_blk):
    m, rw = y.shape
    row = pl.BlockSpec((tm, rw), lambda i: (i, 0))
    const = pl.BlockSpec((1, rw), lambda i: (0, 0))
    return pl.pallas_call(
        _rwkv_post_kernel,
        grid=(m // tm,),
        in_specs=[row, row, row, row, pl.BlockSpec((tm, rw), lambda i: (i, z_blk)),
                  const, const, const],
        out_specs=row,
        out_shape=jax.ShapeDtypeStruct((m, rw), BF16),
        compiler_params=_cparams(("parallel",)),
        name="rwkv_post",
    )(y, r, k, v, p, lnx_g, lnx_b, r_k)


def _gmlp_kernel(u_ref, vg_ref, z_ref, g_ref, b_ref, ws_ref, bias_ref, o_ref, vn_ref, *, seq):
    vg = vg_ref[...]
    mu = jnp.mean(vg, axis=-1, keepdims=True)
    d = vg - mu
    var = jnp.mean(d * d, axis=-1, keepdims=True)
    vn = d * lax.rsqrt(var + LN_EPS) * g_ref[...] + b_ref[...]
    vn_ref[...] = vn
    if seq:
        tm, gw = vg.shape[0], vg.shape[1] // GMLP_GROUPS
        ri = lax.broadcasted_iota(jnp.int32, (GMLP_CHUNK, GMLP_CHUNK), 0)
        ci = lax.broadcasted_iota(jnp.int32, (GMLP_CHUNK, GMLP_CHUNK), 1)
        vb = vn.astype(BF16)
        cols = []
        for g in range(GMLP_GROUPS):
            wg = jnp.where(ci <= ri, ws_ref[g], 0.0).astype(BF16)
            rows = [_dot(wg, vb[c * GMLP_CHUNK:(c + 1) * GMLP_CHUNK, g * gw:(g + 1) * gw])
                    for c in range(tm // GMLP_CHUNK)]
            cols.append(jnp.concatenate(rows, axis=0))
        mixed = jnp.concatenate(cols, axis=1)
        bias = jnp.concatenate([bias_ref[...]] * (tm // GMLP_CHUNK), axis=0)
        mixed = mixed + bias
    else:
        mixed = vn * ws_ref[...] + bias_ref[...]
    z = z_ref[...]
    o_ref[...] = (u_ref[...] * mixed * (z * jax.nn.sigmoid(z))).astype(o_ref.dtype)


def _gmlp(p, gln_g, gln_b, ws, bias, *, seq, tm, gw, u_blk):
    m = p.shape[0]
    col = lambda c: pl.BlockSpec((tm, gw), lambda i, c=c: (i, c))
    const = pl.BlockSpec((1, gw), lambda i: (0, 0))
    if seq:
        ws_spec = pl.BlockSpec(ws.shape, lambda i: (0, 0, 0))
        bias_spec = pl.BlockSpec(bias.shape, lambda i: (0, 0))
    else:
        ws_spec, bias_spec = const, const
    row = pl.BlockSpec((tm, gw), lambda i: (i, 0))
    return pl.pallas_call(
        functools.partial(_gmlp_kernel, seq=seq),
        grid=(m // tm,),
        in_specs=[col(u_blk), col(u_blk + 1), col(u_blk + 2), const, const, ws_spec, bias_spec],
        out_specs=[row, row],
        out_shape=[jax.ShapeDtypeStruct((m, gw), BF16), jax.ShapeDtypeStruct((m, gw), F32)],
        compiler_params=_cparams(("parallel",)),
        name="gmlp",
    )(p, p, p, gln_g, gln_b, ws, bias)


def _out_proj_kernel(or_ref, og_ref, w_ref, h_ref, *rest, final):
    rw = or_ref.shape[1]
    h = h_ref[...] + _dot(or_ref[...], w_ref[:rw, :]) + _dot(og_ref[...], w_ref[rw:, :])
    if final:
        fg_ref, o_ref = rest
        ms = jnp.mean(h * h, axis=-1, keepdims=True)
        o_ref[...] = h * lax.rsqrt(ms + RMS_EPS) * fg_ref[...]
    else:
        (o_ref,) = rest
        o_ref[...] = h


def _out_proj(o_r, o_g, w_all, layer, h, final_g, tm):
    m, d = h.shape
    rw, gw = o_r.shape[1], o_g.shape[1]
    in_specs = [pl.BlockSpec((tm, rw), lambda i: (i, 0)),
                pl.BlockSpec((tm, gw), lambda i: (i, 0)),
                pl.BlockSpec((None, rw + gw, d), lambda i: (layer, 0, 0)),
                pl.BlockSpec((tm, d), lambda i: (i, 0))]
    args = [o_r, o_g, w_all, h]
    if final_g is not None:
        in_specs.append(pl.BlockSpec((1, d), lambda i: (0, 0)))
        args.append(final_g)
    return pl.pallas_call(
        functools.partial(_out_proj_kernel, final=final_g is not None),
        grid=(m // tm,),
        in_specs=in_specs,
        out_specs=pl.BlockSpec((tm, d), lambda i: (i, 0)),
        out_shape=jax.ShapeDtypeStruct((m, d), F32),
        compiler_params=_cparams(("parallel",)),
        name="out_proj",
    )(*args)


def _pad_last(x, width):
    return jnp.pad(x, [(0, 0)] * (x.ndim - 1) + [(0, width - x.shape[-1])])


def _pad_rows(x, rows):
    return jnp.pad(x, [(0, 0)] * (x.ndim - 2) + [(0, rows - x.shape[-2]), (0, 0)])


def _pick_tile(m, pref, mult):
    t = min(m, pref)
    while m % t or t % mult:
        t -= mult
    return t


def _trunk(x, s_init, shift_init, prm, *, seq):
    n_seq, seq_len, d = x.shape
    m = n_seq * seq_len
    depth = prm["depth"]
    rw, gw = prm["rw"], prm["gw"]
    lora_w = prm["lora_w"]
    n_heads = rw // HEAD
    lo_col = 3 * rw + rw + 3 * gw
    tn = prm["tn"]

    tm_mm = _pick_tile(m, 1024, 8)
    tm_ew = _pick_tile(m, 256, GMLP_CHUNK if seq else 8)
    tm_out = _pick_tile(m, 512, 8)

    h = x.reshape(m, d)
    s_out, shift_out, vn_out = [], [], []
    if not seq:
        s_init = jnp.transpose(s_init, (0, 2, 3, 4, 1))
        s_buf = jnp.zeros_like(s_init)
    v_first = None
    for l in range(depth):
        p = _in_proj(h, prm["norm_g"][l], prm["w_in"], l, tm_mm, tn, rw, gw, lora_w)

        sh = shift_init[l]
        prev_rkv = sh[:, :3 * rw]
        prev_lo = _pad_last(sh[:, 3 * rw:], tn)
        if seq:
            prev_rkv, prev_lo = prev_rkv[:, None, :], prev_lo[:, None, :]
        vres = None
        if l > 0:
            vres = (v_first, prm["v0"][l - 1], prm["v1"][l - 1], prm["v2"][l - 1])
        r, lw, k, v, a, b = _rwkv_prep(
            p, prev_rkv, prev_lo, prm["mu_rkv"][l], prm["mu_lo"][l], prm["w0"][l], prm["w2"][l],
            prm["a0"][l], prm["a2"][l], prm["k_k"][l], prm["k_a"][l], vres,
            seq=seq, seq_len=seq_len, tm=tm_ew, rw=rw, lo_col=lo_col, lo_w=tn)
        if l == 0:
            v_first = v

        if seq:
            s0 = s_init[l].reshape(n_seq, n_heads // 2, 2, HEAD, HEAD)
            z = jnp.zeros_like(s0[:, :, 0])
            s0_bd = jnp.concatenate([jnp.concatenate([s0[:, :, 0], z], axis=-1),
                                     jnp.concatenate([z, s0[:, :, 1]], axis=-1)], axis=-2)
            y, s_bd = _wkv_chunk(r, lw, k, v, a, b, s0_bd, n_seq, seq_len)
            s_new = jnp.stack([s_bd[:, :, :HEAD, :HEAD], s_bd[:, :, HEAD:, HEAD:]], axis=2)
            s_out.append(s_new.reshape(n_seq, n_heads, HEAD, HEAD))
        else:
            y, s_buf = _wkv_step(r, lw, k, v, a, b, s_init, s_buf, l)

        o_r = _rwkv_post(y, r, k, v, p, prm["lnx_g"][l], prm["lnx_b"][l], prm["r_k"][l],
                         tm_ew, 3)
        if seq:
            ws, bias = prm["w_s"][l], prm["bias_seq"][l]
        else:
            ws, bias = prm["w_s0"][l], prm["bias0"][l]
        o_g, vn = _gmlp(p, prm["gln_g"][l], prm["gln_b"][l], ws, bias,
                        seq=seq, tm=tm_ew, gw=gw, u_blk=4)
        vn_out.append(vn.reshape(n_seq, seq_len, gw))

        last = p.reshape(n_seq, seq_len, -1)[:, -1]
        shift_out.append(jnp.concatenate(
            [last[:, :3 * rw], last[:, lo_col:lo_col + lora_w]], axis=-1))

        h = _out_proj(o_r, o_g, prm["w_out"], l, h,
                      prm["final_g"] if l == depth - 1 else None, tm_out)
    s_final = jnp.stack(s_out) if seq else jnp.transpose(s_buf, (0, 4, 1, 2, 3))
    return h.reshape(n_seq, seq_len, d), s_final, shift_out, vn_out


def kernel(x_prompt, x_sample, state_wkv, state_shift, norm_g, w_in, mu_shift, w0, w2, a0, a2,
           k_k, k_a, r_k, lnx_g, lnx_b, v0, v1, v2, gln_g, gln_b, w_s, b_s, w_out, final_g):
    depth, d, _ = w_in.shape
    rw = w0.shape[1]
    gw = gln_g.shape[1]
    dl, il = w2.shape[1], a2.shape[1]
    assert rw % PAIR == 0 and gw == rw and w_s.shape[2] == GMLP_CHUNK
    assert gw // GMLP_GROUPS == LANES
    sc = 3 * rw + dl + il
    tn = 256
    in_cols = w_in.shape[2]

    row3 = lambda t: t[:, None, :]
    prm = dict(
        depth=depth, rw=rw, gw=gw, lora_w=dl + il, tn=tn,
        norm_g=row3(norm_g),
        w_in=jnp.swapaxes(w_in, 1, 2).astype(BF16).reshape(depth * in_cols, d),
        mu_rkv=row3(mu_shift[:, :3 * rw]),
        mu_lo=row3(_pad_last(mu_shift[:, 3 * rw:], tn)),
        w0=row3(w0), w2=_pad_rows(w2, tn).astype(BF16),
        a0=row3(a0),
        a2=jnp.pad(a2, ((0, 0), (dl, tn - dl - il), (0, 0))).astype(BF16),
        k_k=row3(k_k), k_a=row3(k_a), r_k=row3(r_k.reshape(depth, rw)),
        lnx_g=row3(lnx_g), lnx_b=row3(lnx_b),
        v0=row3(v0), v1=_pad_last(v1, LORA_PAD).astype(BF16), v2=_pad_rows(v2, LORA_PAD).astype(BF16),
        gln_g=row3(gln_g), gln_b=row3(gln_b),
        w_s=w_s,
        bias_seq=jnp.repeat(jnp.swapaxes(b_s, 1, 2), gw // GMLP_GROUPS, axis=2),
        w_s0=row3(jnp.repeat(w_s[:, :, 0, 0], gw // GMLP_GROUPS, axis=1)),
        bias0=row3(jnp.repeat(b_s[:, :, 0], gw // GMLP_GROUPS, axis=1)),
        w_out=w_out.astype(BF16),
        final_g=final_g[None, :],
    )

    nb = x_prompt.shape[0]
    n_heads = rw // HEAD
    s0_p = jnp.zeros((depth, nb, n_heads, HEAD, HEAD), F32)
    sh0_p = jnp.zeros((depth, nb, sc), F32)
    y_p, s_p, sh_p, _ = _trunk(x_prompt, s0_p, sh0_p, prm, seq=True)
    y_s, s_s, sh_s, vn_s = _trunk(x_sample, state_wkv, state_shift, prm, seq=False)
    return (y_p, y_s, s_p, jnp.stack(sh_p), s_s, jnp.stack(sh_s), jnp.stack(vn_s))
```

```python
import functools

import jax
import jax.numpy as jnp
from jax import lax
from jax.experimental import pallas as pl
from jax.experimental.pallas import tpu as pltpu

F32 = jnp.float32
BF16 = jnp.bfloat16

HEAD = 64
LANES = 128
PAIR = 2 * HEAD
WKV_CHUNK = HEAD
GMLP_CHUNK = 128
GMLP_GROUPS = 8
LORA_PAD = 128
RMS_EPS = 1e-6
LN_EPS = 1e-5
LNX_EPS = 64e-5
VMEM_LIMIT = 56 * 1024 * 1024

_NT = (((1,), (1,)), ((), ()))
_TN = (((0,), (0,)), ((), ()))


def _dot(a, b):
    return jnp.dot(a, b, preferred_element_type=F32)


def _dot_nt(a, b):
    return lax.dot_general(a, b, _NT, preferred_element_type=F32)


def _dot_tn(a, b):
    return lax.dot_general(a, b, _TN, preferred_element_type=F32)


def _cparams(sem):
    return pltpu.CompilerParams(dimension_semantics=sem, vmem_limit_bytes=VMEM_LIMIT)


def _head_ones():
    r = lax.broadcasted_iota(jnp.int32, (LANES, LANES), 0) // HEAD
    c = lax.broadcasted_iota(jnp.int32, (LANES, LANES), 1) // HEAD
    return jnp.where(r == c, 1.0, 0.0).astype(BF16)


def _head_sum(x, ones_bd):
    outs = []
    for j in range(x.shape[1] // LANES):
        xs = x[:, j * LANES:(j + 1) * LANES]
        hi = xs.astype(BF16)
        lo = (xs - hi.astype(F32)).astype(BF16)
        outs.append(_dot(hi, ones_bd) + _dot(lo, ones_bd))
    return jnp.concatenate(outs, axis=1)


def _in_proj_kernel(h_ref, g_ref, w_ref, o_ref, xn_ref):
    @pl.when(pl.program_id(1) == 0)
    def _():
        x = h_ref[...]
        ms = jnp.mean(x * x, axis=-1, keepdims=True)
        xn_ref[...] = (x * lax.rsqrt(ms + RMS_EPS) * g_ref[...]).astype(BF16)

    o_ref[...] = _dot_nt(xn_ref[...], w_ref[...])


_ROW_ALIGN = 16


def _in_proj(h, g, wt_rows, layer, tm, tn, rw, gw, lora_w):
    m, d = h.shape
    in_cols = 3 * rw + lora_w + rw + 3 * gw
    n_rkv = 3 * rw // tn
    n_main = n_rkv + (rw + 3 * gw) // tn
    assert (3 * rw) % tn == 0 and (rw + 3 * gw) % tn == 0 and lora_w <= tn
    assert in_cols % _ROW_ALIGN == 0 and tn % _ROW_ALIGN == 0 and lora_w % _ROW_ALIGN == 0
    base = layer * in_cols // _ROW_ALIGN
    step = tn // _ROW_ALIGN
    z0 = (3 * rw + lora_w) // _ROW_ALIGN
    lo0 = 3 * rw // _ROW_ALIGN

    def w_index(i, j):
        start = jnp.where(j < n_rkv, step * j,
                          jnp.where(j < n_main, z0 + step * (j - n_rkv), lo0))
        return ((base + start) * _ROW_ALIGN, 0)

    return pl.pallas_call(
        _in_proj_kernel,
        grid=(m // tm, n_main + 1),
        in_specs=[
            pl.BlockSpec((tm, d), lambda i, j: (i, 0)),
            pl.BlockSpec((1, d), lambda i, j: (0, 0)),
            pl.BlockSpec((pl.Element(tn), pl.Element(d)), w_index),
        ],
        out_specs=pl.BlockSpec((tm, tn), lambda i, j: (i, j)),
        out_shape=jax.ShapeDtypeStruct((m, (n_main + 1) * tn), F32),
        scratch_shapes=[pltpu.VMEM((tm, d), BF16)],
        compiler_params=_cparams(("parallel", "arbitrary")),
        name="in_proj",
    )(h, g, wt_rows)


def _shift_seq(x, mu, init_row, carry_ref, first):
    prev0 = jnp.where(first, init_row, carry_ref[...])
    sh = pltpu.roll(x, 1, 0)
    rows = lax.broadcasted_iota(jnp.int32, x.shape, 0)
    sh = jnp.where(rows == 0, prev0, sh)
    carry_ref[...] = x[x.shape[0] - 1:, :]
    return x + (sh - x) * mu


def _rwkv_prep_kernel(*refs, seq, has_vres, tiles_per_seq):
    (r_ref, k_ref, v_ref, lo_ref, pr_ref, pk_ref, pv_ref, plo_ref,
     mur_ref, muk_ref, muv_ref, mulo_ref, w0_ref, w2_ref, a0_ref, a2_ref,
     kk_ref, ka_ref) = refs[:18]
    pos = 18
    if has_vres:
        vf_ref, v0_ref, v1_ref, v2_ref = refs[pos:pos + 4]
        pos += 4
    (ro_ref, lwo_ref, ko_ref, vo_ref, ao_ref, bo_ref) = refs[pos:pos + 6]
    pos += 6

    if seq:
        cr_ref, ck_ref, cv_ref, clo_ref = refs[pos:pos + 4]
        first = (pl.program_id(0) % tiles_per_seq) == 0
        r = _shift_seq(r_ref[...], mur_ref[...], pr_ref[...], cr_ref, first)
        k = _shift_seq(k_ref[...], muk_ref[...], pk_ref[...], ck_ref, first)
        v = _shift_seq(v_ref[...], muv_ref[...], pv_ref[...], cv_ref, first)
        lo = _shift_seq(lo_ref[...], mulo_ref[...], plo_ref[...], clo_ref, first)
    else:
        def mix(x_ref, p_ref, mu_ref):
            x = x_ref[...]
            return x + (p_ref[...] - x) * mu_ref[...]
        r = mix(r_ref, pr_ref, mur_ref)
        k = mix(k_ref, pk_ref, muk_ref)
        v = mix(v_ref, pv_ref, muv_ref)
        lo = mix(lo_ref, plo_ref, mulo_ref)

    wraw = w0_ref[...] + _dot(jnp.tanh(lo).astype(BF16), w2_ref[...])
    w = -jax.nn.softplus(-wraw) - 0.5
    lw = -jnp.exp(w)
    a = jax.nn.sigmoid(a0_ref[...] + _dot(lo.astype(BF16), a2_ref[...]))
    if has_vres:
        vv = _dot(_dot(v.astype(BF16), v1_ref[...]).astype(BF16), v2_ref[...])
        v = v + (vf_ref[...] - v) * jax.nn.sigmoid(v0_ref[...] + vv)
    ones_bd = _head_ones()
    kk = k * kk_ref[...]
    kk = kk / jnp.maximum(jnp.sqrt(_head_sum(kk * kk, ones_bd)), 1e-12)
    ro_ref[...] = r
    lwo_ref[...] = lw
    ko_ref[...] = k * (1.0 + (a - 1.0) * ka_ref[...])
    vo_ref[...] = v
    ao_ref[...] = -kk
    bo_ref[...] = kk * a


def _rwkv_prep(p, prev_rkv, prev_lo, mu_rkv, mu_lo, w0, w2, a0, a2, k_k, k_a,
               vres, *, seq, seq_len, tm, rw, lo_col, lo_w):
    m = p.shape[0]
    lo_blk = lo_col // lo_w
    row = lambda i: (i, 0)
    col = lambda c: (lambda i: (i, c))
    const = lambda i: (0, 0)
    in_specs = [pl.BlockSpec((tm, rw), col(0)), pl.BlockSpec((tm, rw), col(1)),
                pl.BlockSpec((tm, rw), col(2)), pl.BlockSpec((tm, lo_w), col(lo_blk))]
    args = [p, p, p, p]
    if seq:
        tps = seq_len // tm
        for c in range(3):
            in_specs.append(pl.BlockSpec((None, 1, rw), lambda i, c=c: (i // tps, 0, c)))
        in_specs.append(pl.BlockSpec((None, 1, lo_w), lambda i: (i // tps, 0, 0)))
    else:
        tps = 1
        for c in range(3):
            in_specs.append(pl.BlockSpec((tm, rw), col(c)))
        in_specs.append(pl.BlockSpec((tm, lo_w), row))
    args += [prev_rkv, prev_rkv, prev_rkv, prev_lo]
    for c in range(3):
        in_specs.append(pl.BlockSpec((1, rw), lambda i, c=c: (0, c)))
    in_specs.append(pl.BlockSpec((1, lo_w), const))
    args += [mu_rkv, mu_rkv, mu_rkv, mu_lo]
    in_specs += [pl.BlockSpec((1, rw), const), pl.BlockSpec((lo_w, rw), const),
                 pl.BlockSpec((1, rw), const), pl.BlockSpec((lo_w, rw), const),
                 pl.BlockSpec((1, rw), const), pl.BlockSpec((1, rw), const)]
    args += [w0, w2, a0, a2, k_k, k_a]
    if vres is not None:
        v_first, v0, v1, v2 = vres
        in_specs += [pl.BlockSpec((tm, rw), row), pl.BlockSpec((1, rw), const),
                     pl.BlockSpec((rw, LORA_PAD), const), pl.BlockSpec((LORA_PAD, rw), const)]
        args += [v_first, v0, v1, v2]
    out = jax.ShapeDtypeStruct((m, rw), F32)
    scratch = [pltpu.VMEM((1, rw), F32)] * 3 + [pltpu.VMEM((1, lo_w), F32)] if seq else []
    return pl.pallas_call(
        functools.partial(_rwkv_prep_kernel, seq=seq, has_vres=vres is not None,
                          tiles_per_seq=tps),
        grid=(m // tm,),
        in_specs=in_specs,
        out_specs=[pl.BlockSpec((tm, rw), row)] * 6,
        out_shape=[out] * 6,
        scratch_shapes=scratch,
        compiler_params=_cparams(("arbitrary",)),
        name="rwkv_prep",
    )(*args)


def _wkv_chunk_kernel(r_ref, lw_ref, k_ref, v_ref, a_ref, b_ref, s0_ref, y_ref, s_ref):
    c_len = r_ref.shape[0]
    n_pairs = r_ref.shape[1] // PAIR

    @pl.when(pl.program_id(1) == 0)
    def _():
        s_ref[...] = s0_ref[...]

    lw = lw_ref[...]
    ti = lax.broadcasted_iota(jnp.int32, (c_len, c_len), 0)
    si = lax.broadcasted_iota(jnp.int32, (c_len, c_len), 1)
    tri = jnp.where(si <= ti, 1.0, 0.0).astype(BF16)
    x1 = lw.astype(BF16)
    x2 = (lw - x1.astype(F32)).astype(BF16)
    lp = _dot(tri, x1) + _dot(tri, x2)
    lp_last = lp[c_len - 1:, :]

    p = jnp.exp(lp)
    pinv = jnp.exp(-lp)
    a_t = (a_ref[...] * jnp.exp(lp - lw)).astype(BF16)
    r_t = (r_ref[...] * p).astype(BF16)
    b_t = b_ref[...] * pinv
    k_t = k_ref[...] * pinv
    pend = jnp.exp(lp_last - lp)
    b_e = (b_ref[...] * pend).astype(BF16)
    k_e = (k_ref[...] * pend).astype(BF16)
    p_last = p[c_len - 1:, :]
    v = v_ref[...]

    lane = lax.broadcasted_iota(jnp.int32, (c_len, PAIR), 1)
    rowi = lax.broadcasted_iota(jnp.int32, (c_len, PAIR), 0)
    lo_half = lane < HEAD
    scol = lane % HEAD
    strict = scol < rowi
    incl = scol <= rowi
    r2 = lax.broadcasted_iota(jnp.int32, (PAIR, PAIR), 0)
    c2 = lax.broadcasted_iota(jnp.int32, (PAIR, PAIR), 1)
    same_head = (r2 // HEAD) == (c2 // HEAD)
    eye_pair = jnp.where(scol == rowi, 1.0, 0.0).astype(F32)

    def bd(z):
        return jnp.concatenate([jnp.where(lo_half, z, 0.0), jnp.where(lo_half, 0.0, z)], axis=0)

    pairs = range(n_pairs)
    sls = [slice(pi * PAIR, (pi + 1) * PAIR) for pi in pairs]
    s_old = [s_ref[pi] for pi in pairs]
    g = []
    for pi in pairs:
        sl = sls[pi]
        x = jnp.concatenate([a_t[:, sl], r_t[:, sl]], axis=0)
        rhs = jnp.concatenate([bd(b_t[:, sl]).astype(BF16), bd(k_t[:, sl]).astype(BF16),
                               s_old[pi].astype(BF16)], axis=0)
        g.append(_dot_nt(x, rhs))
    xs = [gi[:, 2 * PAIR:] for gi in g]
    q_b = [jnp.where(incl, gi[c_len:, :PAIR], 0.0) for gi in g]
    mv = []
    for pi in pairs:
        gk = g[pi][:, PAIR:2 * PAIR]
        gkm = jnp.concatenate([jnp.where(strict, gk[:c_len], 0.0),
                               jnp.where(incl, gk[c_len:], 0.0)], axis=0)
        mv.append(_dot(gkm.astype(BF16), bd(v[:, sls[pi]]).astype(BF16)))

    pw = [jnp.where(strict, gi[:c_len, :PAIR], 0.0) for gi in g]
    tinv = [eye_pair + pj for pj in pw]
    pw = [_dot(pj.astype(BF16), bd(pj).astype(BF16)) for pj in pw]
    n_sq = (c_len - 1).bit_length() - 1
    for j in range(n_sq):
        last = j == n_sq - 1
        for pi in pairs:
            pb = pw[pi].astype(BF16)
            if last:
                tinv[pi] = tinv[pi] + _dot(pb, bd(tinv[pi]).astype(BF16))
            else:
                res = _dot(pb, jnp.concatenate([bd(tinv[pi]), bd(pw[pi])], axis=1).astype(BF16))
                tinv[pi] = tinv[pi] + res[:, :PAIR]
                pw[pi] = res[:, PAIR:]

    u = []
    for pi in pairs:
        u.append(_dot(tinv[pi].astype(BF16), bd(xs[pi][:c_len] + mv[pi][:c_len]).astype(BF16)))
    for pi in pairs:
        sl = sls[pi]
        y_ref[:, sl] = (xs[pi][c_len:] + mv[pi][c_len:]
                        + _dot(q_b[pi].astype(BF16), bd(u[pi]).astype(BF16)))
        uv = jnp.concatenate([u[pi], v[:, sl]], axis=0).astype(BF16)
        bk = jnp.concatenate([b_e[:, sl], k_e[:, sl]], axis=0)
        upd = _dot_tn(uv, bk)
        s_ref[pi] = s_old[pi] * p_last[:, sl] + jnp.where(same_head, upd, 0.0)


def _wkv_chunk(r, lw, k, v, a, b, s0_bd, n_seq, seq_len):
    m, rw = r.shape
    n_chunks = seq_len // WKV_CHUNK
    n_pairs = rw // PAIR
    blk = pl.BlockSpec((WKV_CHUNK, rw), lambda n, c: (n * n_chunks + c, 0))
    sblk = pl.BlockSpec((None, n_pairs, PAIR, PAIR), lambda n, c: (n, 0, 0, 0))
    return pl.pallas_call(
        _wkv_chunk_kernel,
        grid=(n_seq, n_chunks),
        in_specs=[blk] * 6 + [sblk],
        out_specs=[blk, sblk],
        out_shape=[jax.ShapeDtypeStruct((m, rw), F32),
                   jax.ShapeDtypeStruct((n_seq, n_pairs, PAIR, PAIR), F32)],
        compiler_params=_cparams(("parallel", "arbitrary")),
        name="wkv_chunk",
    )(r, lw, k, v, a, b, s0_bd)


def _wkv_step_kernel(r_ref, lw_ref, k_ref, v_ref, a_ref, b_ref, s_ref, _, y_ref, so_ref,
                     vt_ref, yt_ref):
    r_t, k_t, a_t, b_t = r_ref[...].T, k_ref[...].T, a_ref[...].T, b_ref[...].T
    w_t = jnp.exp(lw_ref[...]).T
    vt_ref[...] = v_ref[...].T
    for hh in range(PAIR // HEAD):
        sl = slice(hh * HEAD, (hh + 1) * HEAD)
        r, k, a, b, w = r_t[sl], k_t[sl], a_t[sl], b_t[sl], w_t[sl]

        def body(vi, carry, hh=hh, r=r, k=k, a=a, b=b, w=w):
            s = s_ref[hh, vi]
            sa = jnp.sum(s * a, axis=0, keepdims=True)
            s_new = s * w + sa * b + vt_ref[pl.ds(hh * HEAD + vi, 1), :] * k
            so_ref[hh, vi] = s_new
            yt_ref[pl.ds(hh * HEAD + vi, 1), :] = jnp.sum(s_new * r, axis=0, keepdims=True)
            return carry

        lax.fori_loop(0, HEAD, body, 0, unroll=8)
    y_ref[...] = yt_ref[...].T


def _wkv_step(r, lw, k, v, a, b, s_all, s_out_all, layer):
    m, rw = r.shape
    hp = PAIR // HEAD
    assert m % LANES == 0
    blk = pl.BlockSpec((LANES, PAIR), lambda pi, j: (j, pi))
    sblk = pl.BlockSpec((None, hp, HEAD, HEAD, LANES), lambda pi, j: (layer, pi, 0, 0, j))
    return pl.pallas_call(
        _wkv_step_kernel,
        grid=(rw // PAIR, m // LANES),
        in_specs=[blk] * 6 + [sblk, pl.BlockSpec(memory_space=pl.ANY)],
        out_specs=[blk, sblk],
        out_shape=[jax.ShapeDtypeStruct((m, rw), F32),
                   jax.ShapeDtypeStruct(s_out_all.shape, F32)],
        scratch_shapes=[pltpu.VMEM((PAIR, LANES), F32), pltpu.VMEM((PAIR, LANES), F32)],
        input_output_aliases={7: 1},
        compiler_params=_cparams(("parallel", "parallel")),
        name="wkv_step",
    )(r, lw, k, v, a, b, s_all, s_out_all)


def _silu(z):
    return z * jax.nn.sigmoid(z)


def _rwkv_gate(y, r, k, v, z, g, b, rk):
    ones_bd = _head_ones()
    mu = _head_sum(y, ones_bd) * (1.0 / HEAD)
    d = y - mu
    var = _head_sum(d * d, ones_bd) * (1.0 / HEAD)
    yn = d * lax.rsqrt(var + LNX_EPS) * g + b
    bonus = _head_sum(r * k * rk, ones_bd) * v
    return (yn + bonus) * _silu(z)


def _gmlp_mix(vg, g, b, ws_ref, bias_ref, seq):
    mu = jnp.mean(vg, axis=-1, keepdims=True)
    d = vg - mu
    var = jnp.mean(d * d, axis=-1, keepdims=True)
    vn = d * lax.rsqrt(var + LN_EPS) * g + b
    if not seq:
        return vn * ws_ref[...] + bias_ref[...], vn
    tm, gw = vg.shape[0], vg.shape[1] // GMLP_GROUPS
    ri = lax.broadcasted_iota(jnp.int32, (GMLP_CHUNK, GMLP_CHUNK), 0)
    ci = lax.broadcasted_iota(jnp.int32, (GMLP_CHUNK, GMLP_CHUNK), 1)
    vb = vn.astype(BF16)
    cols = []
    for gi in range(GMLP_GROUPS):
        wg = jnp.where(ci <= ri, ws_ref[gi], 0.0).astype(BF16)
        rows = [_dot(wg, vb[c * GMLP_CHUNK:(c + 1) * GMLP_CHUNK, gi * gw:(gi + 1) * gw])
                for c in range(tm // GMLP_CHUNK)]
        cols.append(jnp.concatenate(rows, axis=0))
    bias = jnp.concatenate([bias_ref[...]] * (tm // GMLP_CHUNK), axis=0)
    return jnp.concatenate(cols, axis=1) + bias, vn


def _mix_out_kernel(y_ref, r_ref, k_ref, v_ref, zr_ref, u_ref, vg_ref, zg_ref, h_ref,
                    lg_ref, lb_ref, rk_ref, gg_ref, gb_ref, ws_ref, bias_ref, w_ref, *rest,
                    seq, final):
    rw = y_ref.shape[1]
    o_r = _rwkv_gate(y_ref[...], r_ref[...], k_ref[...], v_ref[...], zr_ref[...],
                     lg_ref[...], lb_ref[...], rk_ref[...])
    mixed, vn = _gmlp_mix(vg_ref[...], gg_ref[...], gb_ref[...], ws_ref, bias_ref, seq)
    o_g = u_ref[...] * mixed * _silu(zg_ref[...])
    h = (h_ref[...] + _dot(o_r.astype(BF16), w_ref[:rw, :])
         + _dot(o_g.astype(BF16), w_ref[rw:, :]))
    if final:
        ms = jnp.mean(h * h, axis=-1, keepdims=True)
        h = h * lax.rsqrt(ms + RMS_EPS) * rest[0][...]
    outs = rest[1:] if final else rest
    outs[0][...] = h
    if not seq:
        outs[1][...] = vn


def _mix_out(y, r, k, v, p, h, lnx_g, lnx_b, r_k, gln_g, gln_b, ws, bias, w_all, layer,
             final_g, *, seq, tm, z_blk):
    m, rw = y.shape
    d = h.shape[1]
    gw = gln_g.shape[1]
    row = pl.BlockSpec((tm, rw), lambda i: (i, 0))
    col = lambda c: pl.BlockSpec((tm, rw), lambda i, c=c: (i, c))
    hrow = pl.BlockSpec((tm, d), lambda i: (i, 0))
    const = pl.BlockSpec((1, rw), lambda i: (0, 0))
    if seq:
        ws_spec = pl.BlockSpec(ws.shape, lambda i: (0, 0, 0))
        bias_spec = pl.BlockSpec(bias.shape, lambda i: (0, 0))
    else:
        ws_spec, bias_spec = const, const
    in_specs = [row, row, row, row, col(z_blk), col(z_blk + 1), col(z_blk + 2), col(z_blk + 3),
                hrow, const, const, const, const, const, ws_spec, bias_spec,
                pl.BlockSpec((None, rw + gw, d), lambda i: (layer, 0, 0),
                             pipeline_mode=pl.Buffered(1))]
    args = [y, r, k, v, p, p, p, p, h, lnx_g, lnx_b, r_k, gln_g, gln_b, ws, bias, w_all]
    if final_g is not None:
        in_specs.append(pl.BlockSpec((1, d), lambda i: (0, 0)))
        args.append(final_g)
    out_specs = [hrow]
    out_shape = [jax.ShapeDtypeStruct((m, d), F32)]
    if not seq:
        out_specs.append(pl.BlockSpec((tm, gw), lambda i: (i, 0)))
        out_shape.append(jax.ShapeDtypeStruct((m, gw), F32))
    outs = pl.pallas_call(
        functools.partial(_mix_out_kernel, seq=seq, final=final_g is not None),
        grid=(m // tm,),
        in_specs=in_specs,
        out_specs=out_specs,
        out_shape=out_shape,
        compiler_params=_cparams(("parallel",)),
        name="mix_out",
    )(*args)
    return (outs[0], None) if seq else (outs[0], outs[1])


def _pad_last(x, width):
    return jnp.pad(x, [(0, 0)] * (x.ndim - 1) + [(0, width - x.shape[-1])])


def _pad_rows(x, rows):
    return jnp.pad(x, [(0, 0)] * (x.ndim - 2) + [(0, rows - x.shape[-2]), (0, 0)])


def _pick_tile(m, pref, mult):
    t = min(m, pref)
    while m % t or t % mult:
        t -= mult
    return t


def _trunk(x, s_init, shift_init, prm, *, seq):
    n_seq, seq_len, d = x.shape
    m = n_seq * seq_len
    depth = prm["depth"]
    rw, gw = prm["rw"], prm["gw"]
    lora_w = prm["lora_w"]
    n_heads = rw // HEAD
    lo_col = 3 * rw + rw + 3 * gw
    tn = prm["tn"]

    tm_mm = _pick_tile(m, 1024, 8)
    tm_ew = _pick_tile(m, 256, GMLP_CHUNK if seq else 8)

    h = x.reshape(m, d)
    s_out, shift_out, vn_out = [], [], []
    if not seq:
        s_init = jnp.transpose(s_init, (0, 2, 3, 4, 1))
        s_buf = jnp.zeros_like(s_init)
    v_first = None
    for l in range(depth):
        p = _in_proj(h, prm["norm_g"][l], prm["w_in"], l, tm_mm, tn, rw, gw, lora_w)

        sh = shift_init[l]
        prev_rkv = sh[:, :3 * rw]
        prev_lo = _pad_last(sh[:, 3 * rw:], tn)
        if seq:
            prev_rkv, prev_lo = prev_rkv[:, None, :], prev_lo[:, None, :]
        vres = None
        if l > 0:
            vres = (v_first, prm["v0"][l - 1], prm["v1"][l - 1], prm["v2"][l - 1])
        r, lw, k, v, a, b = _rwkv_prep(
            p, prev_rkv, prev_lo, prm["mu_rkv"][l], prm["mu_lo"][l], prm["w0"][l], prm["w2"][l],
            prm["a0"][l], prm["a2"][l], prm["k_k"][l], prm["k_a"][l], vres,
            seq=seq, seq_len=seq_len, tm=tm_ew, rw=rw, lo_col=lo_col, lo_w=tn)
        if l == 0:
            v_first = v

        if seq:
            s0 = s_init[l].reshape(n_seq, n_heads // 2, 2, HEAD, HEAD)
            z = jnp.zeros_like(s0[:, :, 0])
            s0_bd = jnp.concatenate([jnp.concatenate([s0[:, :, 0], z], axis=-1),
                                     jnp.concatenate([z, s0[:, :, 1]], axis=-1)], axis=-2)
            y, s_bd = _wkv_chunk(r, lw, k, v, a, b, s0_bd, n_seq, seq_len)
            s_new = jnp.stack([s_bd[:, :, :HEAD, :HEAD], s_bd[:, :, HEAD:, HEAD:]], axis=2)
            s_out.append(s_new.reshape(n_seq, n_heads, HEAD, HEAD))
        else:
            y, s_buf = _wkv_step(r, lw, k, v, a, b, s_init, s_buf, l)

        last = p.reshape(n_seq, seq_len, -1)[:, -1]
        shift_out.append(jnp.concatenate(
            [last[:, :3 * rw], last[:, lo_col:lo_col + lora_w]], axis=-1))

        if seq:
            ws, bias = prm["w_s"][l], prm["bias_seq"][l]
        else:
            ws, bias = prm["w_s0"][l], prm["bias0"][l]
        h, vn = _mix_out(y, r, k, v, p, h, prm["lnx_g"][l], prm["lnx_b"][l], prm["r_k"][l],
                         prm["gln_g"][l], prm["gln_b"][l], ws, bias, prm["w_out"], l,
                         prm["final_g"] if l == depth - 1 else None,
                         seq=seq, tm=tm_ew, z_blk=3)
        if not seq:
            vn_out.append(vn.reshape(n_seq, seq_len, gw))
    s_final = jnp.stack(s_out) if seq else jnp.transpose(s_buf, (0, 4, 1, 2, 3))
    return h.reshape(n_seq, seq_len, d), s_final, shift_out, vn_out


def kernel(x_prompt, x_sample, state_wkv, state_shift, norm_g, w_in, mu_shift, w0, w2, a0, a2,
           k_k, k_a, r_k, lnx_g, lnx_b, v0, v1, v2, gln_g, gln_b, w_s, b_s, w_out, final_g):
    depth, d, _ = w_in.shape
    rw = w0.shape[1]
    gw = gln_g.shape[1]
    dl, il = w2.shape[1], a2.shape[1]
    assert rw % PAIR == 0 and gw == rw and w_s.shape[2] == GMLP_CHUNK
    assert gw // GMLP_GROUPS == LANES
    sc = 3 * rw + dl + il
    tn = 512
    in_cols = w_in.shape[2]

    row3 = lambda t: t[:, None, :]
    prm = dict(
        depth=depth, rw=rw, gw=gw, lora_w=dl + il, tn=tn,
        norm_g=row3(norm_g),
        w_in=jnp.swapaxes(w_in, 1, 2).astype(BF16).reshape(depth * in_cols, d),
        mu_rkv=row3(mu_shift[:, :3 * rw]),
        mu_lo=row3(_pad_last(mu_shift[:, 3 * rw:], tn)),
        w0=row3(w0), w2=_pad_rows(w2, tn).astype(BF16),
        a0=row3(a0),
        a2=jnp.pad(a2, ((0, 0), (dl, tn - dl - il), (0, 0))).astype(BF16),
        k_k=row3(k_k), k_a=row3(k_a), r_k=row3(r_k.reshape(depth, rw)),
        lnx_g=row3(lnx_g), lnx_b=row3(lnx_b),
        v0=row3(v0), v1=_pad_last(v1, LORA_PAD).astype(BF16), v2=_pad_rows(v2, LORA_PAD).astype(BF16),
        gln_g=row3(gln_g), gln_b=row3(gln_b),
        w_s=w_s,
        bias_seq=jnp.repeat(jnp.swapaxes(b_s, 1, 2), gw // GMLP_GROUPS, axis=2),
        w_s0=row3(jnp.repeat(w_s[:, :, 0, 0], gw // GMLP_GROUPS, axis=1)),
        bias0=row3(jnp.repeat(b_s[:, :, 0], gw // GMLP_GROUPS, axis=1)),
        w_out=w_out.astype(BF16),
        final_g=final_g[None, :],
    )

    nb = x_prompt.shape[0]
    n_heads = rw // HEAD
    s0_p = jnp.zeros((depth, nb, n_heads, HEAD, HEAD), F32)
    sh0_p = jnp.zeros((depth, nb, sc), F32)
    y_p, s_p, sh_p, _ = _trunk(x_prompt, s0_p, sh0_p, prm, seq=True)
    y_s, s_s, sh_s, vn_s = _trunk(x_sample, state_wkv, state_shift, prm, seq=False)
    return (y_p, y_s, s_p, jnp.stack(sh_p), s_s, jnp.stack(sh_s), jnp.stack(vn_s))
```

```python
import functools

import jax
import jax.numpy as jnp
from jax import lax
from jax.experimental import pallas as pl
from jax.experimental.pallas import tpu as pltpu

F32 = jnp.float32
BF16 = jnp.bfloat16

HEAD = 64
LANES = 128
PAIR = 2 * HEAD
WKV_CHUNK = HEAD
SEQ_ROWS = 2 * WKV_CHUNK
GMLP_CHUNK = 128
GMLP_GROUPS = 8
LORA_PAD = 128
LO_W = 256
RMS_EPS = 1e-6
LN_EPS = 1e-5
LNX_EPS = 64e-5
VMEM_LIMIT = 56 * 1024 * 1024

_NT = (((1,), (1,)), ((), ()))
_TN = (((0,), (0,)), ((), ()))


def _dot(a, b):
    return jnp.dot(a, b, preferred_element_type=F32)


def _dot_nt(a, b):
    return lax.dot_general(a, b, _NT, preferred_element_type=F32)


def _dot_tn(a, b):
    return lax.dot_general(a, b, _TN, preferred_element_type=F32)


def _cparams(sem):
    return pltpu.CompilerParams(dimension_semantics=sem, vmem_limit_bytes=VMEM_LIMIT)


def _silu(z):
    return z * jax.nn.sigmoid(z)


def _head_ones():
    r = lax.broadcasted_iota(jnp.int32, (LANES, LANES), 0) // HEAD
    c = lax.broadcasted_iota(jnp.int32, (LANES, LANES), 1) // HEAD
    return jnp.where(r == c, 1.0, 0.0).astype(BF16)


def _split_hi_lo(x):
    hi = x.astype(BF16)
    return hi, (x - hi.astype(F32)).astype(BF16)


def _dot_hi_lo(hi_lo, mat, mat_left):
    hi, lo = hi_lo
    return (_dot(mat, hi) + _dot(mat, lo)) if mat_left else (_dot(hi, mat) + _dot(lo, mat))


def _head_sum(x, ones_bd):
    outs = [_dot_hi_lo(_split_hi_lo(x[:, j * LANES:(j + 1) * LANES]), ones_bd, False)
            for j in range(x.shape[1] // LANES)]
    return jnp.concatenate(outs, axis=1)


def _chunk_tri(rows):
    ti = lax.broadcasted_iota(jnp.int32, (rows, rows), 0)
    si = lax.broadcasted_iota(jnp.int32, (rows, rows), 1)
    same_chunk = si // WKV_CHUNK == ti // WKV_CHUNK
    return jnp.where((si <= ti) & same_chunk, 1.0, 0.0).astype(BF16)


def _decay_and_iclr(wraw, araw):
    w = -jax.nn.softplus(-wraw) - 0.5
    return -jnp.exp(w), jax.nn.sigmoid(araw)


def _in_proj_kernel(h_ref, g_ref, w_ref, o_ref, xn_ref):
    @pl.when(pl.program_id(1) == 0)
    def _():
        x = h_ref[...]
        ms = jnp.mean(x * x, axis=-1, keepdims=True)
        xn_ref[...] = (x * lax.rsqrt(ms + RMS_EPS) * g_ref[...]).astype(BF16)

    o_ref[...] = _dot_nt(xn_ref[...], w_ref[...])


_ROW_ALIGN = 16


def _in_proj(h, g, wt_rows, layer, tm, tn, rw, gw, lora_w):
    m, d = h.shape
    in_cols = 3 * rw + lora_w + rw + 3 * gw
    n_rkv = 3 * rw // tn
    n_main = n_rkv + (rw + 3 * gw) // tn
    assert (3 * rw) % tn == 0 and (rw + 3 * gw) % tn == 0 and lora_w <= tn
    assert in_cols % _ROW_ALIGN == 0 and tn % _ROW_ALIGN == 0 and lora_w % _ROW_ALIGN == 0
    base = layer * in_cols // _ROW_ALIGN
    step = tn // _ROW_ALIGN
    z0 = (3 * rw + lora_w) // _ROW_ALIGN
    lo0 = 3 * rw // _ROW_ALIGN

    def w_index(i, j):
        start = jnp.where(j < n_rkv, step * j,
                          jnp.where(j < n_main, z0 + step * (j - n_rkv), lo0))
        return ((base + start) * _ROW_ALIGN, 0)

    return pl.pallas_call(
        _in_proj_kernel,
        grid=(m // tm, n_main + 1),
        in_specs=[
            pl.BlockSpec((tm, d), lambda i, j: (i, 0)),
            pl.BlockSpec((1, d), lambda i, j: (0, 0)),
            pl.BlockSpec((pl.Element(tn), pl.Element(d)), w_index),
        ],
        out_specs=pl.BlockSpec((tm, tn), lambda i, j: (i, j)),
        out_shape=jax.ShapeDtypeStruct((m, (n_main + 1) * tn), F32),
        scratch_shapes=[pltpu.VMEM((tm, d), BF16)],
        compiler_params=_cparams(("parallel", "arbitrary")),
        name="in_proj",
    )(h, g, wt_rows)


def _rwkv_tokens(k, v, tanh_lo, lo, w0, w2, a0, a2, k_k, k_a, vres):
    lw, a = _decay_and_iclr(w0 + _dot(tanh_lo, w2[...]), a0 + _dot(lo, a2[...]))
    if vres is not None:
        v_first, v0, vv_mid, v2 = vres
        v = v + (v_first - v) * jax.nn.sigmoid(v0 + _dot(vv_mid, v2[...]))
    kk = k * k_k
    kk = kk / jnp.maximum(jnp.sqrt(_head_sum(kk * kk, _head_ones())), 1e-12)
    return lw, k * (1.0 + (a - 1.0) * k_a), v, -kk, kk * a


def _rwkv_gate(y, bonus, z, g, b):
    ones_bd = _head_ones()
    mu = _head_sum(y, ones_bd) * (1.0 / HEAD)
    d = y - mu
    var = _head_sum(d * d, ones_bd) * (1.0 / HEAD)
    return (d * lax.rsqrt(var + LNX_EPS) * g + b + bonus) * _silu(z)


def _bonus(r, k, v, rk):
    return _head_sum(r * k * rk, _head_ones()) * v


def _rwkv_prep_kernel(*refs, has_vres):
    (r_ref, k_ref, v_ref, lo_ref, pr_ref, pk_ref, pv_ref, plo_ref,
     mur_ref, muk_ref, muv_ref, mulo_ref, w0_ref, w2_ref, a0_ref, a2_ref,
     kk_ref, ka_ref) = refs[:18]
    pos = 18
    if has_vres:
        vf_ref, v0_ref, v1_ref, v2_ref = refs[pos:pos + 4]
        pos += 4
    ro_ref, lwo_ref, ko_ref, vo_ref, ao_ref, bo_ref = refs[pos:pos + 6]

    def mix(x_ref, p_ref, mu_ref):
        x = x_ref[...]
        return x + (p_ref[...] - x) * mu_ref[...]

    lo = mix(lo_ref, plo_ref, mulo_ref)
    v = mix(v_ref, pv_ref, muv_ref)
    vres = None
    if has_vres:
        vres = (vf_ref[...], v0_ref[...], _dot(v.astype(BF16), v1_ref[...]).astype(BF16), v2_ref)
    lw, k, v, a, b = _rwkv_tokens(
        mix(k_ref, pk_ref, muk_ref), v, jnp.tanh(lo).astype(BF16), lo.astype(BF16),
        w0_ref[...], w2_ref, a0_ref[...], a2_ref, kk_ref[...], ka_ref[...], vres)
    ro_ref[...] = mix(r_ref, pr_ref, mur_ref)
    lwo_ref[...] = lw
    ko_ref[...] = k
    vo_ref[...] = v
    ao_ref[...] = a
    bo_ref[...] = b


def _token_param_specs(rw, index):
    vec = pl.BlockSpec((1, rw), index)
    return ([pl.BlockSpec((1, rw), lambda *g, c=c: (0, c)) for c in range(3)]
            + [pl.BlockSpec((1, LO_W), index), vec, pl.BlockSpec((LO_W, rw), index),
               vec, pl.BlockSpec((LO_W, rw), index), vec, vec])


def _vres_specs(rows, rw, row_index, index):
    return [pl.BlockSpec((rows, rw), row_index), pl.BlockSpec((1, rw), index),
            pl.BlockSpec((rw, LORA_PAD), index), pl.BlockSpec((LORA_PAD, rw), index)]


def _rwkv_prep(p, prev_rkv, prev_lo, tok, vres, *, tm, rw, lo_col):
    m = p.shape[0]
    row = lambda i: (i, 0)
    col = lambda c: (lambda i: (i, c))
    const = lambda i: (0, 0)
    in_specs = [pl.BlockSpec((tm, rw), col(0)), pl.BlockSpec((tm, rw), col(1)),
                pl.BlockSpec((tm, rw), col(2)), pl.BlockSpec((tm, LO_W), col(lo_col // LO_W)),
                pl.BlockSpec((tm, rw), col(0)), pl.BlockSpec((tm, rw), col(1)),
                pl.BlockSpec((tm, rw), col(2)), pl.BlockSpec((tm, LO_W), row)]
    args = [p, p, p, p, prev_rkv, prev_rkv, prev_rkv, prev_lo]
    in_specs += _token_param_specs(rw, const)
    args += list(tok)
    if vres is not None:
        in_specs += _vres_specs(tm, rw, row, const)
        args += list(vres)
    out = jax.ShapeDtypeStruct((m, rw), F32)
    return pl.pallas_call(
        functools.partial(_rwkv_prep_kernel, has_vres=vres is not None),
        grid=(m // tm,),
        in_specs=in_specs,
        out_specs=[pl.BlockSpec((tm, rw), row)] * 6,
        out_shape=[out] * 6,
        compiler_params=_cparams(("parallel",)),
        name="rwkv_prep",
    )(*args)


def _shift_seq(x, mu, init_row, carry_ref, sl, first):
    prev0 = jnp.where(first, init_row, carry_ref[:, sl])
    sh = pltpu.roll(x, 1, 0)
    rows = lax.broadcasted_iota(jnp.int32, x.shape, 0)
    sh = jnp.where(rows == 0, prev0, sh)
    carry_ref[:, sl] = x[x.shape[0] - 1:, :]
    return x + (sh - x) * mu


_CHAIN_OPERANDS = ("at", "rt", "btl", "bth", "ktl", "kth", "vl", "vh", "be", "ke")


def _chain_operands(r, lw, lp, k, v, a, b):
    rows, width = r.shape
    c_len = WKV_CHUNK
    n_sub = rows // c_len
    ends =[lp[(ci + 1) * c_len - 1:(ci + 1) * c_len] for ci in range(n_sub)]
    lp_end = jnp.concatenate([jnp.broadcast_to(e, (c_len, width)) for e in ends], axis=0)
    pinv = jnp.exp(-lp)
    pend = jnp.exp(lp_end - lp)
    first_head = lax.broadcasted_iota(jnp.int32, (rows, width), 1) < HEAD
    halves = lambda z: (jnp.where(first_head, z, 0.0), jnp.where(first_head, 0.0, z))
    ops = ((a * jnp.exp(lp - lw), r * jnp.exp(lp)) + halves(b * pinv) + halves(k * pinv)
           + halves(v) + (b * pend, k * pend))
    return [o.astype(BF16) for o in ops], jnp.exp(jnp.concatenate(ends, axis=0))


def _wkv_chain(ops, plast_ref, rd, s_ref, fill):
    _, rows, width = ops["at"].shape
    c_len = WKV_CHUNK
    n_sub = rows // c_len
    n_pairs = width // PAIR

    lane = lax.broadcasted_iota(jnp.int32, (c_len, PAIR), 1)
    rowi = lax.broadcasted_iota(jnp.int32, (c_len, PAIR), 0)
    lo_half = lane < HEAD
    scol = lane % HEAD
    strict = scol < rowi
    incl = scol <= rowi
    r2 = lax.broadcasted_iota(jnp.int32, (PAIR, PAIR), 0)
    c2 = lax.broadcasted_iota(jnp.int32, (PAIR, PAIR), 1)
    same_head = (r2 // HEAD) == (c2 // HEAD)
    eye_pair = jnp.where(scol == rowi, 1.0, 0.0).astype(F32)

    def bd(z):
        return jnp.concatenate([jnp.where(lo_half, z, 0.0), jnp.where(lo_half, 0.0, z)], axis=0)

    chains = [(ci, pi) for ci in range(n_sub) for pi in range(n_pairs)]
    rs = {c: slice(c[0] * c_len, (c[0] + 1) * c_len) for c in chains}
    ls = {c: slice(c[1] * PAIR, (c[1] + 1) * PAIR) for c in chains}
    ld = lambda name, c: ops[name][rd, rs[c], ls[c]]
    x, g = {}, {}
    for c in chains:
        x[c] = jnp.concatenate([ld("at", c), ld("rt", c)], axis=0)
        rhs = jnp.concatenate([ld("btl", c), ld("bth", c), ld("ktl", c), ld("kth", c)], axis=0)
        g[c] = _dot_nt(x[c], rhs)
    fill()
    q_b = {c: jnp.where(incl, g[c][c_len:, :PAIR], 0.0).astype(BF16) for c in chains}
    mv = {}
    for c in chains:
        gk = g[c][:, PAIR:]
        gkm = jnp.concatenate([jnp.where(strict, gk[:c_len], 0.0),
                               jnp.where(incl, gk[c_len:], 0.0)], axis=0)
        mv[c] = _dot(gkm.astype(BF16), jnp.concatenate([ld("vl", c), ld("vh", c)], axis=0))
    fill()

    pw = {c: jnp.where(strict, g[c][:c_len, :PAIR], 0.0) for c in chains}
    tinv = {c: eye_pair + pw[c] for c in chains}
    pw = {c: _dot(pw[c].astype(BF16), bd(pw[c]).astype(BF16)) for c in chains}
    fill()
    n_sq = (c_len - 1).bit_length() - 1
    for j in range(n_sq):
        for c in chains:
            pb = pw[c].astype(BF16)
            if j == n_sq - 1:
                tinv[c] = tinv[c] + _dot(pb, bd(tinv[c]).astype(BF16))
            else:
                res = _dot(pb, jnp.concatenate([bd(tinv[c]), bd(pw[c])], axis=1).astype(BF16))
                tinv[c] = tinv[c] + res[:, :PAIR]
                pw[c] = res[:, PAIR:]
        fill()

    s_cur = [s_ref[pi] for pi in range(n_pairs)]
    y_rows = []
    for ci in range(n_sub):
        cs = [(ci, pi) for pi in range(n_pairs)]
        xs = {c: _dot_nt(x[c], s_cur[c[1]].astype(BF16)) for c in cs}
        fill()
        u = {c: _dot(tinv[c].astype(BF16), bd(xs[c][:c_len] + mv[c][:c_len]).astype(BF16))
             for c in cs}
        fill()
        y_cols = []
        for c in cs:
            y_cols.append(xs[c][c_len:] + mv[c][c_len:] + _dot(q_b[c], bd(u[c]).astype(BF16)))
            uv = jnp.concatenate([u[c].astype(BF16), ld("vl", c) + ld("vh", c)], axis=0)
            bk = jnp.concatenate([ld("be", c), ld("ke", c)], axis=0)
            p_last = plast_ref[rd, ci:ci + 1, ls[c]]
            s_cur[c[1]] = s_cur[c[1]] * p_last + jnp.where(same_head, _dot_tn(uv, bk), 0.0)
        y_rows.append(jnp.concatenate(y_cols, axis=1))
        fill()
    for pi in range(n_pairs):
        s_ref[pi] = s_cur[pi]
    return jnp.concatenate(y_rows, axis=0)


def _wkv_seq_kernel(*refs, has_vres, emit_v):
    (r_ref, k_ref, v_ref, lo_ref, z_ref, pr_ref, pk_ref, pv_ref, plo_ref,
     mur_ref, muk_ref, muv_ref, mulo_ref, w0_ref, w2_ref, a0_ref, a2_ref,
     kk_ref, ka_ref, rk_ref, lg_ref, lb_ref, s0_ref) = refs[:23]
    pos = 23
    if has_vres:
        vf_ref, v0_ref, v1_ref, v2_ref = refs[pos:pos + 4]
        pos += 4
    o_ref = refs[pos]
    pos += 1
    if emit_v:
        vo_ref = refs[pos]
        pos += 1
    s_ref, cr_ref, ck_ref, cv_ref, clo_ref = refs[pos:pos + 5]
    pos += 5
    ops = dict(zip(_CHAIN_OPERANDS, refs[pos:pos + len(_CHAIN_OPERANDS)]))
    vs_ref, bonus_ref, plast_ref = refs[pos + len(_CHAIN_OPERANDS):]

    j = pl.program_id(1)
    first = j == 0
    wr = j % 2
    rd = 1 - wr

    @pl.when(first)
    def _():
        s_ref[...] = s0_ref[...]
        for ref in list(ops.values()) + [vs_ref, bonus_ref]:
            ref[1] = jnp.zeros(ref.shape[1:], ref.dtype)
        plast_ref[1] = jnp.ones(plast_ref.shape[1:], F32)

    rows, width = vs_ref.shape[1:]
    n_sub = rows // WKV_CHUNK
    n_lane_blocks = width // LANES
    ones_bd = _head_ones()
    tri = _chunk_tri(rows)
    shared, mid = {}, {}

    def shared_vector_part():
        lo = _shift_seq(lo_ref[...], mulo_ref[...], plo_ref[...], clo_ref, slice(None), first)
        shared["tanh_lo"] = jnp.tanh(lo).astype(BF16)
        shared["lo"] = lo.astype(BF16)
        shared["v"] = _shift_seq(v_ref[...], muv_ref[...], pv_ref[...], cv_ref, slice(None), first)

    def shared_matmul_part():
        shared["w_lora"] = _dot(shared["tanh_lo"], w2_ref[...])
        shared["a_lora"] = _dot(shared["lo"], a2_ref[...])
        if has_vres:
            shared["vv_mid"] = _dot(shared["v"].astype(BF16), v1_ref[...])

    def shared_matmul_part2():
        if has_vres:
            shared["vv"] = _dot(shared["vv_mid"].astype(BF16), v2_ref[...])

    def block_vector_part(cb):
        sl = slice(cb * LANES, (cb + 1) * LANES)
        r = _shift_seq(r_ref[:, sl], mur_ref[:, sl], pr_ref[:, sl], cr_ref, sl, first)
        k = _shift_seq(k_ref[:, sl], muk_ref[:, sl], pk_ref[:, sl], ck_ref, sl, first)
        v = shared["v"][:, sl]
        if has_vres:
            v = v + (vf_ref[:, sl] - v) * jax.nn.sigmoid(v0_ref[:, sl] + shared["vv"][:, sl])
        lw, a = _decay_and_iclr(w0_ref[:, sl] + shared["w_lora"][:, sl],
                                a0_ref[:, sl] + shared["a_lora"][:, sl])
        kk = k * kk_ref[:, sl]
        k = k * (1.0 + (a - 1.0) * ka_ref[:, sl])
        mid[cb] = dict(r=r, k=k, v=v, a=a, lw=lw, kk=kk, kk2=_split_hi_lo(kk * kk),
                       rk=_split_hi_lo(r * k * rk_ref[:, sl]), lws=_split_hi_lo(lw))

    def block_matmul_part(cb):
        sl = slice(cb * LANES, (cb + 1) * LANES)
        m = mid.pop(cb)
        kk = m["kk"] / jnp.maximum(jnp.sqrt(_dot_hi_lo(m["kk2"], ones_bd, False)), 1e-12)
        bonus = _dot_hi_lo(m["rk"], ones_bd, False) * m["v"]
        lp = _dot_hi_lo(m["lws"], tri, True)
        vals, p_last = _chain_operands(m["r"], m["lw"], lp, m["k"], m["v"], -kk, kk * m["a"])
        for name, val in zip(_CHAIN_OPERANDS, vals):
            ops[name][wr, :, sl] = val
        vs_ref[wr, :, sl] = m["v"]
        bonus_ref[wr, :, sl] = bonus
        plast_ref[wr, :n_sub, sl] = p_last

    def piece(t):
        def run():
            if 0 < t <= n_lane_blocks:
                block_matmul_part(t - 1)
            if t < n_lane_blocks:
                block_vector_part(t)
        return run

    shared_vector_part()
    shared_matmul_part()
    pending = [shared_matmul_part2] + [piece(t) for t in range(n_lane_blocks + 1)]

    def fill():
        if pending:
            pending.pop(0)()

    y = _wkv_chain(ops, plast_ref, rd, s_ref, fill)
    while pending:
        fill()
    o_ref[...] = _rwkv_gate(y, bonus_ref[rd], z_ref[...], lg_ref[...], lb_ref[...]
                            ).astype(o_ref.dtype)
    if emit_v:
        vo_ref[...] = vs_ref[rd]


def _wkv_seq(p, prev_rkv, prev_lo, tok, gate, s0_bd, vres, *, n_seq, seq_len, rw, lo_col,
             z_blk, emit_v):
    m = p.shape[0]
    n_blocks = seq_len // SEQ_ROWS
    n_pairs = rw // PAIR
    prep_row = lambda n, c: n * n_blocks + jnp.minimum(c, n_blocks - 1)
    run_row = lambda n, c: n * n_blocks + jnp.maximum(c - 1, 0)
    row = lambda n, c: (prep_row(n, c), 0)
    col = lambda j: (lambda n, c: (prep_row(n, c), j))
    const = lambda n, c: (0, 0)
    blk = lambda j: pl.BlockSpec((SEQ_ROWS, rw), col(j))
    in_specs = [blk(0), blk(1), blk(2), pl.BlockSpec((SEQ_ROWS, LO_W), col(lo_col // LO_W)),
                pl.BlockSpec((SEQ_ROWS, rw), lambda n, c: (run_row(n, c), z_blk))]
    args = [p, p, p, p, p]
    for j in range(3):
        in_specs.append(pl.BlockSpec((None, 1, rw), lambda n, c, j=j: (n, 0, j)))
    in_specs.append(pl.BlockSpec((None, 1, LO_W), lambda n, c: (n, 0, 0)))
    args += [prev_rkv, prev_rkv, prev_rkv, prev_lo]
    in_specs += _token_param_specs(rw, const)
    args += list(tok)
    in_specs += [pl.BlockSpec((1, rw), const)] * 3
    args += list(gate)
    sblk = pl.BlockSpec((None, n_pairs, PAIR, PAIR), lambda n, c: (n, 0, 0, 0))
    in_specs.append(sblk)
    args.append(s0_bd)
    if vres is not None:
        in_specs += _vres_specs(SEQ_ROWS, rw, row, const)
        args += list(vres)
    run_blk = pl.BlockSpec((SEQ_ROWS, rw), lambda n, c: (run_row(n, c), 0))
    out_specs = [run_blk]
    out_shape = [jax.ShapeDtypeStruct((m, rw), BF16)]
    if emit_v:
        out_specs.append(run_blk)
        out_shape.append(jax.ShapeDtypeStruct((m, rw), F32))
    out_specs.append(sblk)
    out_shape.append(jax.ShapeDtypeStruct((n_seq, n_pairs, PAIR, PAIR), F32))
    outs = pl.pallas_call(
        functools.partial(_wkv_seq_kernel, has_vres=vres is not None, emit_v=emit_v),
        grid=(n_seq, n_blocks + 1),
        in_specs=in_specs,
        out_specs=out_specs,
        out_shape=out_shape,
        scratch_shapes=([pltpu.VMEM((1, rw), F32)] * 3 + [pltpu.VMEM((1, LO_W), F32)]
                        + [pltpu.VMEM((2, SEQ_ROWS, rw), BF16)] * len(_CHAIN_OPERANDS)
                        + [pltpu.VMEM((2, SEQ_ROWS, rw), F32)] * 2
                        + [pltpu.VMEM((2, 8, rw), F32)]),
        compiler_params=_cparams(("parallel", "arbitrary")),
        name="wkv_seq",
    )(*args)
    return (outs[0], outs[1], outs[2]) if emit_v else (outs[0], None, outs[1])


def _wkv_step_kernel(r_ref, lw_ref, k_ref, v_ref, a_ref, b_ref, s_ref, _, y_ref, so_ref,
                     vt_ref, yt_ref):
    r_t, k_t, a_t, b_t = r_ref[...].T, k_ref[...].T, a_ref[...].T, b_ref[...].T
    w_t = jnp.exp(lw_ref[...]).T
    vt_ref[...] = v_ref[...].T
    for hh in range(PAIR // HEAD):
        sl = slice(hh * HEAD, (hh + 1) * HEAD)
        r, k, a, b, w = r_t[sl], k_t[sl], a_t[sl], b_t[sl], w_t[sl]

        def body(vi, carry, hh=hh, r=r, k=k, a=a, b=b, w=w):
            s = s_ref[hh, vi]
            sa = jnp.sum(s * a, axis=0, keepdims=True)
            s_new = s * w + sa * b + vt_ref[pl.ds(hh * HEAD + vi, 1), :] * k
            so_ref[hh, vi] = s_new
            yt_ref[pl.ds(hh * HEAD + vi, 1), :] = jnp.sum(s_new * r, axis=0, keepdims=True)
            return carry

        lax.fori_loop(0, HEAD, body, 0, unroll=8)
    y_ref[...] = yt_ref[...].T


def _wkv_step(r, lw, k, v, a, b, s_all, s_out_all, layer):
    m, rw = r.shape
    hp = PAIR // HEAD
    assert m % LANES == 0
    blk = pl.BlockSpec((LANES, PAIR), lambda pi, j: (j, pi))
    sblk = pl.BlockSpec((None, hp, HEAD, HEAD, LANES), lambda pi, j: (layer, pi, 0, 0, j))
    return pl.pallas_call(
        _wkv_step_kernel,
        grid=(rw // PAIR, m // LANES),
        in_specs=[blk] * 6 + [sblk, pl.BlockSpec(memory_space=pl.ANY)],
        out_specs=[blk, sblk],
        out_shape=[jax.ShapeDtypeStruct((m, rw), F32),
                   jax.ShapeDtypeStruct(s_out_all.shape, F32)],
        scratch_shapes=[pltpu.VMEM((PAIR, LANES), F32), pltpu.VMEM((PAIR, LANES), F32)],
        input_output_aliases={7: 1},
        compiler_params=_cparams(("parallel", "parallel")),
        name="wkv_step",
    )(r, lw, k, v, a, b, s_all, s_out_all)


def _gmlp_mix(vg, g, b, ws_ref, bias_ref, seq):
    mu = jnp.mean(vg, axis=-1, keepdims=True)
    d = vg - mu
    var = jnp.mean(d * d, axis=-1, keepdims=True)
    vn = d * lax.rsqrt(var + LN_EPS) * g + b
    if not seq:
        return vn * ws_ref[...] + bias_ref[...], vn
    tm, gw = vg.shape[0], vg.shape[1] // GMLP_GROUPS
    ri = lax.broadcasted_iota(jnp.int32, (GMLP_CHUNK, GMLP_CHUNK), 0)
    ci = lax.broadcasted_iota(jnp.int32, (GMLP_CHUNK, GMLP_CHUNK), 1)
    vb = vn.astype(BF16)
    cols = []
    for gi in range(GMLP_GROUPS):
        wg = jnp.where(ci <= ri, ws_ref[gi], 0.0).astype(BF16)
        rows = [_dot(wg, vb[c * GMLP_CHUNK:(c + 1) * GMLP_CHUNK, gi * gw:(gi + 1) * gw])
                for c in range(tm // GMLP_CHUNK)]
        cols.append(jnp.concatenate(rows, axis=0))
    bias = jnp.concatenate([bias_ref[...]] * (tm // GMLP_CHUNK), axis=0)
    return jnp.concatenate(cols, axis=1) + bias, vn


def _mix_out_kernel(*refs, seq, final):
    if seq:
        or_ref = refs[0]
        pos = 1
        o_r = or_ref[...]
    else:
        y_ref, r_ref, k_ref, v_ref, zr_ref, lg_ref, lb_ref, rk_ref = refs[:8]
        pos = 8
        v = v_ref[...]
        o_r = _rwkv_gate(y_ref[...], _bonus(r_ref[...], k_ref[...], v, rk_ref[...]),
                         zr_ref[...], lg_ref[...], lb_ref[...]).astype(BF16)
    u_ref, vg_ref, zg_ref, h_ref, gg_ref, gb_ref, ws_ref, bias_ref, w_ref = refs[pos:pos + 9]
    rest = refs[pos + 9:]
    rw = o_r.shape[1]
    mixed, vn = _gmlp_mix(vg_ref[...], gg_ref[...], gb_ref[...], ws_ref, bias_ref, seq)
    o_g = u_ref[...] * mixed * _silu(zg_ref[...])
    h = h_ref[...] + _dot(o_r, w_ref[:rw, :]) + _dot(o_g.astype(BF16), w_ref[rw:, :])
    if final:
        ms = jnp.mean(h * h, axis=-1, keepdims=True)
        h = h * lax.rsqrt(ms + RMS_EPS) * rest[0][...]
    outs = rest[1:] if final else rest
    outs[0][...] = h
    if not seq:
        outs[1][...] = vn


def _mix_out(branch, p, h, gln_g, gln_b, ws, bias, w_all, layer, final_g, *, seq, tm, u_blk):
    m, d = h.shape
    gw = gln_g.shape[1]
    rw = branch[0].shape[1]
    row = pl.BlockSpec((tm, rw), lambda i: (i, 0))
    col = lambda c: pl.BlockSpec((tm, gw), lambda i, c=c: (i, c))
    hrow = pl.BlockSpec((tm, d), lambda i: (i, 0))
    const = pl.BlockSpec((1, gw), lambda i: (0, 0))
    if seq:
        in_specs = [row]
        args = list(branch)
        ws_spec = pl.BlockSpec(ws.shape, lambda i: (0, 0, 0))
        bias_spec = pl.BlockSpec(bias.shape, lambda i: (0, 0))
    else:
        y, r, k, v, lnx_g, lnx_b, r_k = branch
        in_specs = [row, row, row, row, col(u_blk - 1), const, const, const]
        args = [y, r, k, v, p, lnx_g, lnx_b, r_k]
        ws_spec, bias_spec = const, const
    in_specs += [col(u_blk), col(u_blk + 1), col(u_blk + 2), hrow, const, const,
                 ws_spec, bias_spec,
                 pl.BlockSpec((None, rw + gw, d), lambda i: (layer, 0, 0),
                              pipeline_mode=pl.Buffered(1))]
    args += [p, p, p, h, gln_g, gln_b, ws, bias, w_all]
    if final_g is not None:
        in_specs.append(pl.BlockSpec((1, d), lambda i: (0, 0)))
        args.append(final_g)
    out_specs = [hrow]
    out_shape = [jax.ShapeDtypeStruct((m, d), F32)]
    if not seq:
        out_specs.append(pl.BlockSpec((tm, gw), lambda i: (i, 0)))
        out_shape.append(jax.ShapeDtypeStruct((m, gw), F32))
    outs = pl.pallas_call(
        functools.partial(_mix_out_kernel, seq=seq, final=final_g is not None),
        grid=(m // tm,),
        in_specs=in_specs,
        out_specs=out_specs,
        out_shape=out_shape,
        compiler_params=_cparams(("parallel",)),
        name="mix_out",
    )(*args)
    return (outs[0], None) if seq else (outs[0], outs[1])


def _pad_last(x, width):
    return jnp.pad(x, [(0, 0)] * (x.ndim - 1) + [(0, width - x.shape[-1])])


def _pad_rows(x, rows):
    return jnp.pad(x, [(0, 0)] * (x.ndim - 2) + [(0, rows - x.shape[-2]), (0, 0)])


def _pick_tile(m, pref, mult):
    t = min(m, pref)
    while m % t or t % mult:
        t -= mult
    return t


def _trunk(x, s_init, shift_init, prm, *, seq):
    n_seq, seq_len, d = x.shape
    m = n_seq * seq_len
    depth = prm["depth"]
    rw, gw = prm["rw"], prm["gw"]
    lora_w = prm["lora_w"]
    n_heads = rw // HEAD
    lo_col = 3 * rw + rw + 3 * gw
    tn = prm["tn"]
    z_blk = 3

    tm_mm = _pick_tile(m, 1024, 8)
    tm_ew = _pick_tile(m, 256, GMLP_CHUNK if seq else 8)
    assert not seq or seq_len % SEQ_ROWS == 0

    h = x.reshape(m, d)
    s_out, shift_out, vn_out = [], [], []
    if not seq:
        s_init = jnp.transpose(s_init, (0, 2, 3, 4, 1))
        s_buf = jnp.zeros_like(s_init)
    v_first = None
    for l in range(depth):
        p = _in_proj(h, prm["norm_g"][l], prm["w_in"], l, tm_mm, tn, rw, gw, lora_w)

        sh = shift_init[l]
        prev_rkv = sh[:, :3 * rw]
        prev_lo = _pad_last(sh[:, 3 * rw:], LO_W)
        tok = (prm["mu_rkv"][l],) * 3 + (prm["mu_lo"][l], prm["w0"][l], prm["w2"][l],
                                         prm["a0"][l], prm["a2"][l], prm["k_k"][l], prm["k_a"][l])
        gate = (prm["r_k"][l], prm["lnx_g"][l], prm["lnx_b"][l])
        vres = None
        if l > 0:
            vres = (v_first, prm["v0"][l - 1], prm["v1"][l - 1], prm["v2"][l - 1])

        if seq:
            s0 = s_init[l].reshape(n_seq, n_heads // 2, 2, HEAD, HEAD)
            z = jnp.zeros_like(s0[:, :, 0])
            s0_bd = jnp.concatenate([jnp.concatenate([s0[:, :, 0], z], axis=-1),
                                     jnp.concatenate([z, s0[:, :, 1]], axis=-1)], axis=-2)
            o_r, v, s_bd = _wkv_seq(p, prev_rkv[:, None, :], prev_lo[:, None, :], tok, gate,
                                    s0_bd, vres, n_seq=n_seq, seq_len=seq_len, rw=rw,
                                    lo_col=lo_col, z_blk=z_blk, emit_v=l == 0)
            s_new = jnp.stack([s_bd[:, :, :HEAD, :HEAD], s_bd[:, :, HEAD:, HEAD:]], axis=2)
            s_out.append(s_new.reshape(n_seq, n_heads, HEAD, HEAD))
            branch = (o_r,)
            ws, bias = prm["w_s"][l], prm["bias_seq"][l]
        else:
            r, lw, k, v, a, b = _rwkv_prep(p, prev_rkv, prev_lo, tok, vres,
                                           tm=tm_ew, rw=rw, lo_col=lo_col)
            y, s_buf = _wkv_step(r, lw, k, v, a, b, s_init, s_buf, l)
            branch = (y, r, k, v, gate[1], gate[2], gate[0])
            ws, bias = prm["w_s0"][l], prm["bias0"][l]
        if l == 0:
            v_first = v

        last = p.reshape(n_seq, seq_len, -1)[:, -1]
        shift_out.append(jnp.concatenate(
            [last[:, :3 * rw], last[:, lo_col:lo_col + lora_w]], axis=-1))

        h, vn = _mix_out(branch, p, h, prm["gln_g"][l], prm["gln_b"][l], ws, bias,
                         prm["w_out"], l, prm["final_g"] if l == depth - 1 else None,
                         seq=seq, tm=tm_ew, u_blk=z_blk + 1)
        if not seq:
            vn_out.append(vn.reshape(n_seq, seq_len, gw))
    s_final = jnp.stack(s_out) if seq else jnp.transpose(s_buf, (0, 4, 1, 2, 3))
    return h.reshape(n_seq, seq_len, d), s_final, shift_out, vn_out


def kernel(x_prompt, x_sample, state_wkv, state_shift, norm_g, w_in, mu_shift, w0, w2, a0, a2,
           k_k, k_a, r_k, lnx_g, lnx_b, v0, v1, v2, gln_g, gln_b, w_s, b_s, w_out, final_g):
    depth, d, _ = w_in.shape
    rw = w0.shape[1]
    gw = gln_g.shape[1]
    dl, il = w2.shape[1], a2.shape[1]
    assert rw % PAIR == 0 and gw == rw and w_s.shape[2] == GMLP_CHUNK
    assert gw // GMLP_GROUPS == LANES and dl + il <= LO_W
    sc = 3 * rw + dl + il
    tn = 512
    in_cols = w_in.shape[2]

    row3 = lambda t: t[:, None, :]
    prm = dict(
        depth=depth, rw=rw, gw=gw, lora_w=dl + il, tn=tn,
        norm_g=row3(norm_g),
        w_in=jnp.swapaxes(w_in, 1, 2).astype(BF16).reshape(depth * in_cols, d),
        mu_rkv=row3(mu_shift[:, :3 * rw]),
        mu_lo=row3(_pad_last(mu_shift[:, 3 * rw:], LO_W)),
        w0=row3(w0), w2=_pad_rows(w2, LO_W).astype(BF16),
        a0=row3(a0),
        a2=jnp.pad(a2, ((0, 0), (dl, LO_W - dl - il), (0, 0))).astype(BF16),
        k_k=row3(k_k), k_a=row3(k_a), r_k=row3(r_k.reshape(depth, rw)),
        lnx_g=row3(lnx_g), lnx_b=row3(lnx_b),
        v0=row3(v0), v1=_pad_last(v1, LORA_PAD).astype(BF16), v2=_pad_rows(v2, LORA_PAD).astype(BF16),
        gln_g=row3(gln_g), gln_b=row3(gln_b),
        w_s=w_s,
        bias_seq=jnp.repeat(jnp.swapaxes(b_s, 1, 2), gw // GMLP_GROUPS, axis=2),
        w_s0=row3(jnp.repeat(w_s[:, :, 0, 0], gw // GMLP_GROUPS, axis=1)),
        bias0=row3(jnp.repeat(b_s[:, :, 0], gw // GMLP_GROUPS, axis=1)),
        w_out=w_out.astype(BF16),
        final_g=final_g[None, :],
    )

    nb = x_prompt.shape[0]
    n_heads = rw // HEAD
    s0_p = jnp.zeros((depth, nb, n_heads, HEAD, HEAD), F32)
    sh0_p = jnp.zeros((depth, nb, sc), F32)
    y_p, s_p, sh_p, _ = _trunk(x_prompt, s0_p, sh0_p, prm, seq=True)
    y_s, s_s, sh_s, vn_s = _trunk(x_sample, state_wkv, state_shift, prm, seq=False)
    return (y_p, y_s, s_p, jnp.stack(sh_p), s_s, jnp.stack(sh_s), jnp.stack(vn_s))
```

```python
import functools

import jax
import jax.numpy as jnp
from jax import lax
from jax.experimental import pallas as pl
from jax.experimental.pallas import tpu as pltpu

F32 = jnp.float32
BF16 = jnp.bfloat16

HEAD = 64
LANES = 128
PAIR = 2 * HEAD
WKV_CHUNK = HEAD
SEQ_ROWS = 2 * WKV_CHUNK
GMLP_CHUNK = 128
GMLP_GROUPS = 8
LORA_PAD = 128
LO_W = 256
RMS_EPS = 1e-6
LN_EPS = 1e-5
LNX_EPS = 64e-5
VMEM_LIMIT = 56 * 1024 * 1024

_NT = (((1,), (1,)), ((), ()))
_TN = (((0,), (0,)), ((), ()))


def _dot(a, b):
    return jnp.dot(a, b, preferred_element_type=F32)


def _dot_nt(a, b):
    return lax.dot_general(a, b, _NT, preferred_element_type=F32)


def _dot_tn(a, b):
    return lax.dot_general(a, b, _TN, preferred_element_type=F32)


def _cparams(sem):
    return pltpu.CompilerParams(dimension_semantics=sem, vmem_limit_bytes=VMEM_LIMIT)


def _silu(z):
    return z * jax.nn.sigmoid(z)


def _head_ones():
    r = lax.broadcasted_iota(jnp.int32, (LANES, LANES), 0) // HEAD
    c = lax.broadcasted_iota(jnp.int32, (LANES, LANES), 1) // HEAD
    return jnp.where(r == c, 1.0, 0.0).astype(BF16)


def _split_hi_lo(x):
    hi = x.astype(BF16)
    return hi, (x - hi.astype(F32)).astype(BF16)


def _dot_hi_lo(hi_lo, mat, mat_left):
    hi, lo = hi_lo
    return (_dot(mat, hi) + _dot(mat, lo)) if mat_left else (_dot(hi, mat) + _dot(lo, mat))


def _head_sum_hi_lo(hi_lo, ones_bd):
    hi, lo = hi_lo
    outs = [_dot_hi_lo((hi[:, j * LANES:(j + 1) * LANES], lo[:, j * LANES:(j + 1) * LANES]),
                       ones_bd, False) for j in range(hi.shape[1] // LANES)]
    return jnp.concatenate(outs, axis=1)


def _head_sum(x, ones_bd):
    return _head_sum_hi_lo(_split_hi_lo(x), ones_bd)


def _chunk_tri(rows):
    ti = lax.broadcasted_iota(jnp.int32, (rows, rows), 0)
    si = lax.broadcasted_iota(jnp.int32, (rows, rows), 1)
    same_chunk = si // WKV_CHUNK == ti // WKV_CHUNK
    return jnp.where((si <= ti) & same_chunk, 1.0, 0.0).astype(BF16)


def _decay_and_iclr(wraw, araw):
    w = -jax.nn.softplus(-wraw) - 0.5
    return -jnp.exp(w), jax.nn.sigmoid(araw)


def _in_proj_kernel(h_ref, g_ref, w_ref, o_ref, xn_ref):
    @pl.when(pl.program_id(1) == 0)
    def _():
        x = h_ref[...]
        ms = jnp.mean(x * x, axis=-1, keepdims=True)
        xn_ref[...] = (x * lax.rsqrt(ms + RMS_EPS) * g_ref[...]).astype(BF16)

    o_ref[...] = _dot_nt(xn_ref[...], w_ref[...])


_ROW_ALIGN = 16


def _in_proj(h, g, wt_rows, layer, tm, tn, rw, gw, lora_w):
    m, d = h.shape
    in_cols = 3 * rw + lora_w + rw + 3 * gw
    n_rkv = 3 * rw // tn
    n_main = n_rkv + (rw + 3 * gw) // tn
    assert (3 * rw) % tn == 0 and (rw + 3 * gw) % tn == 0 and lora_w <= tn
    assert in_cols % _ROW_ALIGN == 0 and tn % _ROW_ALIGN == 0 and lora_w % _ROW_ALIGN == 0
    base = layer * in_cols // _ROW_ALIGN
    step = tn // _ROW_ALIGN
    z0 = (3 * rw + lora_w) // _ROW_ALIGN
    lo0 = 3 * rw // _ROW_ALIGN

    def w_index(i, j):
        start = jnp.where(j < n_rkv, step * j,
                          jnp.where(j < n_main, z0 + step * (j - n_rkv), lo0))
        return ((base + start) * _ROW_ALIGN, 0)

    return pl.pallas_call(
        _in_proj_kernel,
        grid=(m // tm, n_main + 1),
        in_specs=[
            pl.BlockSpec((tm, d), lambda i, j: (i, 0)),
            _layer_spec(g, layer),
            pl.BlockSpec((pl.Element(tn), pl.Element(d)), w_index),
        ],
        out_specs=pl.BlockSpec((tm, tn), lambda i, j: (i, j)),
        out_shape=jax.ShapeDtypeStruct((m, (n_main + 1) * tn), F32),
        scratch_shapes=[pltpu.VMEM((tm, d), BF16)],
        compiler_params=_cparams(("parallel", "arbitrary")),
        name="in_proj",
    )(h, g, wt_rows)


def _rwkv_tokens(k, v, tanh_lo, lo, w0, w2, a0, a2, k_k, k_a, vres):
    lw, a = _decay_and_iclr(w0 + _dot(tanh_lo, w2[...]), a0 + _dot(lo, a2[...]))
    if vres is not None:
        v_first, v0, vv_mid, v2 = vres
        v = v + (v_first - v) * jax.nn.sigmoid(v0 + _dot(vv_mid, v2[...]))
    kk = k * k_k
    kk = kk / jnp.maximum(jnp.sqrt(_head_sum(kk * kk, _head_ones())), 1e-12)
    return lw, k * (1.0 + (a - 1.0) * k_a), v, -kk, kk * a


def _rwkv_gate(y, bonus, z, g, b):
    ones_bd = _head_ones()
    mu = _head_sum(y, ones_bd) * (1.0 / HEAD)
    d = y - mu
    var = _head_sum(d * d, ones_bd) * (1.0 / HEAD)
    return (d * lax.rsqrt(var + LNX_EPS) * g + b + bonus) * _silu(z)


def _bonus(r, k, v, rk):
    return _head_sum(r * k * rk, _head_ones()) * v


def _rwkv_prep_kernel(*refs, has_vres):
    (r_ref, k_ref, v_ref, lo_ref, pr_ref, pk_ref, pv_ref, plo_ref,
     mur_ref, muk_ref, muv_ref, mulo_ref, w0_ref, w2_ref, a0_ref, a2_ref,
     kk_ref, ka_ref) = refs[:18]
    pos = 18
    if has_vres:
        vf_ref, v0_ref, v1_ref, v2_ref = refs[pos:pos + 4]
        pos += 4
    ro_ref, lwo_ref, ko_ref, vo_ref, ao_ref, bo_ref = refs[pos:pos + 6]

    def mix(x_ref, p_ref, mu_ref):
        x = x_ref[...]
        return x + (p_ref[...] - x) * mu_ref[...]

    lo = mix(lo_ref, plo_ref, mulo_ref)
    v = mix(v_ref, pv_ref, muv_ref)
    vres = None
    if has_vres:
        vres = (vf_ref[...], v0_ref[...], _dot(v.astype(BF16), v1_ref[...]).astype(BF16), v2_ref)
    lw, k, v, a, b = _rwkv_tokens(
        mix(k_ref, pk_ref, muk_ref), v, jnp.tanh(lo).astype(BF16), lo.astype(BF16),
        w0_ref[...], w2_ref, a0_ref[...], a2_ref, kk_ref[...], ka_ref[...], vres)
    ro_ref[...] = mix(r_ref, pr_ref, mur_ref)
    lwo_ref[...] = lw
    ko_ref[...] = k
    vo_ref[...] = v
    ao_ref[...] = a
    bo_ref[...] = b


def _layer_spec(arr, layer):
    zeros = (0,) * (arr.ndim - 1)
    return pl.BlockSpec((None,) + arr.shape[1:], lambda *g: (layer,) + zeros)


def _token_param_specs(tok, rw, layer):
    return ([pl.BlockSpec((None, 1, rw), lambda *g, c=c: (layer, 0, c)) for c in range(3)]
            + [_layer_spec(t, layer) for t in tok[3:]])


def _vres_specs(vres, rows, rw, row_index, layer):
    return ([pl.BlockSpec((rows, rw), row_index)]
            + [_layer_spec(t, layer - 1) for t in vres[1:]])


def _rwkv_prep(p, prev_rkv, prev_lo, tok, vres, layer, *, tm, rw, lo_col):
    m = p.shape[0]
    row = lambda i: (i, 0)
    col = lambda c: (lambda i: (i, c))
    in_specs = [pl.BlockSpec((tm, rw), col(0)), pl.BlockSpec((tm, rw), col(1)),
                pl.BlockSpec((tm, rw), col(2)), pl.BlockSpec((tm, LO_W), col(lo_col // LO_W))]
    in_specs += [pl.BlockSpec((None, tm, rw), lambda i, c=c: (layer, i, c)) for c in range(3)]
    in_specs.append(pl.BlockSpec((None, tm, LO_W), lambda i: (layer, i, 0)))
    args = [p, p, p, p, prev_rkv, prev_rkv, prev_rkv, prev_lo]
    in_specs += _token_param_specs(tok, rw, layer)
    args += list(tok)
    if vres is not None:
        in_specs += _vres_specs(vres, tm, rw, row, layer)
        args += list(vres)
    out = jax.ShapeDtypeStruct((m, rw), F32)
    return pl.pallas_call(
        functools.partial(_rwkv_prep_kernel, has_vres=vres is not None),
        grid=(m // tm,),
        in_specs=in_specs,
        out_specs=[pl.BlockSpec((tm, rw), row)] * 6,
        out_shape=[out] * 6,
        compiler_params=_cparams(("parallel",)),
        name="rwkv_prep",
    )(*args)


def _shift_seq(x, mu, init_row, carry_ref, sl, first):
    prev0 = jnp.where(first, init_row, carry_ref[:, sl])
    sh = pltpu.roll(x, 1, 0)
    rows = lax.broadcasted_iota(jnp.int32, x.shape, 0)
    sh = jnp.where(rows == 0, prev0, sh)
    carry_ref[:, sl] = x[x.shape[0] - 1:, :]
    return x + (sh - x) * mu


_CHAIN_OPERANDS = ("at", "rt", "btl", "bth", "ktl", "kth", "vl", "vh", "be", "ke")


def _chain_operands(r, lw, lp, k, v, a, b):
    rows, width = r.shape
    c_len = WKV_CHUNK
    n_sub = rows // c_len
    ends =[lp[(ci + 1) * c_len - 1:(ci + 1) * c_len] for ci in range(n_sub)]
    lp_end = jnp.concatenate([jnp.broadcast_to(e, (c_len, width)) for e in ends], axis=0)
    pinv = jnp.exp(-lp)
    pend = jnp.exp(lp_end - lp)
    first_head = lax.broadcasted_iota(jnp.int32, (rows, width), 1) < HEAD
    halves = lambda z: (jnp.where(first_head, z, 0.0), jnp.where(first_head, 0.0, z))
    ops = ((a * jnp.exp(lp - lw), r * jnp.exp(lp)) + halves(b * pinv) + halves(k * pinv)
           + halves(v) + (b * pend, k * pend))
    return [o.astype(BF16) for o in ops], jnp.exp(jnp.concatenate(ends, axis=0))


def _wkv_chain(ops, plast_ref, rd, s_ref, fill):
    _, rows, width = ops["at"].shape
    c_len = WKV_CHUNK
    n_sub = rows // c_len
    n_pairs = width // PAIR

    lane = lax.broadcasted_iota(jnp.int32, (c_len, PAIR), 1)
    rowi = lax.broadcasted_iota(jnp.int32, (c_len, PAIR), 0)
    lo_half = lane < HEAD
    scol = lane % HEAD
    strict = scol < rowi
    incl = scol <= rowi
    r2 = lax.broadcasted_iota(jnp.int32, (PAIR, PAIR), 0)
    c2 = lax.broadcasted_iota(jnp.int32, (PAIR, PAIR), 1)
    same_head = (r2 // HEAD) == (c2 // HEAD)
    eye_pair = jnp.where(scol == rowi, 1.0, 0.0).astype(F32)

    def bd(z):
        return jnp.concatenate([jnp.where(lo_half, z, 0.0), jnp.where(lo_half, 0.0, z)], axis=0)

    chains = [(ci, pi) for ci in range(n_sub) for pi in range(n_pairs)]
    rs = {c: slice(c[0] * c_len, (c[0] + 1) * c_len) for c in chains}
    ls = {c: slice(c[1] * PAIR, (c[1] + 1) * PAIR) for c in chains}
    ld = lambda name, c: ops[name][rd, rs[c], ls[c]]
    x, g = {}, {}
    for c in chains:
        x[c] = jnp.concatenate([ld("at", c), ld("rt", c)], axis=0)
        rhs = jnp.concatenate([ld("btl", c), ld("bth", c), ld("ktl", c), ld("kth", c)], axis=0)
        g[c] = _dot_nt(x[c], rhs)
    fill()
    q_b = {c: jnp.where(incl, g[c][c_len:, :PAIR], 0.0).astype(BF16) for c in chains}
    mv = {}
    for c in chains:
        gk = g[c][:, PAIR:]
        gkm = jnp.concatenate([jnp.where(strict, gk[:c_len], 0.0),
                               jnp.where(incl, gk[c_len:], 0.0)], axis=0)
        mv[c] = _dot(gkm.astype(BF16), jnp.concatenate([ld("vl", c), ld("vh", c)], axis=0))
    fill()

    pw = {c: jnp.where(strict, g[c][:c_len, :PAIR], 0.0) for c in chains}
    tinv = {c: eye_pair + pw[c] for c in chains}
    pw = {c: _dot(pw[c].astype(BF16), bd(pw[c]).astype(BF16)) for c in chains}
    fill()
    n_sq = (c_len - 1).bit_length() - 1
    for j in range(n_sq):
        for c in chains:
            pb = pw[c].astype(BF16)
            if j == n_sq - 1:
                tinv[c] = tinv[c] + _dot(pb, bd(tinv[c]).astype(BF16))
            else:
                res = _dot(pb, jnp.concatenate([bd(tinv[c]), bd(pw[c])], axis=1).astype(BF16))
                tinv[c] = tinv[c] + res[:, :PAIR]
                pw[c] = res[:, PAIR:]
        fill()

    s_cur = [s_ref[pi] for pi in range(n_pairs)]
    y_rows = []
    for ci in range(n_sub):
        cs = [(ci, pi) for pi in range(n_pairs)]
        xs = {c: _dot_nt(x[c], s_cur[c[1]].astype(BF16)) for c in cs}
        fill()
        u = {c: _dot(tinv[c].astype(BF16), bd(xs[c][:c_len] + mv[c][:c_len]).astype(BF16))
             for c in cs}
        fill()
        y_cols = []
        for c in cs:
            y_cols.append(xs[c][c_len:] + mv[c][c_len:] + _dot(q_b[c], bd(u[c]).astype(BF16)))
            uv = jnp.concatenate([u[c].astype(BF16), ld("vl", c) + ld("vh", c)], axis=0)
            bk = jnp.concatenate([ld("be", c), ld("ke", c)], axis=0)
            p_last = plast_ref[rd, ci:ci + 1, ls[c]]
            s_cur[c[1]] = s_cur[c[1]] * p_last + jnp.where(same_head, _dot_tn(uv, bk), 0.0)
        y_rows.append(jnp.concatenate(y_cols, axis=1))
        fill()
    for pi in range(n_pairs):
        s_ref[pi] = s_cur[pi]
    return jnp.concatenate(y_rows, axis=0)


def _wkv_seq_kernel(*refs, has_vres, emit_v):
    (r_ref, k_ref, v_ref, lo_ref, z_ref, pr_ref, pk_ref, pv_ref, plo_ref,
     mur_ref, muk_ref, muv_ref, mulo_ref, w0_ref, w2_ref, a0_ref, a2_ref,
     kk_ref, ka_ref, rk_ref, lg_ref, lb_ref, s0_ref) = refs[:23]
    pos = 23
    if has_vres:
        vf_ref, v0_ref, v1_ref, v2_ref = refs[pos:pos + 4]
        pos += 4
    o_ref = refs[pos]
    pos += 1
    if emit_v:
        vo_ref = refs[pos]
        pos += 1
    s_ref, cr_ref, ck_ref, cv_ref, clo_ref = refs[pos:pos + 5]
    pos += 5
    ops = dict(zip(_CHAIN_OPERANDS, refs[pos:pos + len(_CHAIN_OPERANDS)]))
    vs_ref, bonus_ref, plast_ref = refs[pos + len(_CHAIN_OPERANDS):]

    j = pl.program_id(1)
    first = j == 0
    wr = j % 2
    rd = 1 - wr

    @pl.when(first)
    def _():
        s_ref[...] = s0_ref[...]
        for ref in list(ops.values()) + [vs_ref, bonus_ref]:
            ref[1] = jnp.zeros(ref.shape[1:], ref.dtype)
        plast_ref[1] = jnp.ones(plast_ref.shape[1:], F32)

    rows, width = vs_ref.shape[1:]
    n_sub = rows // WKV_CHUNK
    n_lane_blocks = width // LANES
    ones_bd = _head_ones()
    tri = _chunk_tri(rows)
    shared, mid = {}, {}

    def shared_vector_part():
        lo = _shift_seq(lo_ref[...], mulo_ref[...], plo_ref[...], clo_ref, slice(None), first)
        shared["tanh_lo"] = jnp.tanh(lo).astype(BF16)
        shared["lo"] = lo.astype(BF16)
        shared["v"] = _shift_seq(v_ref[...], muv_ref[...], pv_ref[...], cv_ref, slice(None), first)

    def shared_matmul_part():
        shared["w_lora"] = _dot(shared["tanh_lo"], w2_ref[...])
        shared["a_lora"] = _dot(shared["lo"], a2_ref[...])
        if has_vres:
            shared["vv_mid"] = _dot(shared["v"].astype(BF16), v1_ref[...])

    def shared_matmul_part2():
        if has_vres:
            shared["vv"] = _dot(shared["vv_mid"].astype(BF16), v2_ref[...])

    def block_vector_part(cb):
        sl = slice(cb * LANES, (cb + 1) * LANES)
        r = _shift_seq(r_ref[:, sl], mur_ref[:, sl], pr_ref[:, sl], cr_ref, sl, first)
        k = _shift_seq(k_ref[:, sl], muk_ref[:, sl], pk_ref[:, sl], ck_ref, sl, first)
        v = shared["v"][:, sl]
        if has_vres:
            v = v + (vf_ref[:, sl] - v) * jax.nn.sigmoid(v0_ref[:, sl] + shared["vv"][:, sl])
        lw, a = _decay_and_iclr(w0_ref[:, sl] + shared["w_lora"][:, sl],
                                a0_ref[:, sl] + shared["a_lora"][:, sl])
        kk = k * kk_ref[:, sl]
        k = k * (1.0 + (a - 1.0) * ka_ref[:, sl])
        mid[cb] = dict(r=r, k=k, v=v, a=a, lw=lw, kk=kk, kk2=_split_hi_lo(kk * kk),
                       rk=_split_hi_lo(r * k * rk_ref[:, sl]), lws=_split_hi_lo(lw))

    def block_matmul_part(cb):
        sl = slice(cb * LANES, (cb + 1) * LANES)
        m = mid.pop(cb)
        kk = m["kk"] / jnp.maximum(jnp.sqrt(_dot_hi_lo(m["kk2"], ones_bd, False)), 1e-12)
        bonus = _dot_hi_lo(m["rk"], ones_bd, False) * m["v"]
        lp = _dot_hi_lo(m["lws"], tri, True)
        vals, p_last = _chain_operands(m["r"], m["lw"], lp, m["k"], m["v"], -kk, kk * m["a"])
        for name, val in zip(_CHAIN_OPERANDS, vals):
            ops[name][wr, :, sl] = val
        vs_ref[wr, :, sl] = m["v"]
        bonus_ref[wr, :, sl] = bonus
        plast_ref[wr, :n_sub, sl] = p_last

    def piece(t):
        def run():
            if 0 < t <= n_lane_blocks:
                block_matmul_part(t - 1)
            if t < n_lane_blocks:
                block_vector_part(t)
        return run

    shared_vector_part()
    shared_matmul_part()
    pending = [shared_matmul_part2] + [piece(t) for t in range(n_lane_blocks + 1)]

    def fill():
        if pending:
            pending.pop(0)()

    y = _wkv_chain(ops, plast_ref, rd, s_ref, fill)
    while pending:
        fill()
    o_ref[...] = _rwkv_gate(y, bonus_ref[rd], z_ref[...], lg_ref[...], lb_ref[...]
                            ).astype(o_ref.dtype)
    if emit_v:
        vo_ref[...] = vs_ref[rd]


def _wkv_seq(p, prev_rkv, prev_lo, tok, gate, s0_bd, vres, layer, *, n_seq, seq_len, rw,
             lo_col, z_blk, emit_v):
    m = p.shape[0]
    n_blocks = seq_len // SEQ_ROWS
    n_pairs = rw // PAIR
    prep_row = lambda n, c: n * n_blocks + jnp.minimum(c, n_blocks - 1)
    run_row = lambda n, c: n * n_blocks + jnp.maximum(c - 1, 0)
    row = lambda n, c: (prep_row(n, c), 0)
    col = lambda j: (lambda n, c: (prep_row(n, c), j))
    blk = lambda j: pl.BlockSpec((SEQ_ROWS, rw), col(j))
    in_specs = [blk(0), blk(1), blk(2), pl.BlockSpec((SEQ_ROWS, LO_W), col(lo_col // LO_W)),
                pl.BlockSpec((SEQ_ROWS, rw), lambda n, c: (run_row(n, c), z_blk))]
    args = [p, p, p, p, p]
    for j in range(3):
        in_specs.append(pl.BlockSpec((None, None, 1, rw), lambda n, c, j=j: (layer, n, 0, j)))
    in_specs.append(pl.BlockSpec((None, None, 1, LO_W), lambda n, c: (layer, n, 0, 0)))
    args += [prev_rkv, prev_rkv, prev_rkv, prev_lo]
    in_specs += _token_param_specs(tok, rw, layer)
    args += list(tok)
    in_specs += [_layer_spec(t, layer) for t in gate]
    args += list(gate)
    sblk = pl.BlockSpec((None, n_pairs, PAIR, PAIR), lambda n, c: (n, 0, 0, 0))
    in_specs.append(pl.BlockSpec((None, None, n_pairs, PAIR, PAIR),
                                 lambda n, c: (layer, n, 0, 0, 0)))
    args.append(s0_bd)
    if vres is not None:
        in_specs += _vres_specs(vres, SEQ_ROWS, rw, row, layer)
        args += list(vres)
    run_blk = pl.BlockSpec((SEQ_ROWS, rw), lambda n, c: (run_row(n, c), 0))
    out_specs = [run_blk]
    out_shape = [jax.ShapeDtypeStruct((m, rw), BF16)]
    if emit_v:
        out_specs.append(run_blk)
        out_shape.append(jax.ShapeDtypeStruct((m, rw), F32))
    out_specs.append(sblk)
    out_shape.append(jax.ShapeDtypeStruct((n_seq, n_pairs, PAIR, PAIR), F32))
    outs = pl.pallas_call(
        functools.partial(_wkv_seq_kernel, has_vres=vres is not None, emit_v=emit_v),
        grid=(n_seq, n_blocks + 1),
        in_specs=in_specs,
        out_specs=out_specs,
        out_shape=out_shape,
        scratch_shapes=([pltpu.VMEM((1, rw), F32)] * 3 + [pltpu.VMEM((1, LO_W), F32)]
                        + [pltpu.VMEM((2, SEQ_ROWS, rw), BF16)] * len(_CHAIN_OPERANDS)
                        + [pltpu.VMEM((2, SEQ_ROWS, rw), F32)] * 2
                        + [pltpu.VMEM((2, 8, rw), F32)]),
        compiler_params=_cparams(("parallel", "arbitrary")),
        name="wkv_seq",
    )(*args)
    return (outs[0], outs[1], outs[2]) if emit_v else (outs[0], None, outs[1])


def _wkv_step_kernel(r_ref, lw_ref, k_ref, v_ref, a_ref, b_ref, s_ref, _, y_ref, so_ref,
                     vt_ref, yt_ref):
    r_t, k_t, a_t, b_t = r_ref[...].T, k_ref[...].T, a_ref[...].T, b_ref[...].T
    w_t = jnp.exp(lw_ref[...]).T
    vt_ref[...] = v_ref[...].T
    for hh in range(PAIR // HEAD):
        sl = slice(hh * HEAD, (hh + 1) * HEAD)
        r, k, a, b, w = r_t[sl], k_t[sl], a_t[sl], b_t[sl], w_t[sl]

        def body(vi, carry, hh=hh, r=r, k=k, a=a, b=b, w=w):
            s = s_ref[hh, vi]
            sa = jnp.sum(s * a, axis=0, keepdims=True)
            s_new = s * w + sa * b + vt_ref[pl.ds(hh * HEAD + vi, 1), :] * k
            so_ref[hh, vi] = s_new
            yt_ref[pl.ds(hh * HEAD + vi, 1), :] = jnp.sum(s_new * r, axis=0, keepdims=True)
            return carry

        lax.fori_loop(0, HEAD, body, 0, unroll=8)
    y_ref[...] = yt_ref[...].T


def _wkv_step(r, lw, k, v, a, b, s_all, s_out_all, layer):
    m, rw = r.shape
    hp = PAIR // HEAD
    assert m % LANES == 0
    blk = pl.BlockSpec((LANES, PAIR), lambda pi, j: (j, pi))
    sblk = pl.BlockSpec((None, hp, HEAD, HEAD, LANES), lambda pi, j: (layer, pi, 0, 0, j))
    return pl.pallas_call(
        _wkv_step_kernel,
        grid=(rw // PAIR, m // LANES),
        in_specs=[blk] * 6 + [sblk, pl.BlockSpec(memory_space=pl.ANY)],
        out_specs=[blk, sblk],
        out_shape=[jax.ShapeDtypeStruct((m, rw), F32),
                   jax.ShapeDtypeStruct(s_out_all.shape, F32)],
        scratch_shapes=[pltpu.VMEM((PAIR, LANES), F32), pltpu.VMEM((PAIR, LANES), F32)],
        input_output_aliases={7: 1},
        compiler_params=_cparams(("parallel", "parallel")),
        name="wkv_step",
    )(r, lw, k, v, a, b, s_all, s_out_all)


def _gmlp_mix(vg, g, b, ws_ref, bias_ref, seq):
    mu = jnp.mean(vg, axis=-1, keepdims=True)
    d = vg - mu
    var = jnp.mean(d * d, axis=-1, keepdims=True)
    vn = d * lax.rsqrt(var + LN_EPS) * g + b
    if not seq:
        return vn * ws_ref[...] + bias_ref[...], vn
    tm, gw = vg.shape[0], vg.shape[1] // GMLP_GROUPS
    ri = lax.broadcasted_iota(jnp.int32, (GMLP_CHUNK, GMLP_CHUNK), 0)
    ci = lax.broadcasted_iota(jnp.int32, (GMLP_CHUNK, GMLP_CHUNK), 1)
    vb = vn.astype(BF16)
    cols = []
    for gi in range(GMLP_GROUPS):
        wg = jnp.where(ci <= ri, ws_ref[gi], 0.0).astype(BF16)
        rows = [_dot(wg, vb[c * GMLP_CHUNK:(c + 1) * GMLP_CHUNK, gi * gw:(gi + 1) * gw])
                for c in range(tm // GMLP_CHUNK)]
        cols.append(jnp.concatenate(rows, axis=0))
    bias = jnp.concatenate([bias_ref[...]] * (tm // GMLP_CHUNK), axis=0)
    return jnp.concatenate(cols, axis=1) + bias, vn


def _mix_out_kernel(*refs, seq, final):
    if seq:
        or_ref = refs[0]
        pos = 1
        o_r = or_ref[...]
    else:
        y_ref, r_ref, k_ref, v_ref, zr_ref, lg_ref, lb_ref, rk_ref = refs[:8]
        pos = 8
        v = v_ref[...]
        o_r = _rwkv_gate(y_ref[...], _bonus(r_ref[...], k_ref[...], v, rk_ref[...]),
                         zr_ref[...], lg_ref[...], lb_ref[...]).astype(BF16)
    u_ref, vg_ref, zg_ref, h_ref, gg_ref, gb_ref, ws_ref, bias_ref, w_ref = refs[pos:pos + 9]
    rest = refs[pos + 9:]
    rw = o_r.shape[1]
    mixed, vn = _gmlp_mix(vg_ref[...], gg_ref[...], gb_ref[...], ws_ref, bias_ref, seq)
    o_g = u_ref[...] * mixed * _silu(zg_ref[...])
    h = h_ref[...] + _dot(o_r, w_ref[:rw, :]) + _dot(o_g.astype(BF16), w_ref[rw:, :])
    if final:
        ms = jnp.mean(h * h, axis=-1, keepdims=True)
        h = h * lax.rsqrt(ms + RMS_EPS) * rest[0][...]
    outs = rest[1:] if final else rest
    outs[0][...] = h
    if not seq:
        outs[1][...] = vn


def _mix_out(branch, p, h, gln_g, gln_b, ws, bias, w_all, layer, final_g, *, seq, tm, u_blk):
    m, d = h.shape
    gw = gln_g.shape[-1]
    rw = branch[0].shape[1]
    row = pl.BlockSpec((tm, rw), lambda i: (i, 0))
    col = lambda c: pl.BlockSpec((tm, gw), lambda i, c=c: (i, c))
    hrow = pl.BlockSpec((tm, d), lambda i: (i, 0))
    lspec = lambda t: _layer_spec(t, layer)
    if seq:
        in_specs = [row]
        args = list(branch)
    else:
        y, r, k, v, lnx_g, lnx_b, r_k = branch
        in_specs = [row, row, row, row, col(u_blk - 1), lspec(lnx_g), lspec(lnx_b), lspec(r_k)]
        args = [y, r, k, v, p, lnx_g, lnx_b, r_k]
    in_specs += [col(u_blk), col(u_blk + 1), col(u_blk + 2), hrow, lspec(gln_g), lspec(gln_b),
                 lspec(ws), lspec(bias),
                 pl.BlockSpec((None, rw + gw, d), lambda i: (layer, 0, 0),
                              pipeline_mode=pl.Buffered(1))]
    args += [p, p, p, h, gln_g, gln_b, ws, bias, w_all]
    if final_g is not None:
        in_specs.append(pl.BlockSpec((1, d), lambda i: (0, 0)))
        args.append(final_g)
    out_specs = [hrow]
    out_shape = [jax.ShapeDtypeStruct((m, d), F32)]
    if not seq:
        out_specs.append(pl.BlockSpec((tm, gw), lambda i: (i, 0)))
        out_shape.append(jax.ShapeDtypeStruct((m, gw), F32))
    outs = pl.pallas_call(
        functools.partial(_mix_out_kernel, seq=seq, final=final_g is not None),
        grid=(m // tm,),
        in_specs=in_specs,
        out_specs=out_specs,
        out_shape=out_shape,
        compiler_params=_cparams(("parallel",)),
        name="mix_out",
    )(*args)
    return (outs[0], None) if seq else (outs[0], outs[1])


def _pad_last(x, width):
    return jnp.pad(x, [(0, 0)] * (x.ndim - 1) + [(0, width - x.shape[-1])])


def _pad_rows(x, rows):
    return jnp.pad(x, [(0, 0)] * (x.ndim - 2) + [(0, rows - x.shape[-2]), (0, 0)])


def _pick_tile(m, pref, mult):
    t = min(m, pref)
    while m % t or t % mult:
        t -= mult
    return t


def _trunk(x, s_init, shift_init, prm, *, seq):
    n_seq, seq_len, d = x.shape
    m = n_seq * seq_len
    depth = prm["depth"]
    rw, gw = prm["rw"], prm["gw"]
    lora_w = prm["lora_w"]
    n_heads = rw // HEAD
    lo_col = 3 * rw + rw + 3 * gw
    tn = 512 if m > 256 else 1024
    z_blk = 3

    tm_mm = _pick_tile(m, 1024, 8)
    tm_ew = _pick_tile(m, 256, GMLP_CHUNK if seq else 8)
    assert not seq or seq_len % SEQ_ROWS == 0

    h = x.reshape(m, d)
    s_out, shift_out, vn_out = [], [], []
    prev_rkv = shift_init[:, :, :3 * rw]
    prev_lo = _pad_last(shift_init[:, :, 3 * rw:], LO_W)
    if seq:
        prev_rkv, prev_lo = prev_rkv[:, :, None, :], prev_lo[:, :, None, :]
        s0 = s_init.reshape(depth, n_seq, n_heads // 2, 2, HEAD, HEAD)
        z = jnp.zeros_like(s0[:, :, :, 0])
        s0_bd = jnp.concatenate([jnp.concatenate([s0[:, :, :, 0], z], axis=-1),
                                 jnp.concatenate([z, s0[:, :, :, 1]], axis=-1)], axis=-2)
    else:
        s_init = jnp.transpose(s_init, (0, 2, 3, 4, 1))
        s_buf = jnp.zeros_like(s_init)
    tok = (prm["mu_rkv"],) * 3 + tuple(prm[n] for n in ("mu_lo", "w0", "w2", "a0", "a2",
                                                         "k_k", "k_a"))
    gate = (prm["r_k"], prm["lnx_g"], prm["lnx_b"])
    v_first = None
    for l in range(depth):
        p = _in_proj(h, prm["norm_g"], prm["w_in"], l, tm_mm, tn, rw, gw, lora_w)
        vres = (v_first, prm["v0"], prm["v1"], prm["v2"]) if l > 0 else None

        if seq:
            o_r, v, s_bd = _wkv_seq(p, prev_rkv, prev_lo, tok, gate, s0_bd, vres, l,
                                    n_seq=n_seq, seq_len=seq_len, rw=rw, lo_col=lo_col,
                                    z_blk=z_blk, emit_v=l == 0)
            s_out.append(s_bd)
            branch = (o_r,)
            ws, bias = prm["w_s"], prm["bias_seq"]
        else:
            r, lw, k, v, a, b = _rwkv_prep(p, prev_rkv, prev_lo, tok, vres, l,
                                           tm=tm_ew, rw=rw, lo_col=lo_col)
            y, s_buf = _wkv_step(r, lw, k, v, a, b, s_init, s_buf, l)
            branch = (y, r, k, v, gate[1], gate[2], gate[0])
            ws, bias = prm["w_s0"], prm["bias0"]
        if l == 0:
            v_first = v

        last = p.reshape(n_seq, seq_len, -1)[:, -1]
        shift_out.append(jnp.concatenate(
            [last[:, :3 * rw], last[:, lo_col:lo_col + lora_w]], axis=-1))

        h, vn = _mix_out(branch, p, h, prm["gln_g"], prm["gln_b"], ws, bias,
                         prm["w_out"], l, prm["final_g"] if l == depth - 1 else None,
                         seq=seq, tm=tm_ew, u_blk=z_blk + 1)
        if not seq:
            vn_out.append(vn.reshape(n_seq, seq_len, gw))
    if seq:
        s_bd = jnp.stack(s_out)
        s_final = jnp.stack([s_bd[..., :HEAD, :HEAD], s_bd[..., HEAD:, HEAD:]], axis=3)
        s_final = s_final.reshape(depth, n_seq, n_heads, HEAD, HEAD)
    else:
        s_final = jnp.transpose(s_buf, (0, 4, 1, 2, 3))
    return h.reshape(n_seq, seq_len, d), s_final, shift_out, vn_out


def kernel(x_prompt, x_sample, state_wkv, state_shift, norm_g, w_in, mu_shift, w0, w2, a0, a2,
           k_k, k_a, r_k, lnx_g, lnx_b, v0, v1, v2, gln_g, gln_b, w_s, b_s, w_out, final_g):
    depth, d, _ = w_in.shape
    rw = w0.shape[1]
    gw = gln_g.shape[1]
    dl, il = w2.shape[1], a2.shape[1]
    assert rw % PAIR == 0 and gw == rw and w_s.shape[2] == GMLP_CHUNK
    assert gw // GMLP_GROUPS == LANES and dl + il <= LO_W
    sc = 3 * rw + dl + il
    in_cols = w_in.shape[2]

    row3 = lambda t: t[:, None, :]
    prm = dict(
        depth=depth, rw=rw, gw=gw, lora_w=dl + il,
        norm_g=row3(norm_g),
        w_in=jnp.swapaxes(w_in, 1, 2).astype(BF16).reshape(depth * in_cols, d),
        mu_rkv=row3(mu_shift[:, :3 * rw]),
        mu_lo=row3(_pad_last(mu_shift[:, 3 * rw:], LO_W)),
        w0=row3(w0), w2=_pad_rows(w2, LO_W).astype(BF16),
        a0=row3(a0),
        a2=jnp.pad(a2, ((0, 0), (dl, LO_W - dl - il), (0, 0))).astype(BF16),
        k_k=row3(k_k), k_a=row3(k_a), r_k=row3(r_k.reshape(depth, rw)),
        lnx_g=row3(lnx_g), lnx_b=row3(lnx_b),
        v0=row3(v0), v1=_pad_last(v1, LORA_PAD).astype(BF16), v2=_pad_rows(v2, LORA_PAD).astype(BF16),
        gln_g=row3(gln_g), gln_b=row3(gln_b),
        w_s=w_s,
        bias_seq=jnp.repeat(jnp.swapaxes(b_s, 1, 2), gw // GMLP_GROUPS, axis=2),
        w_s0=row3(jnp.repeat(w_s[:, :, 0, 0], gw // GMLP_GROUPS, axis=1)),
        bias0=row3(jnp.repeat(b_s[:, :, 0], gw // GMLP_GROUPS, axis=1)),
        w_out=w_out.astype(BF16),
        final_g=final_g[None, :],
    )

    nb = x_prompt.shape[0]
    n_heads = rw // HEAD
    s0_p = jnp.zeros((depth, nb, n_heads, HEAD, HEAD), F32)
    sh0_p = jnp.zeros((depth, nb, sc), F32)
    y_p, s_p, sh_p, _ = _trunk(x_prompt, s0_p, sh0_p, prm, seq=True)
    y_s, s_s, sh_s, vn_s = _trunk(x_sample, state_wkv, state_shift, prm, seq=False)
    return (y_p, y_s, s_p, jnp.stack(sh_p), s_s, jnp.stack(sh_s), jnp.stack(vn_s))
```

```python
import functools

import jax
import jax.numpy as jnp
from jax import lax
from jax.experimental import pallas as pl
from jax.experimental.pallas import tpu as pltpu

F32 = jnp.float32
BF16 = jnp.bfloat16

HEAD = 64
LANES = 128
PAIR = 2 * HEAD
WKV_CHUNK = HEAD
SEQ_ROWS = 2 * WKV_CHUNK
GMLP_CHUNK = 128
GMLP_GROUPS = 8
LORA_PAD = 128
LO_W = 256
RMS_EPS = 1e-6
LN_EPS = 1e-5
LNX_EPS = 64e-5
VMEM_LIMIT = 56 * 1024 * 1024

_NT = (((1,), (1,)), ((), ()))
_TN = (((0,), (0,)), ((), ()))


def _dot(a, b):
    return jnp.dot(a, b, preferred_element_type=F32)


def _dot_nt(a, b):
    return lax.dot_general(a, b, _NT, preferred_element_type=F32)


def _dot_tn(a, b):
    return lax.dot_general(a, b, _TN, preferred_element_type=F32)


def _cparams(sem):
    return pltpu.CompilerParams(dimension_semantics=sem, vmem_limit_bytes=VMEM_LIMIT)


def _silu(z):
    return z * jax.nn.sigmoid(z)


def _head_ones():
    r = lax.broadcasted_iota(jnp.int32, (LANES, LANES), 0) // HEAD
    c = lax.broadcasted_iota(jnp.int32, (LANES, LANES), 1) // HEAD
    return jnp.where(r == c, 1.0, 0.0).astype(BF16)


def _split_hi_lo(x):
    hi = x.astype(BF16)
    return hi, (x - hi.astype(F32)).astype(BF16)


def _dot_hi_lo(hi_lo, mat, mat_left):
    hi, lo = hi_lo
    return (_dot(mat, hi) + _dot(mat, lo)) if mat_left else (_dot(hi, mat) + _dot(lo, mat))


def _head_sum_hi_lo(hi_lo, ones_bd):
    hi, lo = hi_lo
    outs = [_dot_hi_lo((hi[:, j * LANES:(j + 1) * LANES], lo[:, j * LANES:(j + 1) * LANES]),
                       ones_bd, False) for j in range(hi.shape[1] // LANES)]
    return jnp.concatenate(outs, axis=1)


def _head_sum(x, ones_bd):
    return _head_sum_hi_lo(_split_hi_lo(x), ones_bd)


def _chunk_tri(rows):
    ti = lax.broadcasted_iota(jnp.int32, (rows, rows), 0)
    si = lax.broadcasted_iota(jnp.int32, (rows, rows), 1)
    same_chunk = si // WKV_CHUNK == ti // WKV_CHUNK
    return jnp.where((si <= ti) & same_chunk, 1.0, 0.0).astype(BF16)


def _decay_and_iclr(wraw, araw):
    w = -jax.nn.softplus(-wraw) - 0.5
    return -jnp.exp(w), jax.nn.sigmoid(araw)


def _in_proj_kernel(h_ref, g_ref, w_ref, o_ref, xn_ref):
    @pl.when(pl.program_id(1) == 0)
    def _():
        x = h_ref[...]
        ms = jnp.mean(x * x, axis=-1, keepdims=True)
        xn_ref[...] = (x * lax.rsqrt(ms + RMS_EPS) * g_ref[...]).astype(BF16)

    o_ref[...] = _dot_nt(xn_ref[...], w_ref[...])


_ROW_ALIGN = 16


def _in_proj(h, g, wt_rows, layer, tm, tn, rw, gw, lora_w):
    m, d = h.shape
    in_cols = 3 * rw + lora_w + rw + 3 * gw
    n_rkv = 3 * rw // tn
    n_main = n_rkv + (rw + 3 * gw) // tn
    assert (3 * rw) % tn == 0 and (rw + 3 * gw) % tn == 0 and lora_w <= tn
    assert in_cols % _ROW_ALIGN == 0 and tn % _ROW_ALIGN == 0 and lora_w % _ROW_ALIGN == 0
    base = layer * in_cols // _ROW_ALIGN
    step = tn // _ROW_ALIGN
    z0 = (3 * rw + lora_w) // _ROW_ALIGN
    lo0 = 3 * rw // _ROW_ALIGN

    def w_index(i, j):
        start = jnp.where(j < n_rkv, step * j,
                          jnp.where(j < n_main, z0 + step * (j - n_rkv), lo0))
        return ((base + start) * _ROW_ALIGN, 0)

    return pl.pallas_call(
        _in_proj_kernel,
        grid=(m // tm, n_main + 1),
        in_specs=[
            pl.BlockSpec((tm, d), lambda i, j: (i, 0)),
            _layer_spec(g, layer),
            pl.BlockSpec((pl.Element(tn), pl.Element(d)), w_index),
        ],
        out_specs=pl.BlockSpec((tm, tn), lambda i, j: (i, j)),
        out_shape=jax.ShapeDtypeStruct((m, (n_main + 1) * tn), F32),
        scratch_shapes=[pltpu.VMEM((tm, d), BF16)],
        compiler_params=_cparams(("parallel", "arbitrary")),
        name="in_proj",
    )(h, g, wt_rows)


def _rwkv_tokens(k, v, tanh_lo, lo, w0, w2, a0, a2, k_k, k_a, vres):
    lw, a = _decay_and_iclr(w0 + _dot(tanh_lo, w2[...]), a0 + _dot(lo, a2[...]))
    if vres is not None:
        v_first, v0, vv_mid, v2 = vres
        v = v + (v_first - v) * jax.nn.sigmoid(v0 + _dot(vv_mid, v2[...]))
    kk = k * k_k
    kk = kk / jnp.maximum(jnp.sqrt(_head_sum(kk * kk, _head_ones())), 1e-12)
    return lw, k * (1.0 + (a - 1.0) * k_a), v, -kk, kk * a


def _rwkv_gate(y, bonus, z, g, b):
    ones_bd = _head_ones()
    mu = _head_sum(y, ones_bd) * (1.0 / HEAD)
    d = y - mu
    var = _head_sum(d * d, ones_bd) * (1.0 / HEAD)
    return (d * lax.rsqrt(var + LNX_EPS) * g + b + bonus) * _silu(z)


def _bonus(r, k, v, rk):
    return _head_sum(r * k * rk, _head_ones()) * v


def _rwkv_prep_kernel(*refs, has_vres):
    (r_ref, k_ref, v_ref, lo_ref, pr_ref, pk_ref, pv_ref, plo_ref,
     mur_ref, muk_ref, muv_ref, mulo_ref, w0_ref, w2_ref, a0_ref, a2_ref,
     kk_ref, ka_ref) = refs[:18]
    pos = 18
    if has_vres:
        vf_ref, v0_ref, v1_ref, v2_ref = refs[pos:pos + 4]
        pos += 4
    ro_ref, lwo_ref, ko_ref, vo_ref, ao_ref, bo_ref = refs[pos:pos + 6]

    def mix(x_ref, p_ref, mu_ref):
        x = x_ref[...]
        return x + (p_ref[...] - x) * mu_ref[...]

    lo = mix(lo_ref, plo_ref, mulo_ref)
    v = mix(v_ref, pv_ref, muv_ref)
    vres = None
    if has_vres:
        vres = (vf_ref[...], v0_ref[...], _dot(v.astype(BF16), v1_ref[...]).astype(BF16), v2_ref)
    lw, k, v, a, b = _rwkv_tokens(
        mix(k_ref, pk_ref, muk_ref), v, jnp.tanh(lo).astype(BF16), lo.astype(BF16),
        w0_ref[...], w2_ref, a0_ref[...], a2_ref, kk_ref[...], ka_ref[...], vres)
    ro_ref[...] = mix(r_ref, pr_ref, mur_ref)
    lwo_ref[...] = lw
    ko_ref[...] = k
    vo_ref[...] = v
    ao_ref[...] = a
    bo_ref[...] = b


def _layer_spec(arr, layer):
    zeros = (0,) * (arr.ndim - 1)
    return pl.BlockSpec((None,) + arr.shape[1:], lambda *g: (layer,) + zeros)


def _token_param_specs(tok, rw, layer):
    return ([pl.BlockSpec((None, 1, rw), lambda *g, c=c: (layer, 0, c)) for c in range(3)]
            + [_layer_spec(t, layer) for t in tok[3:]])


def _vres_specs(vres, rows, rw, row_index, layer):
    return ([pl.BlockSpec((rows, rw), row_index)]
            + [_layer_spec(t, layer - 1) for t in vres[1:]])


def _rwkv_prep(p, prev_rkv, prev_lo, tok, vres, layer, *, tm, rw, lo_col):
    m = p.shape[0]
    row = lambda i: (i, 0)
    col = lambda c: (lambda i: (i, c))
    in_specs = [pl.BlockSpec((tm, rw), col(0)), pl.BlockSpec((tm, rw), col(1)),
                pl.BlockSpec((tm, rw), col(2)), pl.BlockSpec((tm, LO_W), col(lo_col // LO_W))]
    in_specs += [pl.BlockSpec((None, tm, rw), lambda i, c=c: (layer, i, c)) for c in range(3)]
    in_specs.append(pl.BlockSpec((None, tm, LO_W), lambda i: (layer, i, 0)))
    args = [p, p, p, p, prev_rkv, prev_rkv, prev_rkv, prev_lo]
    in_specs += _token_param_specs(tok, rw, layer)
    args += list(tok)
    if vres is not None:
        in_specs += _vres_specs(vres, tm, rw, row, layer)
        args += list(vres)
    out = jax.ShapeDtypeStruct((m, rw), F32)
    return pl.pallas_call(
        functools.partial(_rwkv_prep_kernel, has_vres=vres is not None),
        grid=(m // tm,),
        in_specs=in_specs,
        out_specs=[pl.BlockSpec((tm, rw), row)] * 6,
        out_shape=[out] * 6,
        compiler_params=_cparams(("parallel",)),
        name="rwkv_prep",
    )(*args)


def _shift_seq(x, mu, init_row, carry_ref, sl, first):
    prev0 = jnp.where(first, init_row, carry_ref[:, sl])
    sh = pltpu.roll(x, 1, 0)
    rows = lax.broadcasted_iota(jnp.int32, x.shape, 0)
    sh = jnp.where(rows == 0, prev0, sh)
    carry_ref[:, sl] = x[x.shape[0] - 1:, :]
    return x + (sh - x) * mu


_EMPTY_FILL_SLOTS = (2, 4, 6, 8)
_CHAIN_OPERANDS = ("at", "rt", "btl", "bth", "ktl", "kth", "vl", "vh", "be", "ke")


def _chain_operands(r, lw, lp, k, v, a, b):
    rows, width = r.shape
    c_len = WKV_CHUNK
    n_sub = rows // c_len
    p_ends = [jnp.exp(lp[(ci + 1) * c_len - 1:(ci + 1) * c_len]) for ci in range(n_sub)]
    pinv = jnp.exp(-lp)
    pend = pinv * jnp.concatenate(
        [jnp.broadcast_to(e, (c_len, width)) for e in p_ends], axis=0)
    first_head = lax.broadcasted_iota(jnp.int32, (rows, width), 1) < HEAD
    halves = lambda z: (jnp.where(first_head, z, 0.0), jnp.where(first_head, 0.0, z))
    ops = ((a * jnp.exp(lp - lw), r * jnp.exp(lp)) + halves(b * pinv) + halves(k * pinv)
           + halves(v) + (b * pend, k * pend))
    return [o.astype(BF16) for o in ops], jnp.concatenate(p_ends, axis=0)


def _wkv_chain(ops, plast_ref, rd, s_ref, fill):
    _, rows, width = ops["at"].shape
    c_len = WKV_CHUNK
    n_sub = rows // c_len
    n_pairs = width // PAIR

    lane = lax.broadcasted_iota(jnp.int32, (c_len, PAIR), 1)
    rowi = lax.broadcasted_iota(jnp.int32, (c_len, PAIR), 0)
    lo_half = lane < HEAD
    scol = lane % HEAD
    strict = scol < rowi
    incl = scol <= rowi
    r2 = lax.broadcasted_iota(jnp.int32, (PAIR, PAIR), 0)
    c2 = lax.broadcasted_iota(jnp.int32, (PAIR, PAIR), 1)
    same_head = (r2 // HEAD) == (c2 // HEAD)
    eye_pair = jnp.where(scol == rowi, 1.0, 0.0).astype(F32)

    def bd(z):
        return jnp.concatenate([jnp.where(lo_half, z, 0.0), jnp.where(lo_half, 0.0, z)], axis=0)

    chains = [(ci, pi) for ci in range(n_sub) for pi in range(n_pairs)]
    rs = {c: slice(c[0] * c_len, (c[0] + 1) * c_len) for c in chains}
    ls = {c: slice(c[1] * PAIR, (c[1] + 1) * PAIR) for c in chains}
    ld = lambda name, c: ops[name][rd, rs[c], ls[c]]
    x, g = {}, {}
    for c in chains:
        x[c] = jnp.concatenate([ld("at", c), ld("rt", c)], axis=0)
        rhs = jnp.concatenate([ld("btl", c), ld("bth", c), ld("ktl", c), ld("kth", c)], axis=0)
        g[c] = _dot_nt(x[c], rhs)
    fill()
    q_b = {c: jnp.where(incl, g[c][c_len:, :PAIR], 0.0).astype(BF16) for c in chains}
    mv = {}
    for c in chains:
        gk = g[c][:, PAIR:]
        gkm = jnp.concatenate([jnp.where(strict, gk[:c_len], 0.0),
                               jnp.where(incl, gk[c_len:], 0.0)], axis=0)
        mv[c] = _dot(gkm.astype(BF16), jnp.concatenate([ld("vl", c), ld("vh", c)], axis=0))
    fill()

    pw = {c: jnp.where(strict, g[c][:c_len, :PAIR], 0.0) for c in chains}
    tinv = {c: eye_pair + pw[c] for c in chains}
    pw = {c: _dot(pw[c].astype(BF16), bd(pw[c]).astype(BF16)) for c in chains}
    fill()
    n_sq = (c_len - 1).bit_length() - 1
    for j in range(n_sq):
        for c in chains:
            pb = pw[c].astype(BF16)
            if j == n_sq - 1:
                tinv[c] = tinv[c] + _dot(pb, bd(tinv[c]).astype(BF16))
            else:
                res = _dot(pb, jnp.concatenate([bd(tinv[c]), bd(pw[c])], axis=1).astype(BF16))
                tinv[c] = tinv[c] + res[:, :PAIR]
                pw[c] = res[:, PAIR:]
        fill()

    s_cur = [s_ref[pi] for pi in range(n_pairs)]
    y_rows = []
    for ci in range(n_sub):
        cs = [(ci, pi) for pi in range(n_pairs)]
        xs = {c: _dot_nt(x[c], s_cur[c[1]].astype(BF16)) for c in cs}
        fill()
        u = {c: _dot(tinv[c].astype(BF16), bd(xs[c][:c_len] + mv[c][:c_len]).astype(BF16))
             for c in cs}
        fill()
        y_cols = []
        for c in cs:
            y_cols.append(xs[c][c_len:] + mv[c][c_len:] + _dot(q_b[c], bd(u[c]).astype(BF16)))
            uv = jnp.concatenate([u[c].astype(BF16), ld("vl", c) + ld("vh", c)], axis=0)
            bk = jnp.concatenate([ld("be", c), ld("ke", c)], axis=0)
            p_last = plast_ref[rd, ci:ci + 1, ls[c]]
            s_cur[c[1]] = s_cur[c[1]] * p_last + jnp.where(same_head, _dot_tn(uv, bk), 0.0)
        y_rows.append(jnp.concatenate(y_cols, axis=1))
        fill()
    for pi in range(n_pairs):
        s_ref[pi] = s_cur[pi]
    return jnp.concatenate(y_rows, axis=0)


def _wkv_seq_kernel(*refs, has_vres, emit_v, blocks_per_seq):
    (r_ref, k_ref, v_ref, lo_ref, z_ref, pr_ref, pk_ref, pv_ref, plo_ref,
     mur_ref, muk_ref, muv_ref, mulo_ref, w0_ref, w2_ref, a0_ref, a2_ref,
     kk_ref, ka_ref, rk_ref, lg_ref, lb_ref, s0_ref) = refs[:23]
    pos = 23
    if has_vres:
        vf_ref, v0_ref, v1_ref, v2_ref = refs[pos:pos + 4]
        pos += 4
    o_ref = refs[pos]
    pos += 1
    if emit_v:
        vo_ref = refs[pos]
        pos += 1
    s_ref, cr_ref, ck_ref, cv_ref, clo_ref = refs[pos:pos + 5]
    pos += 5
    ops = dict(zip(_CHAIN_OPERANDS, refs[pos:pos + len(_CHAIN_OPERANDS)]))
    vs_ref, bonus_ref, plast_ref = refs[pos + len(_CHAIN_OPERANDS):]

    j = pl.program_id(0)
    first = (j % blocks_per_seq) == 0
    wr = j % 2
    rd = 1 - wr

    @pl.when(j == 0)
    def _():
        for ref in list(ops.values()) + [vs_ref, bonus_ref]:
            ref[1] = jnp.zeros(ref.shape[1:], ref.dtype)
        plast_ref[1] = jnp.ones(plast_ref.shape[1:], F32)

    @pl.when((jnp.maximum(j - 1, 0) % blocks_per_seq) == 0)
    def _():
        s_ref[...] = s0_ref[...]

    rows, width = vs_ref.shape[1:]
    n_sub = rows // WKV_CHUNK
    n_lane_blocks = width // LANES
    ones_bd = _head_ones()
    tri = _chunk_tri(rows)
    shared, mid = {}, {}

    def shared_vector_part():
        lo = _shift_seq(lo_ref[...], mulo_ref[...], plo_ref[...], clo_ref, slice(None), first)
        shared["tanh_lo"] = jnp.tanh(lo).astype(BF16)
        shared["lo"] = lo.astype(BF16)
        shared["v"] = _shift_seq(v_ref[...], muv_ref[...], pv_ref[...], cv_ref, slice(None), first)

    def shared_matmul_part():
        shared["w_lora"] = _dot(shared["tanh_lo"], w2_ref[...])
        shared["a_lora"] = _dot(shared["lo"], a2_ref[...])
        if has_vres:
            shared["vv_mid"] = _dot(shared["v"].astype(BF16), v1_ref[...])

    def shared_matmul_part2():
        if has_vres:
            shared["vv"] = _dot(shared["vv_mid"].astype(BF16), v2_ref[...])

    def block_vector_part(cb):
        sl = slice(cb * LANES, (cb + 1) * LANES)
        r = _shift_seq(r_ref[:, sl], mur_ref[:, sl], pr_ref[:, sl], cr_ref, sl, first)
        k = _shift_seq(k_ref[:, sl], muk_ref[:, sl], pk_ref[:, sl], ck_ref, sl, first)
        v = shared["v"][:, sl]
        if has_vres:
            v = v + (vf_ref[:, sl] - v) * jax.nn.sigmoid(v0_ref[:, sl] + shared["vv"][:, sl])
        lw, a = _decay_and_iclr(w0_ref[:, sl] + shared["w_lora"][:, sl],
                                a0_ref[:, sl] + shared["a_lora"][:, sl])
        kk = k * kk_ref[:, sl]
        k = k * (1.0 + (a - 1.0) * ka_ref[:, sl])
        mid[cb] = dict(r=r, k=k, v=v, a=a, lw=lw, kk=kk, kk2=_split_hi_lo(kk * kk),
                       rk=_split_hi_lo(r * k * rk_ref[:, sl]), lws=_split_hi_lo(lw))

    def block_matmul_part(cb):
        sl = slice(cb * LANES, (cb + 1) * LANES)
        m = mid.pop(cb)
        kk = m["kk"] / jnp.maximum(jnp.sqrt(_dot_hi_lo(m["kk2"], ones_bd, False)), 1e-12)
        bonus = _dot_hi_lo(m["rk"], ones_bd, False) * m["v"]
        lp = _dot_hi_lo(m["lws"], tri, True)
        vals, p_last = _chain_operands(m["r"], m["lw"], lp, m["k"], m["v"], -kk, kk * m["a"])
        for name, val in zip(_CHAIN_OPERANDS, vals):
            ops[name][wr, :, sl] = val
        vs_ref[wr, :, sl] = m["v"]
        bonus_ref[wr, :, sl] = bonus
        plast_ref[wr, :n_sub, sl] = p_last

    def piece(t):
        def run():
            if 0 < t <= n_lane_blocks:
                block_matmul_part(t - 1)
            if t < n_lane_blocks:
                block_vector_part(t)
        return run

    shared_vector_part()
    shared_matmul_part()
    pending = [shared_matmul_part2] + [piece(t) for t in range(n_lane_blocks + 1)]

    slot = [0]

    def fill():
        slot[0] += 1
        if pending and slot[0] not in _EMPTY_FILL_SLOTS:
            pending.pop(0)()

    y = _wkv_chain(ops, plast_ref, rd, s_ref, fill)
    while pending:
        fill()
    o_ref[...] = _rwkv_gate(y, bonus_ref[rd], z_ref[...], lg_ref[...], lb_ref[...]
                            ).astype(o_ref.dtype)
    if emit_v:
        vo_ref[...] = vs_ref[rd]


def _wkv_seq(p, prev_rkv, prev_lo, tok, gate, s0_bd, vres, layer, *, n_seq, seq_len, rw,
             lo_col, z_blk, emit_v):
    m = p.shape[0]
    n_blocks = seq_len // SEQ_ROWS
    total = n_seq * n_blocks
    n_pairs = rw // PAIR
    prep_blk = lambda c: jnp.minimum(c, total - 1)
    run_blk = lambda c: jnp.maximum(c - 1, 0)
    row = lambda c: (prep_blk(c), 0)
    col = lambda j: (lambda c: (prep_blk(c), j))
    blk = lambda j: pl.BlockSpec((SEQ_ROWS, rw), col(j))
    in_specs = [blk(0), blk(1), blk(2), pl.BlockSpec((SEQ_ROWS, LO_W), col(lo_col // LO_W)),
                pl.BlockSpec((SEQ_ROWS, rw), lambda c: (run_blk(c), z_blk))]
    args = [p, p, p, p, p]
    for j in range(3):
        in_specs.append(pl.BlockSpec((None, None, 1, rw),
                                     lambda c, j=j: (layer, prep_blk(c) // n_blocks, 0, j)))
    in_specs.append(pl.BlockSpec((None, None, 1, LO_W),
                                 lambda c: (layer, prep_blk(c) // n_blocks, 0, 0)))
    args += [prev_rkv, prev_rkv, prev_rkv, prev_lo]
    in_specs += _token_param_specs(tok, rw, layer)
    args += list(tok)
    in_specs += [_layer_spec(t, layer) for t in gate]
    args += list(gate)
    in_specs.append(pl.BlockSpec((None, None, n_pairs, PAIR, PAIR),
                                 lambda c: (layer, run_blk(c) // n_blocks, 0, 0, 0)))
    args.append(s0_bd)
    if vres is not None:
        in_specs += _vres_specs(vres, SEQ_ROWS, rw, row, layer)
        args += list(vres)
    run_rows = pl.BlockSpec((SEQ_ROWS, rw), lambda c: (run_blk(c), 0))
    out_specs = [run_rows]
    out_shape = [jax.ShapeDtypeStruct((m, rw), BF16)]
    if emit_v:
        out_specs.append(run_rows)
        out_shape.append(jax.ShapeDtypeStruct((m, rw), F32))
    out_specs.append(pl.BlockSpec((None, n_pairs, PAIR, PAIR),
                                  lambda c: (run_blk(c) // n_blocks, 0, 0, 0)))
    out_shape.append(jax.ShapeDtypeStruct((n_seq, n_pairs, PAIR, PAIR), F32))
    outs = pl.pallas_call(
        functools.partial(_wkv_seq_kernel, has_vres=vres is not None, emit_v=emit_v,
                          blocks_per_seq=n_blocks),
        grid=(total + 1,),
        in_specs=in_specs,
        out_specs=out_specs,
        out_shape=out_shape,
        scratch_shapes=([pltpu.VMEM((1, rw), F32)] * 3 + [pltpu.VMEM((1, LO_W), F32)]
                        + [pltpu.VMEM((2, SEQ_ROWS, rw), BF16)] * len(_CHAIN_OPERANDS)
                        + [pltpu.VMEM((2, SEQ_ROWS, rw), F32)] * 2
                        + [pltpu.VMEM((2, 8, rw), F32)]),
        compiler_params=_cparams(("arbitrary",)),
        name="wkv_seq",
    )(*args)
    return (outs[0], outs[1], outs[2]) if emit_v else (outs[0], None, outs[1])


def _wkv_step_kernel(r_ref, lw_ref, k_ref, v_ref, a_ref, b_ref, s_ref, _, y_ref, so_ref,
                     vt_ref, yt_ref):
    r_t, k_t, a_t, b_t = r_ref[...].T, k_ref[...].T, a_ref[...].T, b_ref[...].T
    w_t = jnp.exp(lw_ref[...]).T
    vt_ref[...] = v_ref[...].T
    for hh in range(PAIR // HEAD):
        sl = slice(hh * HEAD, (hh + 1) * HEAD)
        r, k, a, b, w = r_t[sl], k_t[sl], a_t[sl], b_t[sl], w_t[sl]

        def body(vi, carry, hh=hh, r=r, k=k, a=a, b=b, w=w):
            s = s_ref[hh, vi]
            sa = jnp.sum(s * a, axis=0, keepdims=True)
            s_new = s * w + sa * b + vt_ref[pl.ds(hh * HEAD + vi, 1), :] * k
            so_ref[hh, vi] = s_new
            yt_ref[pl.ds(hh * HEAD + vi, 1), :] = jnp.sum(s_new * r, axis=0, keepdims=True)
            return carry

        lax.fori_loop(0, HEAD, body, 0, unroll=8)
    y_ref[...] = yt_ref[...].T


def _wkv_step(r, lw, k, v, a, b, s_all, s_out_all, layer):
    m, rw = r.shape
    hp = PAIR // HEAD
    assert m % LANES == 0
    blk = pl.BlockSpec((LANES, PAIR), lambda pi, j: (j, pi))
    sblk = pl.BlockSpec((None, hp, HEAD, HEAD, LANES), lambda pi, j: (layer, pi, 0, 0, j))
    return pl.pallas_call(
        _wkv_step_kernel,
        grid=(rw // PAIR, m // LANES),
        in_specs=[blk] * 6 + [sblk, pl.BlockSpec(memory_space=pl.ANY)],
        out_specs=[blk, sblk],
        out_shape=[jax.ShapeDtypeStruct((m, rw), F32),
                   jax.ShapeDtypeStruct(s_out_all.shape, F32)],
        scratch_shapes=[pltpu.VMEM((PAIR, LANES), F32), pltpu.VMEM((PAIR, LANES), F32)],
        input_output_aliases={7: 1},
        compiler_params=_cparams(("parallel", "parallel")),
        name="wkv_step",
    )(r, lw, k, v, a, b, s_all, s_out_all)


def _gmlp_mix(vg, g, b, ws_ref, bias_ref, seq):
    mu = jnp.mean(vg, axis=-1, keepdims=True)
    d = vg - mu
    var = jnp.mean(d * d, axis=-1, keepdims=True)
    vn = d * lax.rsqrt(var + LN_EPS) * g + b
    if not seq:
        return vn * ws_ref[...] + bias_ref[...], vn
    tm, gw = vg.shape[0], vg.shape[1] // GMLP_GROUPS
    ri = lax.broadcasted_iota(jnp.int32, (GMLP_CHUNK, GMLP_CHUNK), 0)
    ci = lax.broadcasted_iota(jnp.int32, (GMLP_CHUNK, GMLP_CHUNK), 1)
    vb = vn.astype(BF16)
    cols = []
    for gi in range(GMLP_GROUPS):
        wg = jnp.where(ci <= ri, ws_ref[gi], 0.0).astype(BF16)
        rows = [_dot(wg, vb[c * GMLP_CHUNK:(c + 1) * GMLP_CHUNK, gi * gw:(gi + 1) * gw])
                for c in range(tm // GMLP_CHUNK)]
        cols.append(jnp.concatenate(rows, axis=0))
    bias = jnp.concatenate([bias_ref[...]] * (tm // GMLP_CHUNK), axis=0)
    return jnp.concatenate(cols, axis=1) + bias, vn


def _mix_out_kernel(*refs, seq, final):
    if seq:
        or_ref = refs[0]
        pos = 1
        o_r = or_ref[...]
    else:
        y_ref, r_ref, k_ref, v_ref, zr_ref, lg_ref, lb_ref, rk_ref = refs[:8]
        pos = 8
        v = v_ref[...]
        o_r = _rwkv_gate(y_ref[...], _bonus(r_ref[...], k_ref[...], v, rk_ref[...]),
                         zr_ref[...], lg_ref[...], lb_ref[...]).astype(BF16)
    u_ref, vg_ref, zg_ref, h_ref, gg_ref, gb_ref, ws_ref, bias_ref, w_ref = refs[pos:pos + 9]
    rest = refs[pos + 9:]
    rw = o_r.shape[1]
    mixed, vn = _gmlp_mix(vg_ref[...], gg_ref[...], gb_ref[...], ws_ref, bias_ref, seq)
    o_g = u_ref[...] * mixed * _silu(zg_ref[...])
    h = h_ref[...] + _dot(o_r, w_ref[:rw, :]) + _dot(o_g.astype(BF16), w_ref[rw:, :])
    if final:
        ms = jnp.mean(h * h, axis=-1, keepdims=True)
        h = h * lax.rsqrt(ms + RMS_EPS) * rest[0][...]
    outs = rest[1:] if final else rest
    outs[0][...] = h
    if not seq:
        outs[1][...] = vn


def _mix_out(branch, p, h, gln_g, gln_b, ws, bias, w_all, layer, final_g, *, seq, tm, u_blk):
    m, d = h.shape
    gw = gln_g.shape[-1]
    rw = branch[0].shape[1]
    row = pl.BlockSpec((tm, rw), lambda i: (i, 0))
    col = lambda c: pl.BlockSpec((tm, gw), lambda i, c=c: (i, c))
    hrow = pl.BlockSpec((tm, d), lambda i: (i, 0))
    lspec = lambda t: _layer_spec(t, layer)
    if seq:
        in_specs = [row]
        args = list(branch)
    else:
        y, r, k, v, lnx_g, lnx_b, r_k = branch
        in_specs = [row, row, row, row, col(u_blk - 1), lspec(lnx_g), lspec(lnx_b), lspec(r_k)]
        args = [y, r, k, v, p, lnx_g, lnx_b, r_k]
    in_specs += [col(u_blk), col(u_blk + 1), col(u_blk + 2), hrow, lspec(gln_g), lspec(gln_b),
                 lspec(ws), lspec(bias),
                 pl.BlockSpec((None, rw + gw, d), lambda i: (layer, 0, 0),
                              pipeline_mode=pl.Buffered(1))]
    args += [p, p, p, h, gln_g, gln_b, ws, bias, w_all]
    if final_g is not None:
        in_specs.append(pl.BlockSpec((1, d), lambda i: (0, 0)))
        args.append(final_g)
    out_specs = [hrow]
    out_shape = [jax.ShapeDtypeStruct((m, d), F32)]
    if not seq:
        out_specs.append(pl.BlockSpec((tm, gw), lambda i: (i, 0)))
        out_shape.append(jax.ShapeDtypeStruct((m, gw), F32))
    outs = pl.pallas_call(
        functools.partial(_mix_out_kernel, seq=seq, final=final_g is not None),
        grid=(m // tm,),
        in_specs=in_specs,
        out_specs=out_specs,
        out_shape=out_shape,
        compiler_params=_cparams(("parallel",)),
        name="mix_out",
    )(*args)
    return (outs[0], None) if seq else (outs[0], outs[1])


def _pad_last(x, width):
    return jnp.pad(x, [(0, 0)] * (x.ndim - 1) + [(0, width - x.shape[-1])])


def _pad_rows(x, rows):
    return jnp.pad(x, [(0, 0)] * (x.ndim - 2) + [(0, rows - x.shape[-2]), (0, 0)])


def _pick_tile(m, pref, mult):
    t = min(m, pref)
    while m % t or t % mult:
        t -= mult
    return t


def _trunk(x, s_init, shift_init, prm, *, seq):
    n_seq, seq_len, d = x.shape
    m = n_seq * seq_len
    depth = prm["depth"]
    rw, gw = prm["rw"], prm["gw"]
    lora_w = prm["lora_w"]
    n_heads = rw // HEAD
    lo_col = 3 * rw + rw + 3 * gw
    tn = 512 if m > 256 else 1024
    z_blk = 3

    tm_mm = _pick_tile(m, 1024, 8)
    tm_ew = _pick_tile(m, 256, GMLP_CHUNK if seq else 8)
    assert not seq or seq_len % SEQ_ROWS == 0

    h = x.reshape(m, d)
    s_out, shift_out, vn_out = [], [], []
    prev_rkv = shift_init[:, :, :3 * rw]
    prev_lo = _pad_last(shift_init[:, :, 3 * rw:], LO_W)
    if seq:
        prev_rkv, prev_lo = prev_rkv[:, :, None, :], prev_lo[:, :, None, :]
        s0 = s_init.reshape(depth, n_seq, n_heads // 2, 2, HEAD, HEAD)
        z = jnp.zeros_like(s0[:, :, :, 0])
        s0_bd = jnp.concatenate([jnp.concatenate([s0[:, :, :, 0], z], axis=-1),
                                 jnp.concatenate([z, s0[:, :, :, 1]], axis=-1)], axis=-2)
    else:
        s_init = jnp.transpose(s_init, (0, 2, 3, 4, 1))
        s_buf = jnp.zeros_like(s_init)
    tok = (prm["mu_rkv"],) * 3 + tuple(prm[n] for n in ("mu_lo", "w0", "w2", "a0", "a2",
                                                         "k_k", "k_a"))
    gate = (prm["r_k"], prm["lnx_g"], prm["lnx_b"])
    v_first = None
    for l in range(depth):
        p = _in_proj(h, prm["norm_g"], prm["w_in"], l, tm_mm, tn, rw, gw, lora_w)
        vres = (v_first, prm["v0"], prm["v1"], prm["v2"]) if l > 0 else None

        if seq:
            o_r, v, s_bd = _wkv_seq(p, prev_rkv, prev_lo, tok, gate, s0_bd, vres, l,
                                    n_seq=n_seq, seq_len=seq_len, rw=rw, lo_col=lo_col,
                                    z_blk=z_blk, emit_v=l == 0)
            s_out.append(s_bd)
            branch = (o_r,)
            ws, bias = prm["w_s"], prm["bias_seq"]
        else:
            r, lw, k, v, a, b = _rwkv_prep(p, prev_rkv, prev_lo, tok, vres, l,
                                           tm=tm_ew, rw=rw, lo_col=lo_col)
            y, s_buf = _wkv_step(r, lw, k, v, a, b, s_init, s_buf, l)
            branch = (y, r, k, v, gate[1], gate[2], gate[0])
            ws, bias = prm["w_s0"], prm["bias0"]
        if l == 0:
            v_first = v

        last = p.reshape(n_seq, seq_len, -1)[:, -1]
        shift_out.append(jnp.concatenate(
            [last[:, :3 * rw], last[:, lo_col:lo_col + lora_w]], axis=-1))

        h, vn = _mix_out(branch, p, h, prm["gln_g"], prm["gln_b"], ws, bias,
                         prm["w_out"], l, prm["final_g"] if l == depth - 1 else None,
                         seq=seq, tm=tm_ew, u_blk=z_blk + 1)
        if not seq:
            vn_out.append(vn.reshape(n_seq, seq_len, gw))
    if seq:
        s_bd = jnp.stack(s_out)
        s_final = jnp.stack([s_bd[..., :HEAD, :HEAD], s_bd[..., HEAD:, HEAD:]], axis=3)
        s_final = s_final.reshape(depth, n_seq, n_heads, HEAD, HEAD)
    else:
        s_final = jnp.transpose(s_buf, (0, 4, 1, 2, 3))
    return h.reshape(n_seq, seq_len, d), s_final, shift_out, vn_out


def kernel(x_prompt, x_sample, state_wkv, state_shift, norm_g, w_in, mu_shift, w0, w2, a0, a2,
           k_k, k_a, r_k, lnx_g, lnx_b, v0, v1, v2, gln_g, gln_b, w_s, b_s, w_out, final_g):
    depth, d, _ = w_in.shape
    rw = w0.shape[1]
    gw = gln_g.shape[1]
    dl, il = w2.shape[1], a2.shape[1]
    assert rw % PAIR == 0 and gw == rw and w_s.shape[2] == GMLP_CHUNK
    assert gw // GMLP_GROUPS == LANES and dl + il <= LO_W
    sc = 3 * rw + dl + il
    in_cols = w_in.shape[2]

    row3 = lambda t: t[:, None, :]
    prm = dict(
        depth=depth, rw=rw, gw=gw, lora_w=dl + il,
        norm_g=row3(norm_g),
        w_in=jnp.swapaxes(w_in, 1, 2).astype(BF16).reshape(depth * in_cols, d),
        mu_rkv=row3(mu_shift[:, :3 * rw]),
        mu_lo=row3(_pad_last(mu_shift[:, 3 * rw:], LO_W)),
        w0=row3(w0), w2=_pad_rows(w2, LO_W).astype(BF16),
        a0=row3(a0),
        a2=jnp.pad(a2, ((0, 0), (dl, LO_W - dl - il), (0, 0))).astype(BF16),
        k_k=row3(k_k), k_a=row3(k_a), r_k=row3(r_k.reshape(depth, rw)),
        lnx_g=row3(lnx_g), lnx_b=row3(lnx_b),
        v0=row3(v0), v1=_pad_last(v1, LORA_PAD).astype(BF16), v2=_pad_rows(v2, LORA_PAD).astype(BF16),
        gln_g=row3(gln_g), gln_b=row3(gln_b),
        w_s=w_s,
        bias_seq=jnp.repeat(jnp.swapaxes(b_s, 1, 2), gw // GMLP_GROUPS, axis=2),
        w_s0=row3(jnp.repeat(w_s[:, :, 0, 0], gw // GMLP_GROUPS, axis=1)),
        bias0=row3(jnp.repeat(b_s[:, :, 0], gw // GMLP_GROUPS, axis=1)),
        w_out=w_out.astype(BF16),
        final_g=final_g[None, :],
    )

    nb = x_prompt.shape[0]
    n_heads = rw // HEAD
    s0_p = jnp.zeros((depth, nb, n_heads, HEAD, HEAD), F32)
    sh0_p = jnp.zeros((depth, nb, sc), F32)
    y_p, s_p, sh_p, _ = _trunk(x_prompt, s0_p, sh0_p, prm, seq=True)
    y_s, s_s, sh_s, vn_s = _trunk(x_sample, state_wkv, state_shift, prm, seq=False)
    return (y_p, y_s, s_p, jnp.stack(sh_p), s_s, jnp.stack(sh_s), jnp.stack(vn_s))
```

```python
import functools

import jax
import jax.numpy as jnp
from jax import lax
from jax.experimental import pallas as pl
from jax.experimental.pallas import tpu as pltpu

F32 = jnp.float32
BF16 = jnp.bfloat16

HEAD = 64
LANES = 128
PAIR = 2 * HEAD
WKV_CHUNK = HEAD
SEQ_ROWS = 2 * WKV_CHUNK
GMLP_CHUNK = 128
GMLP_GROUPS = 8
LORA_PAD = 128
LO_W = 256
RMS_EPS = 1e-6
LN_EPS = 1e-5
LNX_EPS = 64e-5
VMEM_LIMIT = 56 * 1024 * 1024

_NT = (((1,), (1,)), ((), ()))
_TN = (((0,), (0,)), ((), ()))


def _dot(a, b):
    return jnp.dot(a, b, preferred_element_type=F32)


def _dot_nt(a, b):
    return lax.dot_general(a, b, _NT, preferred_element_type=F32)


def _dot_tn(a, b):
    return lax.dot_general(a, b, _TN, preferred_element_type=F32)


def _cparams(sem):
    return pltpu.CompilerParams(dimension_semantics=sem, vmem_limit_bytes=VMEM_LIMIT)


def _silu(z):
    return z * jax.nn.sigmoid(z)


def _head_ones():
    r = lax.broadcasted_iota(jnp.int32, (LANES, LANES), 0) // HEAD
    c = lax.broadcasted_iota(jnp.int32, (LANES, LANES), 1) // HEAD
    return jnp.where(r == c, 1.0, 0.0).astype(BF16)


def _split_hi_lo(x):
    hi = x.astype(BF16)
    return hi, (x - hi.astype(F32)).astype(BF16)


def _dot_hi_lo(hi_lo, mat, mat_left):
    hi, lo = hi_lo
    return (_dot(mat, hi) + _dot(mat, lo)) if mat_left else (_dot(hi, mat) + _dot(lo, mat))


def _head_sum_hi_lo(hi_lo, ones_bd):
    hi, lo = hi_lo
    outs = [_dot_hi_lo((hi[:, j * LANES:(j + 1) * LANES], lo[:, j * LANES:(j + 1) * LANES]),
                       ones_bd, False) for j in range(hi.shape[1] // LANES)]
    return jnp.concatenate(outs, axis=1)


def _head_sum(x, ones_bd):
    return _head_sum_hi_lo(_split_hi_lo(x), ones_bd)


def _chunk_tri(rows):
    ti = lax.broadcasted_iota(jnp.int32, (rows, rows), 0)
    si = lax.broadcasted_iota(jnp.int32, (rows, rows), 1)
    same_chunk = si // WKV_CHUNK == ti // WKV_CHUNK
    return jnp.where((si <= ti) & same_chunk, 1.0, 0.0).astype(BF16)


def _decay_and_iclr(wraw, araw):
    w = -jax.nn.softplus(-wraw) - 0.5
    return -jnp.exp(w), jax.nn.sigmoid(araw)


_NORM_ROWS = 256


def _in_proj_kernel(h_ref, g_ref, w_ref, wlo_ref, o_ref, lo_ref, xn_ref):
    @pl.when(pl.program_id(1) == 0)
    def _():
        g = g_ref[...]
        chunk = min(_NORM_ROWS, h_ref.shape[0])

        def norm_rows(c, carry):
            rows = pl.ds(pl.multiple_of(c * chunk, chunk), chunk)
            x = h_ref[rows, :]
            ms = jnp.mean(x * x, axis=-1, keepdims=True)
            xn_ref[rows, :] = (x * lax.rsqrt(ms + RMS_EPS) * g).astype(BF16)
            return carry

        lax.fori_loop(0, h_ref.shape[0] // chunk, norm_rows, 0)
        lo_ref[...] = _dot_nt(xn_ref[...], wlo_ref[...])

    o_ref[...] = _dot_nt(xn_ref[...], w_ref[...])


_ROW_ALIGN = 16


def _in_proj(h, g, wt_rows, layer, tm, tn, rw, gw, lora_w):
    m, d = h.shape
    in_cols = 3 * rw + lora_w + rw + 3 * gw
    n_rkv = 3 * rw // tn
    n_main = n_rkv + (rw + 3 * gw) // tn
    assert (3 * rw) % tn == 0 and (rw + 3 * gw) % tn == 0 and lora_w <= LO_W
    assert in_cols % _ROW_ALIGN == 0 and tn % _ROW_ALIGN == 0 and lora_w % _ROW_ALIGN == 0
    base = layer * in_cols // _ROW_ALIGN
    step = tn // _ROW_ALIGN
    z0 = (3 * rw + lora_w) // _ROW_ALIGN
    lo0 = 3 * rw // _ROW_ALIGN

    def w_index(i, j):
        start = jnp.where(j < n_rkv, step * j, z0 + step * (j - n_rkv))
        return ((base + start) * _ROW_ALIGN, 0)

    return pl.pallas_call(
        _in_proj_kernel,
        grid=(m // tm, n_main),
        in_specs=[
            pl.BlockSpec((tm, d), lambda i, j: (i, 0)),
            _layer_spec(g, layer),
            pl.BlockSpec((pl.Element(tn), pl.Element(d)), w_index),
            pl.BlockSpec((pl.Element(LO_W), pl.Element(d)),
                         lambda i, j: ((base + lo0) * _ROW_ALIGN, 0)),
        ],
        out_specs=[pl.BlockSpec((tm, tn), lambda i, j: (i, j)),
                   pl.BlockSpec((tm, LO_W), lambda i, j: (i, 0))],
        out_shape=[jax.ShapeDtypeStruct((m, n_main * tn), F32),
                   jax.ShapeDtypeStruct((m, LO_W), F32)],
        scratch_shapes=[pltpu.VMEM((tm, d), BF16)],
        compiler_params=_cparams(("parallel", "arbitrary")),
        name="in_proj",
    )(h, g, wt_rows, wt_rows)


def _rwkv_tokens(k, v, tanh_lo, lo, w0, w2, a0, a2, k_k, k_a, vres):
    lw, a = _decay_and_iclr(w0 + _dot(tanh_lo, w2[...]), a0 + _dot(lo, a2[...]))
    if vres is not None:
        v_first, v0, vv_mid, v2 = vres
        v = v + (v_first - v) * jax.nn.sigmoid(v0 + _dot(vv_mid, v2[...]))
    kk = k * k_k
    kk = kk / jnp.maximum(jnp.sqrt(_head_sum(kk * kk, _head_ones())), 1e-12)
    return lw, k * (1.0 + (a - 1.0) * k_a), v, -kk, kk * a


def _rwkv_gate(y, bonus, z, g, b):
    ones_bd = _head_ones()
    mu = _head_sum(y, ones_bd) * (1.0 / HEAD)
    d = y - mu
    var = _head_sum(d * d, ones_bd) * (1.0 / HEAD)
    return (d * lax.rsqrt(var + LNX_EPS) * g + b + bonus) * _silu(z)


def _bonus(r, k, v, rk):
    return _head_sum(r * k * rk, _head_ones()) * v


def _rwkv_prep_kernel(*refs, has_vres):
    (r_ref, k_ref, v_ref, lo_ref, pr_ref, pk_ref, pv_ref, plo_ref,
     mur_ref, muk_ref, muv_ref, mulo_ref, w0_ref, w2_ref, a0_ref, a2_ref,
     kk_ref, ka_ref) = refs[:18]
    pos = 18
    if has_vres:
        vf_ref, v0_ref, v1_ref, v2_ref = refs[pos:pos + 4]
        pos += 4
    ro_ref, lwo_ref, ko_ref, vo_ref, ao_ref, bo_ref = refs[pos:pos + 6]

    def mix(x_ref, p_ref, mu_ref):
        x = x_ref[...]
        return x + (p_ref[...] - x) * mu_ref[...]

    lo = mix(lo_ref, plo_ref, mulo_ref)
    v = mix(v_ref, pv_ref, muv_ref)
    vres = None
    if has_vres:
        vres = (vf_ref[...], v0_ref[...], _dot(v.astype(BF16), v1_ref[...]).astype(BF16), v2_ref)
    lw, k, v, a, b = _rwkv_tokens(
        mix(k_ref, pk_ref, muk_ref), v, jnp.tanh(lo).astype(BF16), lo.astype(BF16),
        w0_ref[...], w2_ref, a0_ref[...], a2_ref, kk_ref[...], ka_ref[...], vres)
    ro_ref[...] = mix(r_ref, pr_ref, mur_ref)
    lwo_ref[...] = lw
    ko_ref[...] = k
    vo_ref[...] = v
    ao_ref[...] = a
    bo_ref[...] = b


def _layer_spec(arr, layer):
    zeros = (0,) * (arr.ndim - 1)
    return pl.BlockSpec((None,) + arr.shape[1:], lambda *g: (layer,) + zeros)


def _token_param_specs(tok, rw, layer):
    return ([pl.BlockSpec((None, 1, rw), lambda *g, c=c: (layer, 0, c)) for c in range(3)]
            + [_layer_spec(t, layer) for t in tok[3:]])


def _vres_specs(vres, rows, rw, row_index, layer):
    return ([pl.BlockSpec((rows, rw), row_index)]
            + [_layer_spec(t, layer - 1) for t in vres[1:]])


def _rwkv_prep(p, p_lo, prev_rkv, prev_lo, tok, vres, layer, *, tm, rw):
    m = p.shape[0]
    row = lambda i: (i, 0)
    col = lambda c: (lambda i: (i, c))
    in_specs = [pl.BlockSpec((tm, rw), col(0)), pl.BlockSpec((tm, rw), col(1)),
                pl.BlockSpec((tm, rw), col(2)), pl.BlockSpec((tm, LO_W), row)]
    in_specs += [pl.BlockSpec((None, tm, rw), lambda i, c=c: (layer, i, c)) for c in range(3)]
    in_specs.append(pl.BlockSpec((None, tm, LO_W), lambda i: (layer, i, 0)))
    args = [p, p, p, p_lo, prev_rkv, prev_rkv, prev_rkv, prev_lo]
    in_specs += _token_param_specs(tok, rw, layer)
    args += list(tok)
    if vres is not None:
        in_specs += _vres_specs(vres, tm, rw, row, layer)
        args += list(vres)
    out = jax.ShapeDtypeStruct((m, rw), F32)
    return pl.pallas_call(
        functools.partial(_rwkv_prep_kernel, has_vres=vres is not None),
        grid=(m // tm,),
        in_specs=in_specs,
        out_specs=[pl.BlockSpec((tm, rw), row)] * 6,
        out_shape=[out] * 6,
        compiler_params=_cparams(("parallel",)),
        name="rwkv_prep",
    )(*args)


def _shift_seq(x, mu, init_row, carry_ref, sl, first):
    prev0 = jnp.where(first, init_row, carry_ref[:, sl])
    sh = pltpu.roll(x, 1, 0)
    rows = lax.broadcasted_iota(jnp.int32, x.shape, 0)
    sh = jnp.where(rows == 0, prev0, sh)
    carry_ref[:, sl] = x[x.shape[0] - 1:, :]
    return x + (sh - x) * mu


_EMPTY_FILL_SLOTS = (2, 4, 6, 8)
_CHAIN_OPERANDS = ("at", "rt", "btl", "bth", "ktl", "kth", "vl", "vh", "be", "ke")


def _chain_operands(r, lw, lp, k, v, a, b):
    rows, width = r.shape
    c_len = WKV_CHUNK
    n_sub = rows // c_len
    p_ends = [jnp.exp(lp[(ci + 1) * c_len - 1:(ci + 1) * c_len]) for ci in range(n_sub)]
    pinv = jnp.exp(-lp)
    pend = pinv * jnp.concatenate(
        [jnp.broadcast_to(e, (c_len, width)) for e in p_ends], axis=0)
    first_head = lax.broadcasted_iota(jnp.int32, (rows, width), 1) < HEAD
    halves = lambda z: (jnp.where(first_head, z, 0.0), jnp.where(first_head, 0.0, z))
    ops = ((a * jnp.exp(lp - lw), r * jnp.exp(lp)) + halves(b * pinv) + halves(k * pinv)
           + halves(v) + (b * pend, k * pend))
    return [o.astype(BF16) for o in ops], jnp.concatenate(p_ends, axis=0)


def _wkv_chain(ops, plast_ref, rd, s_ref, fill):
    _, rows, width = ops["at"].shape
    c_len = WKV_CHUNK
    n_sub = rows // c_len
    n_pairs = width // PAIR

    lane = lax.broadcasted_iota(jnp.int32, (c_len, PAIR), 1)
    rowi = lax.broadcasted_iota(jnp.int32, (c_len, PAIR), 0)
    lo_half = lane < HEAD
    scol = lane % HEAD
    strict = scol < rowi
    incl = scol <= rowi
    r2 = lax.broadcasted_iota(jnp.int32, (PAIR, PAIR), 0)
    c2 = lax.broadcasted_iota(jnp.int32, (PAIR, PAIR), 1)
    same_head = (r2 // HEAD) == (c2 // HEAD)
    eye_pair = jnp.where(scol == rowi, 1.0, 0.0).astype(F32)

    def bd(z):
        return jnp.concatenate([jnp.where(lo_half, z, 0.0), jnp.where(lo_half, 0.0, z)], axis=0)

    chains = [(ci, pi) for ci in range(n_sub) for pi in range(n_pairs)]
    rs = {c: slice(c[0] * c_len, (c[0] + 1) * c_len) for c in chains}
    ls = {c: slice(c[1] * PAIR, (c[1] + 1) * PAIR) for c in chains}
    ld = lambda name, c: ops[name][rd, rs[c], ls[c]]
    x, g = {}, {}
    for c in chains:
        x[c] = jnp.concatenate([ld("at", c), ld("rt", c)], axis=0)
        rhs = jnp.concatenate([ld("btl", c), ld("bth", c), ld("ktl", c), ld("kth", c)], axis=0)
        g[c] = _dot_nt(x[c], rhs)
    fill()
    q_b = {c: jnp.where(incl, g[c][c_len:, :PAIR], 0.0).astype(BF16) for c in chains}
    mv = {}
    for c in chains:
        gk = g[c][:, PAIR:]
        gkm = jnp.concatenate([jnp.where(strict, gk[:c_len], 0.0),
                               jnp.where(incl, gk[c_len:], 0.0)], axis=0)
        mv[c] = _dot(gkm.astype(BF16), jnp.concatenate([ld("vl", c), ld("vh", c)], axis=0))
    fill()

    pw = {c: jnp.where(strict, g[c][:c_len, :PAIR], 0.0) for c in chains}
    tinv = {c: eye_pair + pw[c] for c in chains}
    pw = {c: _dot(pw[c].astype(BF16), bd(pw[c]).astype(BF16)) for c in chains}
    fill()
    n_sq = (c_len - 1).bit_length() - 1
    for j in range(n_sq):
        for c in chains:
            pb = pw[c].astype(BF16)
            if j == n_sq - 1:
                tinv[c] = tinv[c] + _dot(pb, bd(tinv[c]).astype(BF16))
            else:
                res = _dot(pb, jnp.concatenate([bd(tinv[c]), bd(pw[c])], axis=1).astype(BF16))
                tinv[c] = tinv[c] + res[:, :PAIR]
                pw[c] = res[:, PAIR:]
        fill()

    s_cur = [s_ref[pi] for pi in range(n_pairs)]
    y_rows = []
    for ci in range(n_sub):
        cs = [(ci, pi) for pi in range(n_pairs)]
        xs = {c: _dot_nt(x[c], s_cur[c[1]].astype(BF16)) for c in cs}
        fill()
        u = {c: _dot(tinv[c].astype(BF16), bd(xs[c][:c_len] + mv[c][:c_len]).astype(BF16))
             for c in cs}
        fill()
        y_cols = []
        for c in cs:
            y_cols.append(xs[c][c_len:] + mv[c][c_len:] + _dot(q_b[c], bd(u[c]).astype(BF16)))
            uv = jnp.concatenate([u[c].astype(BF16), ld("vl", c) + ld("vh", c)], axis=0)
            bk = jnp.concatenate([ld("be", c), ld("ke", c)], axis=0)
            p_last = plast_ref[rd, ci:ci + 1, ls[c]]
            s_cur[c[1]] = s_cur[c[1]] * p_last + jnp.where(same_head, _dot_tn(uv, bk), 0.0)
        y_rows.append(jnp.concatenate(y_cols, axis=1))
        fill()
    for pi in range(n_pairs):
        s_ref[pi] = s_cur[pi]
    return jnp.concatenate(y_rows, axis=0)


def _wkv_seq_kernel(*refs, has_vres, emit_v, blocks_per_seq):
    (r_ref, k_ref, v_ref, lo_ref, z_ref, pr_ref, pk_ref, pv_ref, plo_ref,
     mur_ref, muk_ref, muv_ref, mulo_ref, w0_ref, w2_ref, a0_ref, a2_ref,
     kk_ref, ka_ref, rk_ref, lg_ref, lb_ref, s0_ref) = refs[:23]
    pos = 23
    if has_vres:
        vf_ref, v0_ref, v1_ref, v2_ref = refs[pos:pos + 4]
        pos += 4
    o_ref = refs[pos]
    pos += 1
    if emit_v:
        vo_ref = refs[pos]
        pos += 1
    s_ref, cr_ref, ck_ref, cv_ref, clo_ref = refs[pos:pos + 5]
    pos += 5
    ops = dict(zip(_CHAIN_OPERANDS, refs[pos:pos + len(_CHAIN_OPERANDS)]))
    vs_ref, bonus_ref, plast_ref = refs[pos + len(_CHAIN_OPERANDS):]

    j = pl.program_id(0)
    first = (j % blocks_per_seq) == 0
    wr = j % 2
    rd = 1 - wr

    @pl.when(j == 0)
    def _():
        for ref in list(ops.values()) + [vs_ref, bonus_ref]:
            ref[1] = jnp.zeros(ref.shape[1:], ref.dtype)
        plast_ref[1] = jnp.ones(plast_ref.shape[1:], F32)

    @pl.when((jnp.maximum(j - 1, 0) % blocks_per_seq) == 0)
    def _():
        s_ref[...] = s0_ref[...]

    rows, width = vs_ref.shape[1:]
    n_sub = rows // WKV_CHUNK
    n_lane_blocks = width // LANES
    ones_bd = _head_ones()
    tri = _chunk_tri(rows)
    shared, mid = {}, {}

    def shared_vector_part():
        lo = _shift_seq(lo_ref[...], mulo_ref[...], plo_ref[...], clo_ref, slice(None), first)
        shared["tanh_lo"] = jnp.tanh(lo).astype(BF16)
        shared["lo"] = lo.astype(BF16)
        shared["v"] = _shift_seq(v_ref[...], muv_ref[...], pv_ref[...], cv_ref, slice(None), first)

    def shared_matmul_part():
        shared["w_lora"] = _dot(shared["tanh_lo"], w2_ref[...])
        shared["a_lora"] = _dot(shared["lo"], a2_ref[...])
        if has_vres:
            shared["vv_mid"] = _dot(shared["v"].astype(BF16), v1_ref[...])

    def shared_matmul_part2():
        if has_vres:
            shared["vv"] = _dot(shared["vv_mid"].astype(BF16), v2_ref[...])

    def block_vector_part(cb):
        sl = slice(cb * LANES, (cb + 1) * LANES)
        r = _shift_seq(r_ref[:, sl], mur_ref[:, sl], pr_ref[:, sl], cr_ref, sl, first)
        k = _shift_seq(k_ref[:, sl], muk_ref[:, sl], pk_ref[:, sl], ck_ref, sl, first)
        v = shared["v"][:, sl]
        if has_vres:
            v = v + (vf_ref[:, sl] - v) * jax.nn.sigmoid(v0_ref[:, sl] + shared["vv"][:, sl])
        lw, a = _decay_and_iclr(w0_ref[:, sl] + shared["w_lora"][:, sl],
                                a0_ref[:, sl] + shared["a_lora"][:, sl])
        kk = k * kk_ref[:, sl]
        k = k * (1.0 + (a - 1.0) * ka_ref[:, sl])
        mid[cb] = dict(r=r, k=k, v=v, a=a, lw=lw, kk=kk, kk2=_split_hi_lo(kk * kk),
                       rk=_split_hi_lo(r * k * rk_ref[:, sl]), lws=_split_hi_lo(lw))

    def block_matmul_part(cb):
        sl = slice(cb * LANES, (cb + 1) * LANES)
        m = mid.pop(cb)
        kk = m["kk"] / jnp.maximum(jnp.sqrt(_dot_hi_lo(m["kk2"], ones_bd, False)), 1e-12)
        bonus = _dot_hi_lo(m["rk"], ones_bd, False) * m["v"]
        lp = _dot_hi_lo(m["lws"], tri, True)
        vals, p_last = _chain_operands(m["r"], m["lw"], lp, m["k"], m["v"], -kk, kk * m["a"])
        for name, val in zip(_CHAIN_OPERANDS, vals):
            ops[name][wr, :, sl] = val
        vs_ref[wr, :, sl] = m["v"]
        bonus_ref[wr, :, sl] = bonus
        plast_ref[wr, :n_sub, sl] = p_last

    def piece(t):
        def run():
            if 0 < t <= n_lane_blocks:
                block_matmul_part(t - 1)
            if t < n_lane_blocks:
                block_vector_part(t)
        return run

    shared_vector_part()
    shared_matmul_part()
    pending = [shared_matmul_part2] + [piece(t) for t in range(n_lane_blocks + 1)]

    slot = [0]

    def fill():
        slot[0] += 1
        if pending and slot[0] not in _EMPTY_FILL_SLOTS:
            pending.pop(0)()

    y = _wkv_chain(ops, plast_ref, rd, s_ref, fill)
    while pending:
        fill()
    o_ref[...] = _rwkv_gate(y, bonus_ref[rd], z_ref[...], lg_ref[...], lb_ref[...]
                            ).astype(o_ref.dtype)
    if emit_v:
        vo_ref[...] = vs_ref[rd]


def _wkv_seq(p, p_lo, prev_rkv, prev_lo, tok, gate, s0_bd, vres, layer, *, n_seq, seq_len, rw,
             z_blk, emit_v):
    m = p.shape[0]
    n_blocks = seq_len // SEQ_ROWS
    total = n_seq * n_blocks
    n_pairs = rw // PAIR
    prep_blk = lambda c: jnp.minimum(c, total - 1)
    run_blk = lambda c: jnp.maximum(c - 1, 0)
    row = lambda c: (prep_blk(c), 0)
    col = lambda j: (lambda c: (prep_blk(c), j))
    blk = lambda j: pl.BlockSpec((SEQ_ROWS, rw), col(j))
    in_specs = [blk(0), blk(1), blk(2), pl.BlockSpec((SEQ_ROWS, LO_W), row),
                pl.BlockSpec((SEQ_ROWS, rw), lambda c: (run_blk(c), z_blk))]
    args = [p, p, p, p_lo, p]
    for j in range(3):
        in_specs.append(pl.BlockSpec((None, None, 1, rw),
                                     lambda c, j=j: (layer, prep_blk(c) // n_blocks, 0, j)))
    in_specs.append(pl.BlockSpec((None, None, 1, LO_W),
                                 lambda c: (layer, prep_blk(c) // n_blocks, 0, 0)))
    args += [prev_rkv, prev_rkv, prev_rkv, prev_lo]
    in_specs += _token_param_specs(tok, rw, layer)
    args += list(tok)
    in_specs += [_layer_spec(t, layer) for t in gate]
    args += list(gate)
    in_specs.append(pl.BlockSpec((None, None, n_pairs, PAIR, PAIR),
                                 lambda c: (layer, run_blk(c) // n_blocks, 0, 0, 0)))
    args.append(s0_bd)
    if vres is not None:
        in_specs += _vres_specs(vres, SEQ_ROWS, rw, row, layer)
        args += list(vres)
    run_rows = pl.BlockSpec((SEQ_ROWS, rw), lambda c: (run_blk(c), 0))
    out_specs = [run_rows]
    out_shape = [jax.ShapeDtypeStruct((m, rw), BF16)]
    if emit_v:
        out_specs.append(run_rows)
        out_shape.append(jax.ShapeDtypeStruct((m, rw), F32))
    out_specs.append(pl.BlockSpec((None, n_pairs, PAIR, PAIR),
                                  lambda c: (run_blk(c) // n_blocks, 0, 0, 0)))
    out_shape.append(jax.ShapeDtypeStruct((n_seq, n_pairs, PAIR, PAIR), F32))
    outs = pl.pallas_call(
        functools.partial(_wkv_seq_kernel, has_vres=vres is not None, emit_v=emit_v,
                          blocks_per_seq=n_blocks),
        grid=(total + 1,),
        in_specs=in_specs,
        out_specs=out_specs,
        out_shape=out_shape,
        scratch_shapes=([pltpu.VMEM((1, rw), F32)] * 3 + [pltpu.VMEM((1, LO_W), F32)]
                        + [pltpu.VMEM((2, SEQ_ROWS, rw), BF16)] * len(_CHAIN_OPERANDS)
                        + [pltpu.VMEM((2, SEQ_ROWS, rw), F32)] * 2
                        + [pltpu.VMEM((2, 8, rw), F32)]),
        compiler_params=_cparams(("arbitrary",)),
        name="wkv_seq",
    )(*args)
    return (outs[0], outs[1], outs[2]) if emit_v else (outs[0], None, outs[1])


def _wkv_step_kernel(r_ref, lw_ref, k_ref, v_ref, a_ref, b_ref, s_ref, _, y_ref, so_ref,
                     vt_ref, yt_ref):
    r_t, k_t, a_t, b_t = r_ref[...].T, k_ref[...].T, a_ref[...].T, b_ref[...].T
    w_t = jnp.exp(lw_ref[...]).T
    vt_ref[...] = v_ref[...].T
    for hh in range(PAIR // HEAD):
        sl = slice(hh * HEAD, (hh + 1) * HEAD)
        r, k, a, b, w = r_t[sl], k_t[sl], a_t[sl], b_t[sl], w_t[sl]

        def body(vi, carry, hh=hh, r=r, k=k, a=a, b=b, w=w):
            s = s_ref[hh, vi]
            sa = jnp.sum(s * a, axis=0, keepdims=True)
            s_new = s * w + sa * b + vt_ref[pl.ds(hh * HEAD + vi, 1), :] * k
            so_ref[hh, vi] = s_new
            yt_ref[pl.ds(hh * HEAD + vi, 1), :] = jnp.sum(s_new * r, axis=0, keepdims=True)
            return carry

        lax.fori_loop(0, HEAD, body, 0, unroll=8)
    y_ref[...] = yt_ref[...].T


def _wkv_step(r, lw, k, v, a, b, s_all, s_out_all, layer):
    m, rw = r.shape
    hp = PAIR // HEAD
    assert m % LANES == 0
    blk = pl.BlockSpec((LANES, PAIR), lambda pi, j: (j, pi))
    sblk = pl.BlockSpec((None, hp, HEAD, HEAD, LANES), lambda pi, j: (layer, pi, 0, 0, j))
    return pl.pallas_call(
        _wkv_step_kernel,
        grid=(rw // PAIR, m // LANES),
        in_specs=[blk] * 6 + [sblk, pl.BlockSpec(memory_space=pl.ANY)],
        out_specs=[blk, sblk],
        out_shape=[jax.ShapeDtypeStruct((m, rw), F32),
                   jax.ShapeDtypeStruct(s_out_all.shape, F32)],
        scratch_shapes=[pltpu.VMEM((PAIR, LANES), F32), pltpu.VMEM((PAIR, LANES), F32)],
        input_output_aliases={7: 1},
        compiler_params=_cparams(("parallel", "parallel")),
        name="wkv_step",
    )(r, lw, k, v, a, b, s_all, s_out_all)


def _gmlp_mix(vg, g, b, ws_ref, bias_ref, seq):
    mu = jnp.mean(vg, axis=-1, keepdims=True)
    d = vg - mu
    var = jnp.mean(d * d, axis=-1, keepdims=True)
    vn = d * lax.rsqrt(var + LN_EPS) * g + b
    if not seq:
        return vn * ws_ref[...] + bias_ref[...], vn
    tm, gw = vg.shape[0], vg.shape[1] // GMLP_GROUPS
    ri = lax.broadcasted_iota(jnp.int32, (GMLP_CHUNK, GMLP_CHUNK), 0)
    ci = lax.broadcasted_iota(jnp.int32, (GMLP_CHUNK, GMLP_CHUNK), 1)
    vb = vn.astype(BF16)
    cols = []
    for gi in range(GMLP_GROUPS):
        wg = jnp.where(ci <= ri, ws_ref[gi], 0.0).astype(BF16)
        rows = [_dot(wg, vb[c * GMLP_CHUNK:(c + 1) * GMLP_CHUNK, gi * gw:(gi + 1) * gw])
                for c in range(tm // GMLP_CHUNK)]
        cols.append(jnp.concatenate(rows, axis=0))
    bias = jnp.concatenate([bias_ref[...]] * (tm // GMLP_CHUNK), axis=0)
    return jnp.concatenate(cols, axis=1) + bias, vn


def _mix_out_kernel(*refs, seq, final):
    if seq:
        or_ref = refs[0]
        pos = 1
        o_r = or_ref[...]
    else:
        y_ref, r_ref, k_ref, v_ref, zr_ref, lg_ref, lb_ref, rk_ref = refs[:8]
        pos = 8
        v = v_ref[...]
        o_r = _rwkv_gate(y_ref[...], _bonus(r_ref[...], k_ref[...], v, rk_ref[...]),
                         zr_ref[...], lg_ref[...], lb_ref[...]).astype(BF16)
    u_ref, vg_ref, zg_ref, h_ref, gg_ref, gb_ref, ws_ref, bias_ref, w_ref = refs[pos:pos + 9]
    rest = refs[pos + 9:]
    rw = o_r.shape[1]
    mixed, vn = _gmlp_mix(vg_ref[...], gg_ref[...], gb_ref[...], ws_ref, bias_ref, seq)
    o_g = u_ref[...] * mixed * _silu(zg_ref[...])
    h = h_ref[...] + _dot(o_r, w_ref[:rw, :]) + _dot(o_g.astype(BF16), w_ref[rw:, :])
    if final:
        ms = jnp.mean(h * h, axis=-1, keepdims=True)
        h = h * lax.rsqrt(ms + RMS_EPS) * rest[0][...]
    outs = rest[1:] if final else rest
    outs[0][...] = h
    if not seq:
        outs[1][...] = vn


def _mix_out(branch, p, h, gln_g, gln_b, ws, bias, w_all, layer, final_g, *, seq, tm, u_blk):
    m, d = h.shape
    gw = gln_g.shape[-1]
    rw = branch[0].shape[1]
    row = pl.BlockSpec((tm, rw), lambda i: (i, 0))
    col = lambda c: pl.BlockSpec((tm, gw), lambda i, c=c: (i, c))
    hrow = pl.BlockSpec((tm, d), lambda i: (i, 0))
    lspec = lambda t: _layer_spec(t, layer)
    if seq:
        in_specs = [row]
        args = list(branch)
    else:
        y, r, k, v, lnx_g, lnx_b, r_k = branch
        in_specs = [row, row, row, row, col(u_blk - 1), lspec(lnx_g), lspec(lnx_b), lspec(r_k)]
        args = [y, r, k, v, p, lnx_g, lnx_b, r_k]
    in_specs += [col(u_blk), col(u_blk + 1), col(u_blk + 2), hrow, lspec(gln_g), lspec(gln_b),
                 lspec(ws), lspec(bias),
                 pl.BlockSpec((None, rw + gw, d), lambda i: (layer, 0, 0),
                              pipeline_mode=pl.Buffered(1))]
    args += [p, p, p, h, gln_g, gln_b, ws, bias, w_all]
    if final_g is not None:
        in_specs.append(pl.BlockSpec((1, d), lambda i: (0, 0)))
        args.append(final_g)
    out_specs = [hrow]
    out_shape = [jax.ShapeDtypeStruct((m, d), F32)]
    if not seq:
        out_specs.append(pl.BlockSpec((tm, gw), lambda i: (i, 0)))
        out_shape.append(jax.ShapeDtypeStruct((m, gw), F32))
    outs = pl.pallas_call(
        functools.partial(_mix_out_kernel, seq=seq, final=final_g is not None),
        grid=(m // tm,),
        in_specs=in_specs,
        out_specs=out_specs,
        out_shape=out_shape,
        compiler_params=_cparams(("parallel",)),
        name="mix_out",
    )(*args)
    return (outs[0], None) if seq else (outs[0], outs[1])


def _pad_last(x, width):
    return jnp.pad(x, [(0, 0)] * (x.ndim - 1) + [(0, width - x.shape[-1])])


def _pad_rows(x, rows):
    return jnp.pad(x, [(0, 0)] * (x.ndim - 2) + [(0, rows - x.shape[-2]), (0, 0)])


def _pick_tile(m, pref, mult):
    t = min(m, pref)
    while m % t or t % mult:
        t -= mult
    return t


def _trunk(x, s_init, shift_init, prm, *, seq):
    n_seq, seq_len, d = x.shape
    m = n_seq * seq_len
    depth = prm["depth"]
    rw, gw = prm["rw"], prm["gw"]
    lora_w = prm["lora_w"]
    n_heads = rw // HEAD
    tn = 1024
    z_blk = 3

    tm_mm = _pick_tile(m, 1024, 8)
    tm_ew = _pick_tile(m, 512, GMLP_CHUNK if seq else 8)
    assert not seq or seq_len % SEQ_ROWS == 0

    h = x.reshape(m, d)
    s_out, shift_out, vn_out = [], [], []
    prev_rkv = shift_init[:, :, :3 * rw]
    prev_lo = _pad_last(shift_init[:, :, 3 * rw:], LO_W)
    if seq:
        prev_rkv, prev_lo = prev_rkv[:, :, None, :], prev_lo[:, :, None, :]
        s0 = s_init.reshape(depth, n_seq, n_heads // 2, 2, HEAD, HEAD)
        z = jnp.zeros_like(s0[:, :, :, 0])
        s0_bd = jnp.concatenate([jnp.concatenate([s0[:, :, :, 0], z], axis=-1),
                                 jnp.concatenate([z, s0[:, :, :, 1]], axis=-1)], axis=-2)
    else:
        s_init = jnp.transpose(s_init, (0, 2, 3, 4, 1))
        s_buf = jnp.zeros_like(s_init)
    tok = (prm["mu_rkv"],) * 3 + tuple(prm[n] for n in ("mu_lo", "w0", "w2", "a0", "a2",
                                                         "k_k", "k_a"))
    gate = (prm["r_k"], prm["lnx_g"], prm["lnx_b"])
    v_first = None
    for l in range(depth):
        p, p_lo = _in_proj(h, prm["norm_g"], prm["w_in"], l, tm_mm, tn, rw, gw, lora_w)
        vres = (v_first, prm["v0"], prm["v1"], prm["v2"]) if l > 0 else None

        if seq:
            o_r, v, s_bd = _wkv_seq(p, p_lo, prev_rkv, prev_lo, tok, gate, s0_bd, vres, l,
                                    n_seq=n_seq, seq_len=seq_len, rw=rw, z_blk=z_blk,
                                    emit_v=l == 0)
            s_out.append(s_bd)
            branch = (o_r,)
            ws, bias = prm["w_s"], prm["bias_seq"]
        else:
            r, lw, k, v, a, b = _rwkv_prep(p, p_lo, prev_rkv, prev_lo, tok, vres, l,
                                           tm=tm_ew, rw=rw)
            y, s_buf = _wkv_step(r, lw, k, v, a, b, s_init, s_buf, l)
            branch = (y, r, k, v, gate[1], gate[2], gate[0])
            ws, bias = prm["w_s0"], prm["bias0"]
        if l == 0:
            v_first = v

        last_row = lambda t: t.reshape(n_seq, seq_len, -1)[:, -1]
        shift_out.append(jnp.concatenate(
            [last_row(p)[:, :3 * rw], last_row(p_lo)[:, :lora_w]], axis=-1))

        h, vn = _mix_out(branch, p, h, prm["gln_g"], prm["gln_b"], ws, bias,
                         prm["w_out"], l, prm["final_g"] if l == depth - 1 else None,
                         seq=seq, tm=tm_ew, u_blk=z_blk + 1)
        if not seq:
            vn_out.append(vn.reshape(n_seq, seq_len, gw))
    if seq:
        s_bd = jnp.stack(s_out)
        s_final = jnp.stack([s_bd[..., :HEAD, :HEAD], s_bd[..., HEAD:, HEAD:]], axis=3)
        s_final = s_final.reshape(depth, n_seq, n_heads, HEAD, HEAD)
    else:
        s_final = jnp.transpose(s_buf, (0, 4, 1, 2, 3))
    return h.reshape(n_seq, seq_len, d), s_final, shift_out, vn_out


def kernel(x_prompt, x_sample, state_wkv, state_shift, norm_g, w_in, mu_shift, w0, w2, a0, a2,
           k_k, k_a, r_k, lnx_g, lnx_b, v0, v1, v2, gln_g, gln_b, w_s, b_s, w_out, final_g):
    depth, d, _ = w_in.shape
    rw = w0.shape[1]
    gw = gln_g.shape[1]
    dl, il = w2.shape[1], a2.shape[1]
    assert rw % PAIR == 0 and gw == rw and w_s.shape[2] == GMLP_CHUNK
    assert gw // GMLP_GROUPS == LANES and dl + il <= LO_W
    sc = 3 * rw + dl + il
    in_cols = w_in.shape[2]

    row3 = lambda t: t[:, None, :]
    prm = dict(
        depth=depth, rw=rw, gw=gw, lora_w=dl + il,
        norm_g=row3(norm_g),
        w_in=jnp.swapaxes(w_in, 1, 2).astype(BF16).reshape(depth * in_cols, d),
        mu_rkv=row3(mu_shift[:, :3 * rw]),
        mu_lo=row3(_pad_last(mu_shift[:, 3 * rw:], LO_W)),
        w0=row3(w0), w2=_pad_rows(w2, LO_W).astype(BF16),
        a0=row3(a0),
        a2=jnp.pad(a2, ((0, 0), (dl, LO_W - dl - il), (0, 0))).astype(BF16),
        k_k=row3(k_k), k_a=row3(k_a), r_k=row3(r_k.reshape(depth, rw)),
        lnx_g=row3(lnx_g), lnx_b=row3(lnx_b),
        v0=row3(v0), v1=_pad_last(v1, LORA_PAD).astype(BF16), v2=_pad_rows(v2, LORA_PAD).astype(BF16),
        gln_g=row3(gln_g), gln_b=row3(gln_b),
        w_s=w_s,
        bias_seq=jnp.repeat(jnp.swapaxes(b_s, 1, 2), gw // GMLP_GROUPS, axis=2),
        w_s0=row3(jnp.repeat(w_s[:, :, 0, 0], gw // GMLP_GROUPS, axis=1)),
        bias0=row3(jnp.repeat(b_s[:, :, 0], gw // GMLP_GROUPS, axis=1)),
        w_out=w_out.astype(BF16),
        final_g=final_g[None, :],
    )

    nb = x_prompt.shape[0]
    n_heads = rw // HEAD
    s0_p = jnp.zeros((depth, nb, n_heads, HEAD, HEAD), F32)
    sh0_p = jnp.zeros((depth, nb, sc), F32)
    y_p, s_p, sh_p, _ = _trunk(x_prompt, s0_p, sh0_p, prm, seq=True)
    y_s, s_s, sh_s, vn_s = _trunk(x_sample, state_wkv, state_shift, prm, seq=False)
    return (y_p, y_s, s_p, jnp.stack(sh_p), s_s, jnp.stack(sh_s), jnp.stack(vn_s))
```

```python
import functools

import jax
import jax.numpy as jnp
from jax import lax
from jax.experimental import pallas as pl
from jax.experimental.pallas import tpu as pltpu

F32 = jnp.float32
BF16 = jnp.bfloat16

HEAD = 64
LANES = 128
PAIR = 2 * HEAD
WKV_CHUNK = HEAD
SEQ_ROWS = 2 * WKV_CHUNK
GMLP_CHUNK = 128
GMLP_GROUPS = 8
LORA_PAD = 128
LO_W = 256
RMS_EPS = 1e-6
LN_EPS = 1e-5
LNX_EPS = 64e-5
VMEM_LIMIT = 56 * 1024 * 1024

_NT = (((1,), (1,)), ((), ()))
_TN = (((0,), (0,)), ((), ()))


def _dot(a, b):
    return jnp.dot(a, b, preferred_element_type=F32)


def _dot_nt(a, b):
    return lax.dot_general(a, b, _NT, preferred_element_type=F32)


def _dot_tn(a, b):
    return lax.dot_general(a, b, _TN, preferred_element_type=F32)


def _cparams(sem):
    return pltpu.CompilerParams(dimension_semantics=sem, vmem_limit_bytes=VMEM_LIMIT)


def _silu(z):
    return z * jax.nn.sigmoid(z)


def _head_ones():
    r = lax.broadcasted_iota(jnp.int32, (LANES, LANES), 0) // HEAD
    c = lax.broadcasted_iota(jnp.int32, (LANES, LANES), 1) // HEAD
    return jnp.where(r == c, 1.0, 0.0).astype(BF16)


def _split_hi_lo(x):
    hi = x.astype(BF16)
    return hi, (x - hi.astype(F32)).astype(BF16)


def _dot_hi_lo(hi_lo, mat, mat_left):
    hi, lo = hi_lo
    return (_dot(mat, hi) + _dot(mat, lo)) if mat_left else (_dot(hi, mat) + _dot(lo, mat))


def _head_sum_hi_lo(hi_lo, ones_bd):
    hi, lo = hi_lo
    outs = [_dot_hi_lo((hi[:, j * LANES:(j + 1) * LANES], lo[:, j * LANES:(j + 1) * LANES]),
                       ones_bd, False) for j in range(hi.shape[1] // LANES)]
    return jnp.concatenate(outs, axis=1)


def _head_sum(x, ones_bd):
    return _head_sum_hi_lo(_split_hi_lo(x), ones_bd)


def _chunk_tri(rows):
    ti = lax.broadcasted_iota(jnp.int32, (rows, rows), 0)
    si = lax.broadcasted_iota(jnp.int32, (rows, rows), 1)
    same_chunk = si // WKV_CHUNK == ti // WKV_CHUNK
    return jnp.where((si <= ti) & same_chunk, 1.0, 0.0).astype(BF16)


def _decay_and_iclr(wraw, araw):
    w = -jax.nn.softplus(-wraw) - 0.5
    return -jnp.exp(w), jax.nn.sigmoid(araw)


_NORM_ROWS = 256


def _in_proj_kernel(h_ref, g_ref, w_ref, wlo_ref, o_ref, lo_ref, xn_ref):
    @pl.when(pl.program_id(1) == 0)
    def _():
        g = g_ref[...]
        chunk = min(_NORM_ROWS, h_ref.shape[0])

        def norm_rows(c, carry):
            rows = pl.ds(pl.multiple_of(c * chunk, chunk), chunk)
            x = h_ref[rows, :]
            ms = jnp.mean(x * x, axis=-1, keepdims=True)
            xn_ref[rows, :] = (x * lax.rsqrt(ms + RMS_EPS) * g).astype(BF16)
            return carry

        lax.fori_loop(0, h_ref.shape[0] // chunk, norm_rows, 0)
        lo_ref[...] = _dot_nt(xn_ref[...], wlo_ref[...])

    o_ref[...] = _dot_nt(xn_ref[...], w_ref[...])


_ROW_ALIGN = 16


def _in_proj(h, g, wt, layer, tm, tn, rw, gw, lora_w):
    m, d = h.shape
    in_cols = 3 * rw + lora_w + rw + 3 * gw
    n_rkv = 3 * rw // tn
    n_main = n_rkv + (rw + 3 * gw) // tn
    assert (3 * rw) % tn == 0 and (rw + 3 * gw) % tn == 0 and lora_w <= LO_W
    assert in_cols % _ROW_ALIGN == 0 and tn % _ROW_ALIGN == 0 and lora_w % _ROW_ALIGN == 0
    assert wt.shape == (in_cols, d)
    step = tn // _ROW_ALIGN
    z0 = (3 * rw + lora_w) // _ROW_ALIGN
    lo0 = 3 * rw // _ROW_ALIGN

    def w_index(i, j):
        start = jnp.where(j < n_rkv, step * j, z0 + step * (j - n_rkv))
        return (start * _ROW_ALIGN, 0)

    return pl.pallas_call(
        _in_proj_kernel,
        grid=(m // tm, n_main),
        in_specs=[
            pl.BlockSpec((tm, d), lambda i, j: (i, 0)),
            _layer_spec(g, layer),
            pl.BlockSpec((pl.Element(tn), pl.Element(d)), w_index),
            pl.BlockSpec((pl.Element(LO_W), pl.Element(d)),
                         lambda i, j: (lo0 * _ROW_ALIGN, 0)),
        ],
        out_specs=[pl.BlockSpec((tm, tn), lambda i, j: (i, j)),
                   pl.BlockSpec((tm, LO_W), lambda i, j: (i, 0))],
        out_shape=[jax.ShapeDtypeStruct((m, n_main * tn), F32),
                   jax.ShapeDtypeStruct((m, LO_W), F32)],
        scratch_shapes=[pltpu.VMEM((tm, d), BF16)],
        compiler_params=_cparams(("parallel", "arbitrary")),
        name="in_proj",
    )(h, g, wt, wt)


def _rwkv_tokens(k, v, tanh_lo, lo, w0, w2, a0, a2, k_k, k_a, vres):
    lw, a = _decay_and_iclr(w0 + _dot(tanh_lo, w2[...]), a0 + _dot(lo, a2[...]))
    if vres is not None:
        v_first, v0, vv_mid, v2 = vres
        v = v + (v_first - v) * jax.nn.sigmoid(v0 + _dot(vv_mid, v2[...]))
    kk = k * k_k
    kk = kk / jnp.maximum(jnp.sqrt(_head_sum(kk * kk, _head_ones())), 1e-12)
    return lw, k * (1.0 + (a - 1.0) * k_a), v, -kk, kk * a


def _rwkv_gate(y, bonus, z, g, b):
    ones_bd = _head_ones()
    mu = _head_sum(y, ones_bd) * (1.0 / HEAD)
    d = y - mu
    var = _head_sum(d * d, ones_bd) * (1.0 / HEAD)
    return (d * lax.rsqrt(var + LNX_EPS) * g + b + bonus) * _silu(z)


def _bonus(r, k, v, rk):
    return _head_sum(r * k * rk, _head_ones()) * v


def _rwkv_prep_kernel(*refs, has_vres):
    (r_ref, k_ref, v_ref, lo_ref, pr_ref, pk_ref, pv_ref, plo_ref,
     mur_ref, muk_ref, muv_ref, mulo_ref, w0_ref, w2_ref, a0_ref, a2_ref,
     kk_ref, ka_ref) = refs[:18]
    pos = 18
    if has_vres:
        vf_ref, v0_ref, v1_ref, v2_ref = refs[pos:pos + 4]
        pos += 4
    ro_ref, lwo_ref, ko_ref, vo_ref, ao_ref, bo_ref = refs[pos:pos + 6]

    def mix(x_ref, p_ref, mu_ref):
        x = x_ref[...]
        return x + (p_ref[...] - x) * mu_ref[...]

    lo = mix(lo_ref, plo_ref, mulo_ref)
    v = mix(v_ref, pv_ref, muv_ref)
    vres = None
    if has_vres:
        vres = (vf_ref[...], v0_ref[...], _dot(v.astype(BF16), v1_ref[...]).astype(BF16), v2_ref)
    lw, k, v, a, b = _rwkv_tokens(
        mix(k_ref, pk_ref, muk_ref), v, jnp.tanh(lo).astype(BF16), lo.astype(BF16),
        w0_ref[...], w2_ref, a0_ref[...], a2_ref, kk_ref[...], ka_ref[...], vres)
    ro_ref[...] = mix(r_ref, pr_ref, mur_ref)
    lwo_ref[...] = lw
    ko_ref[...] = k
    vo_ref[...] = v
    ao_ref[...] = a
    bo_ref[...] = b


def _layer_spec(arr, layer):
    zeros = (0,) * (arr.ndim - 1)
    return pl.BlockSpec((None,) + arr.shape[1:], lambda *g: (layer,) + zeros)


def _token_param_specs(tok, rw, layer):
    return ([pl.BlockSpec((None, 1, rw), lambda *g, c=c: (layer, 0, c)) for c in range(3)]
            + [_layer_spec(t, layer) for t in tok[3:]])


def _vres_specs(vres, rows, rw, row_index, layer):
    return ([pl.BlockSpec((rows, rw), row_index)]
            + [_layer_spec(t, layer - 1) for t in vres[1:]])


def _rwkv_prep(p, p_lo, prev_rkv, prev_lo, tok, vres, layer, *, tm, rw):
    m = p.shape[0]
    row = lambda i: (i, 0)
    col = lambda c: (lambda i: (i, c))
    in_specs = [pl.BlockSpec((tm, rw), col(0)), pl.BlockSpec((tm, rw), col(1)),
                pl.BlockSpec((tm, rw), col(2)), pl.BlockSpec((tm, LO_W), row)]
    in_specs += [pl.BlockSpec((None, tm, rw), lambda i, c=c: (layer, i, c)) for c in range(3)]
    in_specs.append(pl.BlockSpec((None, tm, LO_W), lambda i: (layer, i, 0)))
    args = [p, p, p, p_lo, prev_rkv, prev_rkv, prev_rkv, prev_lo]
    in_specs += _token_param_specs(tok, rw, layer)
    args += list(tok)
    if vres is not None:
        in_specs += _vres_specs(vres, tm, rw, row, layer)
        args += list(vres)
    out = jax.ShapeDtypeStruct((m, rw), F32)
    return pl.pallas_call(
        functools.partial(_rwkv_prep_kernel, has_vres=vres is not None),
        grid=(m // tm,),
        in_specs=in_specs,
        out_specs=[pl.BlockSpec((tm, rw), row)] * 6,
        out_shape=[out] * 6,
        compiler_params=_cparams(("parallel",)),
        name="rwkv_prep",
    )(*args)


def _shift_seq(x, mu, init_row, carry_ref, sl, first):
    prev0 = jnp.where(first, init_row, carry_ref[:, sl])
    sh = pltpu.roll(x, 1, 0)
    rows = lax.broadcasted_iota(jnp.int32, x.shape, 0)
    sh = jnp.where(rows == 0, prev0, sh)
    carry_ref[:, sl] = x[x.shape[0] - 1:, :]
    return x + (sh - x) * mu


_EMPTY_FILL_SLOTS = (2, 4, 6, 8)
_CHAIN_OPERANDS = ("at", "rt", "btl", "bth", "ktl", "kth", "vl", "vh", "be", "ke")


def _chain_operands(r, lw, lp, k, v, a, b):
    rows, width = r.shape
    c_len = WKV_CHUNK
    n_sub = rows // c_len
    p_ends = [jnp.exp(lp[(ci + 1) * c_len - 1:(ci + 1) * c_len]) for ci in range(n_sub)]
    pinv = jnp.exp(-lp)
    pend = pinv * jnp.concatenate(
        [jnp.broadcast_to(e, (c_len, width)) for e in p_ends], axis=0)
    first_head = lax.broadcasted_iota(jnp.int32, (rows, width), 1) < HEAD
    halves = lambda z: (jnp.where(first_head, z, 0.0), jnp.where(first_head, 0.0, z))
    ops = ((a * jnp.exp(lp - lw), r * jnp.exp(lp)) + halves(b * pinv) + halves(k * pinv)
           + halves(v) + (b * pend, k * pend))
    return [o.astype(BF16) for o in ops], jnp.concatenate(p_ends, axis=0)


def _wkv_chain(ops, plast_ref, rd, s_ref, fill):
    _, rows, width = ops["at"].shape
    c_len = WKV_CHUNK
    n_sub = rows // c_len
    n_pairs = width // PAIR

    lane = lax.broadcasted_iota(jnp.int32, (c_len, PAIR), 1)
    rowi = lax.broadcasted_iota(jnp.int32, (c_len, PAIR), 0)
    lo_half = lane < HEAD
    scol = lane % HEAD
    strict = scol < rowi
    incl = scol <= rowi
    r2 = lax.broadcasted_iota(jnp.int32, (PAIR, PAIR), 0)
    c2 = lax.broadcasted_iota(jnp.int32, (PAIR, PAIR), 1)
    same_head = (r2 // HEAD) == (c2 // HEAD)
    eye_pair = jnp.where(scol == rowi, 1.0, 0.0).astype(F32)

    def bd(z):
        return jnp.concatenate([jnp.where(lo_half, z, 0.0), jnp.where(lo_half, 0.0, z)], axis=0)

    chains = [(ci, pi) for ci in range(n_sub) for pi in range(n_pairs)]
    rs = {c: slice(c[0] * c_len, (c[0] + 1) * c_len) for c in chains}
    ls = {c: slice(c[1] * PAIR, (c[1] + 1) * PAIR) for c in chains}
    ld = lambda name, c: ops[name][rd, rs[c], ls[c]]
    x, g = {}, {}
    for c in chains:
        x[c] = jnp.concatenate([ld("at", c), ld("rt", c)], axis=0)
        rhs = jnp.concatenate([ld("btl", c), ld("bth", c), ld("ktl", c), ld("kth", c)], axis=0)
        g[c] = _dot_nt(x[c], rhs)
    fill()
    q_b = {c: jnp.where(incl, g[c][c_len:, :PAIR], 0.0).astype(BF16) for c in chains}
    mv = {}
    for c in chains:
        gk = g[c][:, PAIR:]
        gkm = jnp.concatenate([jnp.where(strict, gk[:c_len], 0.0),
                               jnp.where(incl, gk[c_len:], 0.0)], axis=0)
        mv[c] = _dot(gkm.astype(BF16), jnp.concatenate([ld("vl", c), ld("vh", c)], axis=0))
    fill()

    pw = {c: jnp.where(strict, g[c][:c_len, :PAIR], 0.0) for c in chains}
    tinv = {c: eye_pair + pw[c] for c in chains}
    pw = {c: _dot(pw[c].astype(BF16), bd(pw[c]).astype(BF16)) for c in chains}
    fill()
    n_sq = (c_len - 1).bit_length() - 1
    for j in range(n_sq):
        for c in chains:
            pb = pw[c].astype(BF16)
            if j == n_sq - 1:
                tinv[c] = tinv[c] + _dot(pb, bd(tinv[c]).astype(BF16))
            else:
                res = _dot(pb, jnp.concatenate([bd(tinv[c]), bd(pw[c])], axis=1).astype(BF16))
                tinv[c] = tinv[c] + res[:, :PAIR]
                pw[c] = res[:, PAIR:]
        fill()

    s_cur = [s_ref[pi] for pi in range(n_pairs)]
    y_rows = []
    for ci in range(n_sub):
        cs = [(ci, pi) for pi in range(n_pairs)]
        xs = {c: _dot_nt(x[c], s_cur[c[1]].astype(BF16)) for c in cs}
        fill()
        u = {c: _dot(tinv[c].astype(BF16), bd(xs[c][:c_len] + mv[c][:c_len]).astype(BF16))
             for c in cs}
        fill()
        y_cols = []
        for c in cs:
            y_cols.append(xs[c][c_len:] + mv[c][c_len:] + _dot(q_b[c], bd(u[c]).astype(BF16)))
            uv = jnp.concatenate([u[c].astype(BF16), ld("vl", c) + ld("vh", c)], axis=0)
            bk = jnp.concatenate([ld("be", c), ld("ke", c)], axis=0)
            p_last = plast_ref[rd, ci:ci + 1, ls[c]]
            s_cur[c[1]] = s_cur[c[1]] * p_last + jnp.where(same_head, _dot_tn(uv, bk), 0.0)
        y_rows.append(jnp.concatenate(y_cols, axis=1))
        fill()
    for pi in range(n_pairs):
        s_ref[pi] = s_cur[pi]
    return jnp.concatenate(y_rows, axis=0)


def _wkv_seq_kernel(*refs, has_vres, emit_v, blocks_per_seq, cast_steps):
    (r_ref, k_ref, v_ref, lo_ref, z_ref, pr_ref, pk_ref, pv_ref, plo_ref,
     mur_ref, muk_ref, muv_ref, mulo_ref, w0_ref, w2_ref, a0_ref, a2_ref,
     kk_ref, ka_ref, rk_ref, lg_ref, lb_ref, s0_ref) = refs[:23]
    pos = 23
    if has_vres:
        vf_ref, v0_ref, v1_ref, v2_ref = refs[pos:pos + 4]
        pos += 4
    n_cast = len(cast_steps)
    cast_in = refs[pos:pos + n_cast]
    pos += n_cast
    o_ref = refs[pos]
    pos += 1
    if emit_v:
        vo_ref = refs[pos]
        pos += 1
    s_ref = refs[pos]
    cast_out = refs[pos + 1:pos + 1 + n_cast]
    pos += 1 + n_cast
    cr_ref, ck_ref, cv_ref, clo_ref = refs[pos:pos + 4]
    pos += 4
    ops = dict(zip(_CHAIN_OPERANDS, refs[pos:pos + len(_CHAIN_OPERANDS)]))
    vs_ref, bonus_ref, plast_ref = refs[pos + len(_CHAIN_OPERANDS):]

    j = pl.program_id(0)
    first = (j % blocks_per_seq) == 0
    wr = j % 2
    rd = 1 - wr

    @pl.when(j == 0)
    def _():
        for ref in list(ops.values()) + [vs_ref, bonus_ref]:
            ref[1] = jnp.zeros(ref.shape[1:], ref.dtype)
        plast_ref[1] = jnp.ones(plast_ref.shape[1:], F32)

    @pl.when((jnp.maximum(j - 1, 0) % blocks_per_seq) == 0)
    def _():
        s_ref[...] = s0_ref[...]

    rows, width = vs_ref.shape[1:]
    n_sub = rows // WKV_CHUNK
    n_lane_blocks = width // LANES
    ones_bd = _head_ones()
    tri = _chunk_tri(rows)
    shared, mid = {}, {}

    def shared_vector_part():
        lo = _shift_seq(lo_ref[...], mulo_ref[...], plo_ref[...], clo_ref, slice(None), first)
        shared["tanh_lo"] = jnp.tanh(lo).astype(BF16)
        shared["lo"] = lo.astype(BF16)
        shared["v"] = _shift_seq(v_ref[...], muv_ref[...], pv_ref[...], cv_ref, slice(None), first)

    def shared_matmul_part():
        shared["w_lora"] = _dot(shared["tanh_lo"], w2_ref[...])
        shared["a_lora"] = _dot(shared["lo"], a2_ref[...])
        if has_vres:
            shared["vv_mid"] = _dot(shared["v"].astype(BF16), v1_ref[...])

    def shared_matmul_part2():
        if has_vres:
            shared["vv"] = _dot(shared["vv_mid"].astype(BF16), v2_ref[...])

    def block_vector_part(cb):
        sl = slice(cb * LANES, (cb + 1) * LANES)
        r = _shift_seq(r_ref[:, sl], mur_ref[:, sl], pr_ref[:, sl], cr_ref, sl, first)
        k = _shift_seq(k_ref[:, sl], muk_ref[:, sl], pk_ref[:, sl], ck_ref, sl, first)
        v = shared["v"][:, sl]
        if has_vres:
            v = v + (vf_ref[:, sl] - v) * jax.nn.sigmoid(v0_ref[:, sl] + shared["vv"][:, sl])
        lw, a = _decay_and_iclr(w0_ref[:, sl] + shared["w_lora"][:, sl],
                                a0_ref[:, sl] + shared["a_lora"][:, sl])
        kk = k * kk_ref[:, sl]
        k = k * (1.0 + (a - 1.0) * ka_ref[:, sl])
        mid[cb] = dict(r=r, k=k, v=v, a=a, lw=lw, kk=kk, kk2=_split_hi_lo(kk * kk),
                       rk=_split_hi_lo(r * k * rk_ref[:, sl]), lws=_split_hi_lo(lw))

    def block_matmul_part(cb):
        sl = slice(cb * LANES, (cb + 1) * LANES)
        m = mid.pop(cb)
        kk = m["kk"] / jnp.maximum(jnp.sqrt(_dot_hi_lo(m["kk2"], ones_bd, False)), 1e-12)
        bonus = _dot_hi_lo(m["rk"], ones_bd, False) * m["v"]
        lp = _dot_hi_lo(m["lws"], tri, True)
        vals, p_last = _chain_operands(m["r"], m["lw"], lp, m["k"], m["v"], -kk, kk * m["a"])
        for name, val in zip(_CHAIN_OPERANDS, vals):
            ops[name][wr, :, sl] = val
        vs_ref[wr, :, sl] = m["v"]
        bonus_ref[wr, :, sl] = bonus
        plast_ref[wr, :n_sub, sl] = p_last

    def piece(t):
        def run():
            if 0 < t <= n_lane_blocks:
                block_matmul_part(t - 1)
            if t < n_lane_blocks:
                block_vector_part(t)
        return run

    shared_vector_part()
    shared_matmul_part()
    pending = [shared_matmul_part2] + [piece(t) for t in range(n_lane_blocks + 1)]

    slot = [0]

    def fill():
        slot[0] += 1
        if pending and slot[0] not in _EMPTY_FILL_SLOTS:
            pending.pop(0)()

    y = _wkv_chain(ops, plast_ref, rd, s_ref, fill)
    while pending:
        fill()
    o_ref[...] = _rwkv_gate(y, bonus_ref[rd], z_ref[...], lg_ref[...], lb_ref[...]
                            ).astype(o_ref.dtype)
    if emit_v:
        vo_ref[...] = vs_ref[rd]
    for src_ref, dst_ref, steps in zip(cast_in, cast_out, cast_steps):
        @pl.when(j < steps)
        def _(src_ref=src_ref, dst_ref=dst_ref):
            dst_ref[...] = src_ref[...].astype(dst_ref.dtype)


def _wkv_seq(p, p_lo, prev_rkv, prev_lo, tok, gate, s0_bd, vres, layer, casts, *, n_seq,
             seq_len, rw, z_blk, emit_v):
    m = p.shape[0]
    n_blocks = seq_len // SEQ_ROWS
    total = n_seq * n_blocks
    n_pairs = rw // PAIR
    prep_blk = lambda c: jnp.minimum(c, total - 1)
    run_blk = lambda c: jnp.maximum(c - 1, 0)
    row = lambda c: (prep_blk(c), 0)
    col = lambda j: (lambda c: (prep_blk(c), j))
    blk = lambda j: pl.BlockSpec((SEQ_ROWS, rw), col(j))
    in_specs = [blk(0), blk(1), blk(2), pl.BlockSpec((SEQ_ROWS, LO_W), row),
                pl.BlockSpec((SEQ_ROWS, rw), lambda c: (run_blk(c), z_blk))]
    args = [p, p, p, p_lo, p]
    for j in range(3):
        in_specs.append(pl.BlockSpec((None, None, 1, rw),
                                     lambda c, j=j: (layer, prep_blk(c) // n_blocks, 0, j)))
    in_specs.append(pl.BlockSpec((None, None, 1, LO_W),
                                 lambda c: (layer, prep_blk(c) // n_blocks, 0, 0)))
    args += [prev_rkv, prev_rkv, prev_rkv, prev_lo]
    in_specs += _token_param_specs(tok, rw, layer)
    args += list(tok)
    in_specs += [_layer_spec(t, layer) for t in gate]
    args += list(gate)
    in_specs.append(pl.BlockSpec((None, None, n_pairs, PAIR, PAIR),
                                 lambda c: (layer, run_blk(c) // n_blocks, 0, 0, 0)))
    args.append(s0_bd)
    if vres is not None:
        in_specs += _vres_specs(vres, SEQ_ROWS, rw, row, layer)
        args += list(vres)
    cast_steps, cast_specs, cast_shapes = [], [], []
    for w_all, w_layer in casts:
        n_rows, n_cols = w_all.shape[1:]
        rows = next(r for r in range(_ROW_ALIGN, n_rows + 1, _ROW_ALIGN)
                    if n_rows % r == 0 and n_rows // r <= total + 1)
        steps = n_rows // rows
        in_specs.append(pl.BlockSpec((None, rows, n_cols), lambda c, w_layer=w_layer, steps=steps:
                                     (w_layer, jnp.minimum(c, steps - 1), 0)))
        args.append(w_all)
        cast_specs.append(pl.BlockSpec((rows, n_cols), lambda c, steps=steps:
                                       (jnp.minimum(c, steps - 1), 0)))
        cast_shapes.append(jax.ShapeDtypeStruct((n_rows, n_cols), BF16))
        cast_steps.append(steps)
    run_rows = pl.BlockSpec((SEQ_ROWS, rw), lambda c: (run_blk(c), 0))
    out_specs = [run_rows]
    out_shape = [jax.ShapeDtypeStruct((m, rw), BF16)]
    if emit_v:
        out_specs.append(run_rows)
        out_shape.append(jax.ShapeDtypeStruct((m, rw), F32))
    out_specs.append(pl.BlockSpec((None, n_pairs, PAIR, PAIR),
                                  lambda c: (run_blk(c) // n_blocks, 0, 0, 0)))
    out_shape.append(jax.ShapeDtypeStruct((n_seq, n_pairs, PAIR, PAIR), F32))
    out_specs += cast_specs
    out_shape += cast_shapes
    outs = pl.pallas_call(
        functools.partial(_wkv_seq_kernel, has_vres=vres is not None, emit_v=emit_v,
                          blocks_per_seq=n_blocks, cast_steps=tuple(cast_steps)),
        grid=(total + 1,),
        in_specs=in_specs,
        out_specs=out_specs,
        out_shape=out_shape,
        scratch_shapes=([pltpu.VMEM((1, rw), F32)] * 3 + [pltpu.VMEM((1, LO_W), F32)]
                        + [pltpu.VMEM((2, SEQ_ROWS, rw), BF16)] * len(_CHAIN_OPERANDS)
                        + [pltpu.VMEM((2, SEQ_ROWS, rw), F32)] * 2
                        + [pltpu.VMEM((2, 8, rw), F32)]),
        compiler_params=_cparams(("arbitrary",)),
        name="wkv_seq",
    )(*args)
    outs = list(outs)
    o_r = outs.pop(0)
    v = outs.pop(0) if emit_v else None
    return o_r, v, outs[0], outs[1:]


def _wkv_step_kernel(r_ref, lw_ref, k_ref, v_ref, a_ref, b_ref, s_ref, _, y_ref, so_ref,
                     vt_ref, yt_ref):
    r_t, k_t, a_t, b_t = r_ref[...].T, k_ref[...].T, a_ref[...].T, b_ref[...].T
    w_t = jnp.exp(lw_ref[...]).T
    vt_ref[...] = v_ref[...].T
    for hh in range(PAIR // HEAD):
        sl = slice(hh * HEAD, (hh + 1) * HEAD)
        r, k, a, b, w = r_t[sl], k_t[sl], a_t[sl], b_t[sl], w_t[sl]

        def body(vi, carry, hh=hh, r=r, k=k, a=a, b=b, w=w):
            s = s_ref[hh, vi]
            sa = jnp.sum(s * a, axis=0, keepdims=True)
            s_new = s * w + sa * b + vt_ref[pl.ds(hh * HEAD + vi, 1), :] * k
            so_ref[hh, vi] = s_new
            yt_ref[pl.ds(hh * HEAD + vi, 1), :] = jnp.sum(s_new * r, axis=0, keepdims=True)
            return carry

        lax.fori_loop(0, HEAD, body, 0, unroll=8)
    y_ref[...] = yt_ref[...].T


def _wkv_step(r, lw, k, v, a, b, s_all, s_out_all, layer):
    m, rw = r.shape
    hp = PAIR // HEAD
    assert m % LANES == 0
    blk = pl.BlockSpec((LANES, PAIR), lambda pi, j: (j, pi))
    sblk = pl.BlockSpec((None, hp, HEAD, HEAD, LANES), lambda pi, j: (layer, pi, 0, 0, j))
    return pl.pallas_call(
        _wkv_step_kernel,
        grid=(rw // PAIR, m // LANES),
        in_specs=[blk] * 6 + [sblk, pl.BlockSpec(memory_space=pl.ANY)],
        out_specs=[blk, sblk],
        out_shape=[jax.ShapeDtypeStruct((m, rw), F32),
                   jax.ShapeDtypeStruct(s_out_all.shape, F32)],
        scratch_shapes=[pltpu.VMEM((PAIR, LANES), F32), pltpu.VMEM((PAIR, LANES), F32)],
        input_output_aliases={7: 1},
        compiler_params=_cparams(("parallel", "parallel")),
        name="wkv_step",
    )(r, lw, k, v, a, b, s_all, s_out_all)


def _gmlp_mix(vg, g, b, ws_ref, bias_ref, seq):
    mu = jnp.mean(vg, axis=-1, keepdims=True)
    d = vg - mu
    var = jnp.mean(d * d, axis=-1, keepdims=True)
    vn = d * lax.rsqrt(var + LN_EPS) * g + b
    if not seq:
        return vn * ws_ref[...] + bias_ref[...], vn
    tm, gw = vg.shape[0], vg.shape[1] // GMLP_GROUPS
    ri = lax.broadcasted_iota(jnp.int32, (GMLP_CHUNK, GMLP_CHUNK), 0)
    ci = lax.broadcasted_iota(jnp.int32, (GMLP_CHUNK, GMLP_CHUNK), 1)
    vb = vn.astype(BF16)
    cols = []
    for gi in range(GMLP_GROUPS):
        wg = jnp.where(ci <= ri, ws_ref[gi], 0.0).astype(BF16)
        rows = [_dot(wg, vb[c * GMLP_CHUNK:(c + 1) * GMLP_CHUNK, gi * gw:(gi + 1) * gw])
                for c in range(tm // GMLP_CHUNK)]
        cols.append(jnp.concatenate(rows, axis=0))
    bias = jnp.concatenate([bias_ref[...]] * (tm // GMLP_CHUNK), axis=0)
    return jnp.concatenate(cols, axis=1) + bias, vn


def _mix_out_kernel(*refs, seq, final):
    if seq:
        or_ref = refs[0]
        pos = 1
        o_r = or_ref[...]
    else:
        y_ref, r_ref, k_ref, v_ref, zr_ref, lg_ref, lb_ref, rk_ref = refs[:8]
        pos = 8
        v = v_ref[...]
        o_r = _rwkv_gate(y_ref[...], _bonus(r_ref[...], k_ref[...], v, rk_ref[...]),
                         zr_ref[...], lg_ref[...], lb_ref[...]).astype(BF16)
    u_ref, vg_ref, zg_ref, h_ref, gg_ref, gb_ref, ws_ref, bias_ref, w_ref = refs[pos:pos + 9]
    rest = refs[pos + 9:]
    rw = o_r.shape[1]
    mixed, vn = _gmlp_mix(vg_ref[...], gg_ref[...], gb_ref[...], ws_ref, bias_ref, seq)
    o_g = u_ref[...] * mixed * _silu(zg_ref[...])
    h = h_ref[...] + _dot(o_r, w_ref[:rw, :]) + _dot(o_g.astype(BF16), w_ref[rw:, :])
    if final:
        ms = jnp.mean(h * h, axis=-1, keepdims=True)
        h = h * lax.rsqrt(ms + RMS_EPS) * rest[0][...]
    outs = rest[1:] if final else rest
    outs[0][...] = h
    if not seq:
        outs[1][...] = vn


def _mix_out(branch, p, h, gln_g, gln_b, ws, bias, w, layer, final_g, *, seq, tm, u_blk):
    m, d = h.shape
    gw = gln_g.shape[-1]
    rw = branch[0].shape[1]
    row = pl.BlockSpec((tm, rw), lambda i: (i, 0))
    col = lambda c: pl.BlockSpec((tm, gw), lambda i, c=c: (i, c))
    hrow = pl.BlockSpec((tm, d), lambda i: (i, 0))
    lspec = lambda t: _layer_spec(t, layer)
    if seq:
        in_specs = [row]
        args = list(branch)
    else:
        y, r, k, v, lnx_g, lnx_b, r_k = branch
        in_specs = [row, row, row, row, col(u_blk - 1), lspec(lnx_g), lspec(lnx_b), lspec(r_k)]
        args = [y, r, k, v, p, lnx_g, lnx_b, r_k]
    in_specs += [col(u_blk), col(u_blk + 1), col(u_blk + 2), hrow, lspec(gln_g), lspec(gln_b),
                 lspec(ws), lspec(bias),
                 pl.BlockSpec((rw + gw, d), lambda i: (0, 0), pipeline_mode=pl.Buffered(1))]
    args += [p, p, p, h, gln_g, gln_b, ws, bias, w]
    if final_g is not None:
        in_specs.append(pl.BlockSpec((1, d), lambda i: (0, 0)))
        args.append(final_g)
    out_specs = [hrow]
    out_shape = [jax.ShapeDtypeStruct((m, d), F32)]
    if not seq:
        out_specs.append(pl.BlockSpec((tm, gw), lambda i: (i, 0)))
        out_shape.append(jax.ShapeDtypeStruct((m, gw), F32))
    outs = pl.pallas_call(
        functools.partial(_mix_out_kernel, seq=seq, final=final_g is not None),
        grid=(m // tm,),
        in_specs=in_specs,
        out_specs=out_specs,
        out_shape=out_shape,
        compiler_params=_cparams(("parallel",)),
        name="mix_out",
    )(*args)
    return (outs[0], None) if seq else (outs[0], outs[1])


def _pad_last(x, width):
    return jnp.pad(x, [(0, 0)] * (x.ndim - 1) + [(0, width - x.shape[-1])])


def _pad_rows(x, rows):
    return jnp.pad(x, [(0, 0)] * (x.ndim - 2) + [(0, rows - x.shape[-2]), (0, 0)])


def _pick_tile(m, pref, mult):
    t = min(m, pref)
    while m % t or t % mult:
        t -= mult
    return t


def _trunk(x, s_init, shift_init, prm, weights, *, seq):
    n_seq, seq_len, d = x.shape
    m = n_seq * seq_len
    depth = prm["depth"]
    rw, gw = prm["rw"], prm["gw"]
    lora_w = prm["lora_w"]
    n_heads = rw // HEAD
    tn = 1024
    z_blk = 3

    tm_mm = _pick_tile(m, 1024, 8)
    tm_ew = _pick_tile(m, 512, GMLP_CHUNK if seq else 8)
    assert not seq or seq_len % SEQ_ROWS == 0

    h = x.reshape(m, d)
    s_out, shift_out, vn_out = [], [], []
    prev_rkv = shift_init[:, :, :3 * rw]
    prev_lo = _pad_last(shift_init[:, :, 3 * rw:], LO_W)
    if seq:
        prev_rkv, prev_lo = prev_rkv[:, :, None, :], prev_lo[:, :, None, :]
        s0 = s_init.reshape(depth, n_seq, n_heads // 2, 2, HEAD, HEAD)
        z = jnp.zeros_like(s0[:, :, :, 0])
        s0_bd = jnp.concatenate([jnp.concatenate([s0[:, :, :, 0], z], axis=-1),
                                 jnp.concatenate([z, s0[:, :, :, 1]], axis=-1)], axis=-2)
    else:
        s_init = jnp.transpose(s_init, (0, 2, 3, 4, 1))
        s_buf = jnp.zeros_like(s_init)
    tok = (prm["mu_rkv"],) * 3 + tuple(prm[n] for n in ("mu_lo", "w0", "w2", "a0", "a2",
                                                         "k_k", "k_a"))
    gate = (prm["r_k"], prm["lnx_g"], prm["lnx_b"])
    v_first = None
    w_in_bf, w_out_bf = ([weights], []) if seq else weights
    for l in range(depth):
        p, p_lo = _in_proj(h, prm["norm_g"], w_in_bf[l], l, tm_mm, tn, rw, gw, lora_w)
        vres = (v_first, prm["v0"], prm["v1"], prm["v2"]) if l > 0 else None

        if seq:
            casts = [(prm["w_out"], l)] + ([(prm["w_in"], l + 1)] if l + 1 < depth else [])
            o_r, v, s_bd, cast = _wkv_seq(p, p_lo, prev_rkv, prev_lo, tok, gate, s0_bd, vres, l,
                                          casts, n_seq=n_seq, seq_len=seq_len, rw=rw,
                                          z_blk=z_blk, emit_v=l == 0)
            w_out_bf.append(cast[0])
            w_in_bf += cast[1:]
            s_out.append(s_bd)
            branch = (o_r,)
            ws, bias = prm["w_s"], prm["bias_seq"]
        else:
            r, lw, k, v, a, b = _rwkv_prep(p, p_lo, prev_rkv, prev_lo, tok, vres, l,
                                           tm=tm_ew, rw=rw)
            y, s_buf = _wkv_step(r, lw, k, v, a, b, s_init, s_buf, l)
            branch = (y, r, k, v, gate[1], gate[2], gate[0])
            ws, bias = prm["w_s0"], prm["bias0"]
        if l == 0:
            v_first = v

        last_row = lambda t: t.reshape(n_seq, seq_len, -1)[:, -1]
        shift_out.append(jnp.concatenate(
            [last_row(p)[:, :3 * rw], last_row(p_lo)[:, :lora_w]], axis=-1))

        h, vn = _mix_out(branch, p, h, prm["gln_g"], prm["gln_b"], ws, bias,
                         w_out_bf[l], l, prm["final_g"] if l == depth - 1 else None,
                         seq=seq, tm=tm_ew, u_blk=z_blk + 1)
        if not seq:
            vn_out.append(vn.reshape(n_seq, seq_len, gw))
    if seq:
        s_bd = jnp.stack(s_out)
        s_final = jnp.stack([s_bd[..., :HEAD, :HEAD], s_bd[..., HEAD:, HEAD:]], axis=3)
        s_final = s_final.reshape(depth, n_seq, n_heads, HEAD, HEAD)
    else:
        s_final = jnp.transpose(s_buf, (0, 4, 1, 2, 3))
    return h.reshape(n_seq, seq_len, d), s_final, shift_out, vn_out, (w_in_bf, w_out_bf)


def kernel(x_prompt, x_sample, state_wkv, state_shift, norm_g, w_in, mu_shift, w0, w2, a0, a2,
           k_k, k_a, r_k, lnx_g, lnx_b, v0, v1, v2, gln_g, gln_b, w_s, b_s, w_out, final_g):
    depth, d, _ = w_in.shape
    rw = w0.shape[1]
    gw = gln_g.shape[1]
    dl, il = w2.shape[1], a2.shape[1]
    assert rw % PAIR == 0 and gw == rw and w_s.shape[2] == GMLP_CHUNK
    assert gw // GMLP_GROUPS == LANES and dl + il <= LO_W
    sc = 3 * rw + dl + il
    in_cols = w_in.shape[2]

    row3 = lambda t: t[:, None, :]
    prm = dict(
        depth=depth, rw=rw, gw=gw, lora_w=dl + il,
        norm_g=row3(norm_g),
        w_in=jnp.swapaxes(w_in, 1, 2),
        mu_rkv=row3(mu_shift[:, :3 * rw]),
        mu_lo=row3(_pad_last(mu_shift[:, 3 * rw:], LO_W)),
        w0=row3(w0), w2=_pad_rows(w2, LO_W).astype(BF16),
        a0=row3(a0),
        a2=jnp.pad(a2, ((0, 0), (dl, LO_W - dl - il), (0, 0))).astype(BF16),
        k_k=row3(k_k), k_a=row3(k_a), r_k=row3(r_k.reshape(depth, rw)),
        lnx_g=row3(lnx_g), lnx_b=row3(lnx_b),
        v0=row3(v0), v1=_pad_last(v1, LORA_PAD).astype(BF16), v2=_pad_rows(v2, LORA_PAD).astype(BF16),
        gln_g=row3(gln_g), gln_b=row3(gln_b),
        w_s=w_s,
        bias_seq=jnp.repeat(jnp.swapaxes(b_s, 1, 2), gw // GMLP_GROUPS, axis=2),
        w_s0=row3(jnp.repeat(w_s[:, :, 0, 0], gw // GMLP_GROUPS, axis=1)),
        bias0=row3(jnp.repeat(b_s[:, :, 0], gw // GMLP_GROUPS, axis=1)),
        w_out=w_out,
        final_g=final_g[None, :],
    )

    nb = x_prompt.shape[0]
    n_heads = rw // HEAD
    s0_p = jnp.zeros((depth, nb, n_heads, HEAD, HEAD), F32)
    sh0_p = jnp.zeros((depth, nb, sc), F32)
    y_p, s_p, sh_p, _, weights = _trunk(x_prompt, s0_p, sh0_p, prm,
                                        prm["w_in"][0].astype(BF16), seq=True)
    y_s, s_s, sh_s, vn_s, _ = _trunk(x_sample, state_wkv, state_shift, prm, weights, seq=False)
    return (y_p, y_s, s_p, jnp.stack(sh_p), s_s, jnp.stack(sh_s), jnp.stack(vn_s))
```

```python
import functools

import jax
import jax.numpy as jnp
from jax import lax
from jax.experimental import pallas as pl
from jax.experimental.pallas import tpu as pltpu

F32 = jnp.float32
BF16 = jnp.bfloat16

HEAD = 64
LANES = 128
PAIR = 2 * HEAD
WKV_CHUNK = HEAD
SEQ_ROWS = 2 * WKV_CHUNK
GMLP_CHUNK = 128
GMLP_GROUPS = 8
LORA_PAD = 128
LO_W = 256
RMS_EPS = 1e-6
LN_EPS = 1e-5
LNX_EPS = 64e-5
VMEM_LIMIT = 56 * 1024 * 1024

_NT = (((1,), (1,)), ((), ()))
_TN = (((0,), (0,)), ((), ()))


def _dot(a, b):
    return jnp.dot(a, b, preferred_element_type=F32)


def _dot_nt(a, b):
    return lax.dot_general(a, b, _NT, preferred_element_type=F32)


def _dot_tn(a, b):
    return lax.dot_general(a, b, _TN, preferred_element_type=F32)


def _cparams(sem):
    return pltpu.CompilerParams(dimension_semantics=sem, vmem_limit_bytes=VMEM_LIMIT)


def _silu(z):
    return z * jax.nn.sigmoid(z)


def _head_ones():
    r = lax.broadcasted_iota(jnp.int32, (LANES, LANES), 0) // HEAD
    c = lax.broadcasted_iota(jnp.int32, (LANES, LANES), 1) // HEAD
    return jnp.where(r == c, 1.0, 0.0).astype(BF16)


def _split_hi_lo(x):
    hi = x.astype(BF16)
    return hi, (x - hi.astype(F32)).astype(BF16)


def _dot_hi_lo(hi_lo, mat, mat_left):
    hi, lo = hi_lo
    return (_dot(mat, hi) + _dot(mat, lo)) if mat_left else (_dot(hi, mat) + _dot(lo, mat))


def _head_sum_hi_lo(hi_lo, ones_bd):
    hi, lo = hi_lo
    outs = [_dot_hi_lo((hi[:, j * LANES:(j + 1) * LANES], lo[:, j * LANES:(j + 1) * LANES]),
                       ones_bd, False) for j in range(hi.shape[1] // LANES)]
    return jnp.concatenate(outs, axis=1)


def _head_sum(x, ones_bd):
    return _head_sum_hi_lo(_split_hi_lo(x), ones_bd)


def _chunk_tri(rows):
    ti = lax.broadcasted_iota(jnp.int32, (rows, rows), 0)
    si = lax.broadcasted_iota(jnp.int32, (rows, rows), 1)
    same_chunk = si // WKV_CHUNK == ti // WKV_CHUNK
    return jnp.where((si <= ti) & same_chunk, 1.0, 0.0).astype(BF16)


def _decay_and_iclr(wraw, araw):
    w = -jax.nn.softplus(-wraw) - 0.5
    return -jnp.exp(w), jax.nn.sigmoid(araw)


_NORM_ROWS = 256


def _in_proj_kernel(h_ref, g_ref, w_ref, wlo_ref, o_ref, lo_ref, xn_ref):
    @pl.when(pl.program_id(1) == 0)
    def _():
        g = g_ref[...]
        chunk = min(_NORM_ROWS, h_ref.shape[0])

        def norm_rows(c, carry):
            rows = pl.ds(pl.multiple_of(c * chunk, chunk), chunk)
            x = h_ref[rows, :]
            ms = jnp.mean(x * x, axis=-1, keepdims=True)
            xn_ref[rows, :] = (x * lax.rsqrt(ms + RMS_EPS) * g).astype(BF16)
            return carry

        lax.fori_loop(0, h_ref.shape[0] // chunk, norm_rows, 0)
        lo_ref[...] = _dot_nt(xn_ref[...], wlo_ref[...])

    o_ref[...] = _dot_nt(xn_ref[...], w_ref[...])


_ROW_ALIGN = 16


def _in_proj(h, g, wt, layer, tm, tn, rw, gw, lora_w):
    m, d = h.shape
    in_cols = 3 * rw + lora_w + rw + 3 * gw
    n_rkv = 3 * rw // tn
    n_main = n_rkv + (rw + 3 * gw) // tn
    assert (3 * rw) % tn == 0 and (rw + 3 * gw) % tn == 0 and lora_w <= LO_W
    assert in_cols % _ROW_ALIGN == 0 and tn % _ROW_ALIGN == 0 and lora_w % _ROW_ALIGN == 0
    assert wt.shape == (in_cols, d)
    step = tn // _ROW_ALIGN
    z0 = (3 * rw + lora_w) // _ROW_ALIGN
    lo0 = 3 * rw // _ROW_ALIGN

    def w_index(i, j):
        start = jnp.where(j < n_rkv, step * j, z0 + step * (j - n_rkv))
        return (start * _ROW_ALIGN, 0)

    return pl.pallas_call(
        _in_proj_kernel,
        grid=(m // tm, n_main),
        in_specs=[
            pl.BlockSpec((tm, d), lambda i, j: (i, 0)),
            _layer_spec(g, layer),
            pl.BlockSpec((pl.Element(tn), pl.Element(d)), w_index),
            pl.BlockSpec((pl.Element(LO_W), pl.Element(d)),
                         lambda i, j: (lo0 * _ROW_ALIGN, 0)),
        ],
        out_specs=[pl.BlockSpec((tm, tn), lambda i, j: (i, j)),
                   pl.BlockSpec((tm, LO_W), lambda i, j: (i, 0))],
        out_shape=[jax.ShapeDtypeStruct((m, n_main * tn), F32),
                   jax.ShapeDtypeStruct((m, LO_W), F32)],
        scratch_shapes=[pltpu.VMEM((tm, d), BF16)],
        compiler_params=_cparams(("parallel", "arbitrary")),
        name="in_proj",
    )(h, g, wt, wt)


def _rwkv_tokens(k, v, tanh_lo, lo, w0, w2, a0, a2, k_k, k_a, vres):
    lw, a = _decay_and_iclr(w0 + _dot(tanh_lo, w2[...]), a0 + _dot(lo, a2[...]))
    if vres is not None:
        v_first, v0, vv_mid, v2 = vres
        v = v + (v_first - v) * jax.nn.sigmoid(v0 + _dot(vv_mid, v2[...]))
    kk = k * k_k
    kk = kk / jnp.maximum(jnp.sqrt(_head_sum(kk * kk, _head_ones())), 1e-12)
    return lw, k * (1.0 + (a - 1.0) * k_a), v, -kk, kk * a


def _head_sum_bf16(x, ones_bd):
    outs = [_dot(x[:, j * LANES:(j + 1) * LANES].astype(BF16), ones_bd)
            for j in range(x.shape[1] // LANES)]
    return jnp.concatenate(outs, axis=1)


def _rwkv_gate(y, bonus, z, g, b, head_sum=_head_sum):
    ones_bd = _head_ones()
    mu = head_sum(y, ones_bd) * (1.0 / HEAD)
    d = y - mu
    var = head_sum(d * d, ones_bd) * (1.0 / HEAD)
    return (d * lax.rsqrt(var + LNX_EPS) * g + b + bonus) * _silu(z)


def _bonus(r, k, v, rk):
    return _head_sum(r * k * rk, _head_ones()) * v


def _rwkv_prep_kernel(*refs, has_vres):
    (r_ref, k_ref, v_ref, lo_ref, pr_ref, pk_ref, pv_ref, plo_ref,
     mur_ref, muk_ref, muv_ref, mulo_ref, w0_ref, w2_ref, a0_ref, a2_ref,
     kk_ref, ka_ref) = refs[:18]
    pos = 18
    if has_vres:
        vf_ref, v0_ref, v1_ref, v2_ref = refs[pos:pos + 4]
        pos += 4
    ro_ref, lwo_ref, ko_ref, vo_ref, ao_ref, bo_ref = refs[pos:pos + 6]

    def mix(x_ref, p_ref, mu_ref):
        x = x_ref[...]
        return x + (p_ref[...] - x) * mu_ref[...]

    lo = mix(lo_ref, plo_ref, mulo_ref)
    v = mix(v_ref, pv_ref, muv_ref)
    vres = None
    if has_vres:
        vres = (vf_ref[...], v0_ref[...], _dot(v.astype(BF16), v1_ref[...]).astype(BF16), v2_ref)
    lw, k, v, a, b = _rwkv_tokens(
        mix(k_ref, pk_ref, muk_ref), v, jnp.tanh(lo).astype(BF16), lo.astype(BF16),
        w0_ref[...], w2_ref, a0_ref[...], a2_ref, kk_ref[...], ka_ref[...], vres)
    ro_ref[...] = mix(r_ref, pr_ref, mur_ref)
    lwo_ref[...] = lw
    ko_ref[...] = k
    vo_ref[...] = v
    ao_ref[...] = a
    bo_ref[...] = b


def _layer_spec(arr, layer):
    zeros = (0,) * (arr.ndim - 1)
    return pl.BlockSpec((None,) + arr.shape[1:], lambda *g: (layer,) + zeros)


def _token_param_specs(tok, rw, layer):
    return ([pl.BlockSpec((None, 1, rw), lambda *g, c=c: (layer, 0, c)) for c in range(3)]
            + [_layer_spec(t, layer) for t in tok[3:]])


def _vres_specs(vres, rows, rw, row_index, layer):
    return ([pl.BlockSpec((rows, rw), row_index)]
            + [_layer_spec(t, layer - 1) for t in vres[1:]])


def _rwkv_prep(p, p_lo, prev_rkv, prev_lo, tok, vres, layer, *, tm, rw):
    m = p.shape[0]
    row = lambda i: (i, 0)
    col = lambda c: (lambda i: (i, c))
    in_specs = [pl.BlockSpec((tm, rw), col(0)), pl.BlockSpec((tm, rw), col(1)),
                pl.BlockSpec((tm, rw), col(2)), pl.BlockSpec((tm, LO_W), row)]
    in_specs += [pl.BlockSpec((None, tm, rw), lambda i, c=c: (layer, i, c)) for c in range(3)]
    in_specs.append(pl.BlockSpec((None, tm, LO_W), lambda i: (layer, i, 0)))
    args = [p, p, p, p_lo, prev_rkv, prev_rkv, prev_rkv, prev_lo]
    in_specs += _token_param_specs(tok, rw, layer)
    args += list(tok)
    if vres is not None:
        in_specs += _vres_specs(vres, tm, rw, row, layer)
        args += list(vres)
    out = jax.ShapeDtypeStruct((m, rw), F32)
    return pl.pallas_call(
        functools.partial(_rwkv_prep_kernel, has_vres=vres is not None),
        grid=(m // tm,),
        in_specs=in_specs,
        out_specs=[pl.BlockSpec((tm, rw), row)] * 6,
        out_shape=[out] * 6,
        compiler_params=_cparams(("parallel",)),
        name="rwkv_prep",
    )(*args)


def _shift_seq(x, mu, init_row, carry_ref, sl, first):
    prev0 = jnp.where(first, init_row, carry_ref[:, sl])
    sh = pltpu.roll(x, 1, 0)
    rows = lax.broadcasted_iota(jnp.int32, x.shape, 0)
    sh = jnp.where(rows == 0, prev0, sh)
    carry_ref[:, sl] = x[x.shape[0] - 1:, :]
    return x + (sh - x) * mu


_EMPTY_FILL_SLOTS = (2, 4, 6, 8)
_CHAIN_OPERANDS = ("at", "rt", "btl", "bth", "ktl", "kth", "vl", "vh", "be", "ke")


def _chain_operands(r, lw, lp, k, v, a, b):
    rows, width = r.shape
    c_len = WKV_CHUNK
    n_sub = rows // c_len
    p_ends = [jnp.exp(lp[(ci + 1) * c_len - 1:(ci + 1) * c_len]) for ci in range(n_sub)]
    pinv = jnp.exp(-lp)
    pend = pinv * jnp.concatenate(
        [jnp.broadcast_to(e, (c_len, width)) for e in p_ends], axis=0)
    first_head = lax.broadcasted_iota(jnp.int32, (rows, width), 1) < HEAD
    halves = lambda z: (jnp.where(first_head, z, 0.0), jnp.where(first_head, 0.0, z))
    ops = ((a * jnp.exp(lp - lw), r * jnp.exp(lp)) + halves(b * pinv) + halves(k * pinv)
           + halves(v) + (b * pend, k * pend))
    return [o.astype(BF16) for o in ops], jnp.concatenate(p_ends, axis=0)


def _wkv_chain(ops, plast_ref, rd, s_ref, fill):
    _, rows, width = ops["at"].shape
    c_len = WKV_CHUNK
    n_sub = rows // c_len
    n_pairs = width // PAIR

    lane = lax.broadcasted_iota(jnp.int32, (c_len, PAIR), 1)
    rowi = lax.broadcasted_iota(jnp.int32, (c_len, PAIR), 0)
    lo_half = lane < HEAD
    scol = lane % HEAD
    strict = scol < rowi
    incl = scol <= rowi
    r2 = lax.broadcasted_iota(jnp.int32, (PAIR, PAIR), 0)
    c2 = lax.broadcasted_iota(jnp.int32, (PAIR, PAIR), 1)
    same_head = (r2 // HEAD) == (c2 // HEAD)
    eye_pair = jnp.where(scol == rowi, 1.0, 0.0).astype(F32)

    def bd(z):
        return jnp.concatenate([jnp.where(lo_half, z, 0.0), jnp.where(lo_half, 0.0, z)], axis=0)

    chains = [(ci, pi) for ci in range(n_sub) for pi in range(n_pairs)]
    rs = {c: slice(c[0] * c_len, (c[0] + 1) * c_len) for c in chains}
    ls = {c: slice(c[1] * PAIR, (c[1] + 1) * PAIR) for c in chains}
    ld = lambda name, c: ops[name][rd, rs[c], ls[c]]
    x, g = {}, {}
    for c in chains:
        x[c] = jnp.concatenate([ld("at", c), ld("rt", c)], axis=0)
        rhs = jnp.concatenate([ld("btl", c), ld("bth", c), ld("ktl", c), ld("kth", c)], axis=0)
        g[c] = _dot_nt(x[c], rhs)
    fill()
    q_b = {c: jnp.where(incl, g[c][c_len:, :PAIR], 0.0).astype(BF16) for c in chains}
    mv = {}
    for c in chains:
        gk = g[c][:, PAIR:]
        gkm = jnp.concatenate([jnp.where(strict, gk[:c_len], 0.0),
                               jnp.where(incl, gk[c_len:], 0.0)], axis=0)
        mv[c] = _dot(gkm.astype(BF16), jnp.concatenate([ld("vl", c), ld("vh", c)], axis=0))
    fill()

    pw = {c: jnp.where(strict, g[c][:c_len, :PAIR], 0.0) for c in chains}
    tinv = {c: eye_pair + pw[c] for c in chains}
    pw = {c: _dot(pw[c].astype(BF16), bd(pw[c]).astype(BF16)) for c in chains}
    fill()
    n_sq = (c_len - 1).bit_length() - 1
    for j in range(n_sq):
        for c in chains:
            pb = pw[c].astype(BF16)
            if j == n_sq - 1:
                tinv[c] = tinv[c] + _dot(pb, bd(tinv[c]).astype(BF16))
            else:
                res = _dot(pb, jnp.concatenate([bd(tinv[c]), bd(pw[c])], axis=1).astype(BF16))
                tinv[c] = tinv[c] + res[:, :PAIR]
                pw[c] = res[:, PAIR:]
        fill()

    s_cur = [s_ref[pi] for pi in range(n_pairs)]
    y_rows = []
    for ci in range(n_sub):
        cs = [(ci, pi) for pi in range(n_pairs)]
        xs = {c: _dot_nt(x[c], s_cur[c[1]].astype(BF16)) for c in cs}
        fill()
        u = {c: _dot(tinv[c].astype(BF16), bd(xs[c][:c_len] + mv[c][:c_len]).astype(BF16))
             for c in cs}
        fill()
        y_cols = []
        for c in cs:
            y_cols.append(xs[c][c_len:] + mv[c][c_len:] + _dot(q_b[c], bd(u[c]).astype(BF16)))
            uv = jnp.concatenate([u[c].astype(BF16), ld("vl", c) + ld("vh", c)], axis=0)
            bk = jnp.concatenate([ld("be", c), ld("ke", c)], axis=0)
            p_last = plast_ref[rd, ci:ci + 1, ls[c]]
            s_cur[c[1]] = s_cur[c[1]] * p_last + jnp.where(same_head, _dot_tn(uv, bk), 0.0)
        y_rows.append(jnp.concatenate(y_cols, axis=1))
        fill()
    for pi in range(n_pairs):
        s_ref[pi] = s_cur[pi]
    return jnp.concatenate(y_rows, axis=0)


def _wkv_seq_kernel(*refs, has_vres, emit_v, blocks_per_seq, cast_steps):
    (r_ref, k_ref, v_ref, lo_ref, z_ref, pr_ref, pk_ref, pv_ref, plo_ref,
     mur_ref, muk_ref, muv_ref, mulo_ref, w0_ref, w2_ref, a0_ref, a2_ref,
     kk_ref, ka_ref, rk_ref, lg_ref, lb_ref, s0_ref) = refs[:23]
    pos = 23
    if has_vres:
        vf_ref, v0_ref, v1_ref, v2_ref = refs[pos:pos + 4]
        pos += 4
    n_cast = len(cast_steps)
    cast_in = refs[pos:pos + n_cast]
    pos += n_cast
    o_ref = refs[pos]
    pos += 1
    if emit_v:
        vo_ref = refs[pos]
        pos += 1
    s_ref = refs[pos]
    cast_out = refs[pos + 1:pos + 1 + n_cast]
    pos += 1 + n_cast
    cr_ref, ck_ref, cv_ref, clo_ref = refs[pos:pos + 4]
    pos += 4
    ops = dict(zip(_CHAIN_OPERANDS, refs[pos:pos + len(_CHAIN_OPERANDS)]))
    vs_ref, bonus_ref, plast_ref = refs[pos + len(_CHAIN_OPERANDS):]

    j = pl.program_id(0)
    first = (j % blocks_per_seq) == 0
    wr = j % 2
    rd = 1 - wr

    @pl.when(j == 0)
    def _():
        for ref in list(ops.values()) + [vs_ref, bonus_ref]:
            ref[1] = jnp.zeros(ref.shape[1:], ref.dtype)
        plast_ref[1] = jnp.ones(plast_ref.shape[1:], F32)

    @pl.when((jnp.maximum(j - 1, 0) % blocks_per_seq) == 0)
    def _():
        s_ref[...] = s0_ref[...]

    rows, width = vs_ref.shape[1:]
    n_sub = rows // WKV_CHUNK
    n_lane_blocks = width // LANES
    ones_bd = _head_ones()
    tri = _chunk_tri(rows)
    shared, mid = {}, {}

    def shared_vector_part():
        lo = _shift_seq(lo_ref[...], mulo_ref[...], plo_ref[...], clo_ref, slice(None), first)
        shared["tanh_lo"] = jnp.tanh(lo).astype(BF16)
        shared["lo"] = lo.astype(BF16)
        shared["v"] = _shift_seq(v_ref[...], muv_ref[...], pv_ref[...], cv_ref, slice(None), first)

    def shared_matmul_part():
        shared["w_lora"] = _dot(shared["tanh_lo"], w2_ref[...])
        shared["a_lora"] = _dot(shared["lo"], a2_ref[...])
        if has_vres:
            shared["vv_mid"] = _dot(shared["v"].astype(BF16), v1_ref[...])

    def shared_matmul_part2():
        if has_vres:
            shared["vv"] = _dot(shared["vv_mid"].astype(BF16), v2_ref[...])

    def block_vector_part(cb):
        sl = slice(cb * LANES, (cb + 1) * LANES)
        r = _shift_seq(r_ref[:, sl], mur_ref[:, sl], pr_ref[:, sl], cr_ref, sl, first)
        k = _shift_seq(k_ref[:, sl], muk_ref[:, sl], pk_ref[:, sl], ck_ref, sl, first)
        v = shared["v"][:, sl]
        if has_vres:
            v = v + (vf_ref[:, sl] - v) * jax.nn.sigmoid(v0_ref[:, sl] + shared["vv"][:, sl])
        lw, a = _decay_and_iclr(w0_ref[:, sl] + shared["w_lora"][:, sl],
                                a0_ref[:, sl] + shared["a_lora"][:, sl])
        kk = k * kk_ref[:, sl]
        k = k * (1.0 + (a - 1.0) * ka_ref[:, sl])
        mid[cb] = dict(r=r, k=k, v=v, a=a, lw=lw, kk=kk, kk2=(kk * kk).astype(BF16),
                       rk=(r * k * rk_ref[:, sl]).astype(BF16), lws=_split_hi_lo(lw))

    def block_matmul_part(cb):
        sl = slice(cb * LANES, (cb + 1) * LANES)
        m = mid.pop(cb)
        kk = m["kk"] / jnp.maximum(jnp.sqrt(_dot(m["kk2"], ones_bd)), 1e-12)
        bonus = _dot(m["rk"], ones_bd) * m["v"]
        lp = _dot_hi_lo(m["lws"], tri, True)
        vals, p_last = _chain_operands(m["r"], m["lw"], lp, m["k"], m["v"], -kk, kk * m["a"])
        for name, val in zip(_CHAIN_OPERANDS, vals):
            ops[name][wr, :, sl] = val
        vs_ref[wr, :, sl] = m["v"]
        bonus_ref[wr, :, sl] = bonus
        plast_ref[wr, :n_sub, sl] = p_last

    def piece(t):
        def run():
            if 0 < t <= n_lane_blocks:
                block_matmul_part(t - 1)
            if t < n_lane_blocks:
                block_vector_part(t)
        return run

    shared_vector_part()
    shared_matmul_part()
    pending = [shared_matmul_part2] + [piece(t) for t in range(n_lane_blocks + 1)]

    slot = [0]

    def fill():
        slot[0] += 1
        if pending and slot[0] not in _EMPTY_FILL_SLOTS:
            pending.pop(0)()

    y = _wkv_chain(ops, plast_ref, rd, s_ref, fill)
    while pending:
        fill()
    o_ref[...] = _rwkv_gate(y, bonus_ref[rd], z_ref[...], lg_ref[...], lb_ref[...],
                            _head_sum_bf16).astype(o_ref.dtype)
    if emit_v:
        vo_ref[...] = vs_ref[rd]
    for src_ref, dst_ref, steps in zip(cast_in, cast_out, cast_steps):
        @pl.when(j < steps)
        def _(src_ref=src_ref, dst_ref=dst_ref):
            dst_ref[...] = src_ref[...].astype(dst_ref.dtype)


def _wkv_seq(p, p_lo, prev_rkv, prev_lo, tok, gate, s0_bd, vres, layer, casts, *, n_seq,
             seq_len, rw, z_blk, emit_v):
    m = p.shape[0]
    n_blocks = seq_len // SEQ_ROWS
    total = n_seq * n_blocks
    n_pairs = rw // PAIR
    prep_blk = lambda c: jnp.minimum(c, total - 1)
    run_blk = lambda c: jnp.maximum(c - 1, 0)
    row = lambda c: (prep_blk(c), 0)
    col = lambda j: (lambda c: (prep_blk(c), j))
    blk = lambda j: pl.BlockSpec((SEQ_ROWS, rw), col(j))
    in_specs = [blk(0), blk(1), blk(2), pl.BlockSpec((SEQ_ROWS, LO_W), row),
                pl.BlockSpec((SEQ_ROWS, rw), lambda c: (run_blk(c), z_blk))]
    args = [p, p, p, p_lo, p]
    for j in range(3):
        in_specs.append(pl.BlockSpec((None, None, 1, rw),
                                     lambda c, j=j: (layer, prep_blk(c) // n_blocks, 0, j)))
    in_specs.append(pl.BlockSpec((None, None, 1, LO_W),
                                 lambda c: (layer, prep_blk(c) // n_blocks, 0, 0)))
    args += [prev_rkv, prev_rkv, prev_rkv, prev_lo]
    in_specs += _token_param_specs(tok, rw, layer)
    args += list(tok)
    in_specs += [_layer_spec(t, layer) for t in gate]
    args += list(gate)
    in_specs.append(pl.BlockSpec((None, None, n_pairs, PAIR, PAIR),
                                 lambda c: (layer, run_blk(c) // n_blocks, 0, 0, 0)))
    args.append(s0_bd)
    if vres is not None:
        in_specs += _vres_specs(vres, SEQ_ROWS, rw, row, layer)
        args += list(vres)
    cast_steps, cast_specs, cast_shapes = [], [], []
    for w_all, w_layer in casts:
        n_rows, n_cols = w_all.shape[1:]
        rows = next(r for r in range(_ROW_ALIGN, n_rows + 1, _ROW_ALIGN)
                    if n_rows % r == 0 and n_rows // r <= total + 1)
        steps = n_rows // rows
        in_specs.append(pl.BlockSpec((None, rows, n_cols), lambda c, w_layer=w_layer, steps=steps:
                                     (w_layer, jnp.minimum(c, steps - 1), 0)))
        args.append(w_all)
        cast_specs.append(pl.BlockSpec((rows, n_cols), lambda c, steps=steps:
                                       (jnp.minimum(c, steps - 1), 0)))
        cast_shapes.append(jax.ShapeDtypeStruct((n_rows, n_cols), BF16))
        cast_steps.append(steps)
    run_rows = pl.BlockSpec((SEQ_ROWS, rw), lambda c: (run_blk(c), 0))
    out_specs = [run_rows]
    out_shape = [jax.ShapeDtypeStruct((m, rw), BF16)]
    if emit_v:
        out_specs.append(run_rows)
        out_shape.append(jax.ShapeDtypeStruct((m, rw), F32))
    out_specs.append(pl.BlockSpec((None, n_pairs, PAIR, PAIR),
                                  lambda c: (run_blk(c) // n_blocks, 0, 0, 0)))
    out_shape.append(jax.ShapeDtypeStruct((n_seq, n_pairs, PAIR, PAIR), F32))
    out_specs += cast_specs
    out_shape += cast_shapes
    outs = pl.pallas_call(
        functools.partial(_wkv_seq_kernel, has_vres=vres is not None, emit_v=emit_v,
                          blocks_per_seq=n_blocks, cast_steps=tuple(cast_steps)),
        grid=(total + 1,),
        in_specs=in_specs,
        out_specs=out_specs,
        out_shape=out_shape,
        scratch_shapes=([pltpu.VMEM((1, rw), F32)] * 3 + [pltpu.VMEM((1, LO_W), F32)]
                        + [pltpu.VMEM((2, SEQ_ROWS, rw), BF16)] * len(_CHAIN_OPERANDS)
                        + [pltpu.VMEM((2, SEQ_ROWS, rw), F32)] * 2
                        + [pltpu.VMEM((2, 8, rw), F32)]),
        compiler_params=_cparams(("arbitrary",)),
        name="wkv_seq",
    )(*args)
    outs = list(outs)
    o_r = outs.pop(0)
    v = outs.pop(0) if emit_v else None
    return o_r, v, outs[0], outs[1:]


def _wkv_step_kernel(r_ref, lw_ref, k_ref, v_ref, a_ref, b_ref, s_ref, _, y_ref, so_ref,
                     vt_ref, yt_ref):
    r_t, k_t, a_t, b_t = r_ref[...].T, k_ref[...].T, a_ref[...].T, b_ref[...].T
    w_t = jnp.exp(lw_ref[...]).T
    vt_ref[...] = v_ref[...].T
    for hh in range(PAIR // HEAD):
        sl = slice(hh * HEAD, (hh + 1) * HEAD)
        r, k, a, b, w = r_t[sl], k_t[sl], a_t[sl], b_t[sl], w_t[sl]

        def body(vi, carry, hh=hh, r=r, k=k, a=a, b=b, w=w):
            s = s_ref[hh, vi]
            sa = jnp.sum(s * a, axis=0, keepdims=True)
            s_new = s * w + sa * b + vt_ref[pl.ds(hh * HEAD + vi, 1), :] * k
            so_ref[hh, vi] = s_new
            yt_ref[pl.ds(hh * HEAD + vi, 1), :] = jnp.sum(s_new * r, axis=0, keepdims=True)
            return carry

        lax.fori_loop(0, HEAD, body, 0, unroll=8)
    y_ref[...] = yt_ref[...].T


def _wkv_step(r, lw, k, v, a, b, s_all, s_out_all, layer):
    m, rw = r.shape
    hp = PAIR // HEAD
    assert m % LANES == 0
    blk = pl.BlockSpec((LANES, PAIR), lambda pi, j: (j, pi))
    sblk = pl.BlockSpec((None, hp, HEAD, HEAD, LANES), lambda pi, j: (layer, pi, 0, 0, j))
    return pl.pallas_call(
        _wkv_step_kernel,
        grid=(rw // PAIR, m // LANES),
        in_specs=[blk] * 6 + [sblk, pl.BlockSpec(memory_space=pl.ANY)],
        out_specs=[blk, sblk],
        out_shape=[jax.ShapeDtypeStruct((m, rw), F32),
                   jax.ShapeDtypeStruct(s_out_all.shape, F32)],
        scratch_shapes=[pltpu.VMEM((PAIR, LANES), F32), pltpu.VMEM((PAIR, LANES), F32)],
        input_output_aliases={7: 1},
        compiler_params=_cparams(("parallel", "parallel")),
        name="wkv_step",
    )(r, lw, k, v, a, b, s_all, s_out_all)


def _gmlp_mix(vg, g, b, ws_ref, bias_ref, seq):
    mu = jnp.mean(vg, axis=-1, keepdims=True)
    d = vg - mu
    var = jnp.mean(d * d, axis=-1, keepdims=True)
    vn = d * lax.rsqrt(var + LN_EPS) * g + b
    if not seq:
        return vn * ws_ref[...] + bias_ref[...], vn
    tm, gw = vg.shape[0], vg.shape[1] // GMLP_GROUPS
    ri = lax.broadcasted_iota(jnp.int32, (GMLP_CHUNK, GMLP_CHUNK), 0)
    ci = lax.broadcasted_iota(jnp.int32, (GMLP_CHUNK, GMLP_CHUNK), 1)
    vb = vn.astype(BF16)
    cols = []
    for gi in range(GMLP_GROUPS):
        wg = jnp.where(ci <= ri, ws_ref[gi], 0.0).astype(BF16)
        rows = [_dot(wg, vb[c * GMLP_CHUNK:(c + 1) * GMLP_CHUNK, gi * gw:(gi + 1) * gw])
                for c in range(tm // GMLP_CHUNK)]
        cols.append(jnp.concatenate(rows, axis=0))
    bias = jnp.concatenate([bias_ref[...]] * (tm // GMLP_CHUNK), axis=0)
    return jnp.concatenate(cols, axis=1) + bias, vn


def _mix_out_kernel(*refs, seq, final):
    if seq:
        or_ref = refs[0]
        pos = 1
        o_r = or_ref[...]
    else:
        y_ref, r_ref, k_ref, v_ref, zr_ref, lg_ref, lb_ref, rk_ref = refs[:8]
        pos = 8
        v = v_ref[...]
        o_r = _rwkv_gate(y_ref[...], _bonus(r_ref[...], k_ref[...], v, rk_ref[...]),
                         zr_ref[...], lg_ref[...], lb_ref[...]).astype(BF16)
    u_ref, vg_ref, zg_ref, h_ref, gg_ref, gb_ref, ws_ref, bias_ref, w_ref = refs[pos:pos + 9]
    rest = refs[pos + 9:]
    rw = o_r.shape[1]
    mixed, vn = _gmlp_mix(vg_ref[...], gg_ref[...], gb_ref[...], ws_ref, bias_ref, seq)
    o_g = u_ref[...] * mixed * _silu(zg_ref[...])
    h = h_ref[...] + _dot(o_r, w_ref[:rw, :]) + _dot(o_g.astype(BF16), w_ref[rw:, :])
    if final:
        ms = jnp.mean(h * h, axis=-1, keepdims=True)
        h = h * lax.rsqrt(ms + RMS_EPS) * rest[0][...]
    outs = rest[1:] if final else rest
    outs[0][...] = h
    if not seq:
        outs[1][...] = vn


def _mix_out(branch, p, h, gln_g, gln_b, ws, bias, w, layer, final_g, *, seq, tm, u_blk):
    m, d = h.shape
    gw = gln_g.shape[-1]
    rw = branch[0].shape[1]
    row = pl.BlockSpec((tm, rw), lambda i: (i, 0))
    col = lambda c: pl.BlockSpec((tm, gw), lambda i, c=c: (i, c))
    hrow = pl.BlockSpec((tm, d), lambda i: (i, 0))
    lspec = lambda t: _layer_spec(t, layer)
    if seq:
        in_specs = [row]
        args = list(branch)
    else:
        y, r, k, v, lnx_g, lnx_b, r_k = branch
        in_specs = [row, row, row, row, col(u_blk - 1), lspec(lnx_g), lspec(lnx_b), lspec(r_k)]
        args = [y, r, k, v, p, lnx_g, lnx_b, r_k]
    in_specs += [col(u_blk), col(u_blk + 1), col(u_blk + 2), hrow, lspec(gln_g), lspec(gln_b),
                 lspec(ws), lspec(bias),
                 pl.BlockSpec((rw + gw, d), lambda i: (0, 0), pipeline_mode=pl.Buffered(1))]
    args += [p, p, p, h, gln_g, gln_b, ws, bias, w]
    if final_g is not None:
        in_specs.append(pl.BlockSpec((1, d), lambda i: (0, 0)))
        args.append(final_g)
    out_specs = [hrow]
    out_shape = [jax.ShapeDtypeStruct((m, d), F32)]
    if not seq:
        out_specs.append(pl.BlockSpec((tm, gw), lambda i: (i, 0)))
        out_shape.append(jax.ShapeDtypeStruct((m, gw), F32))
    outs = pl.pallas_call(
        functools.partial(_mix_out_kernel, seq=seq, final=final_g is not None),
        grid=(m // tm,),
        in_specs=in_specs,
        out_specs=out_specs,
        out_shape=out_shape,
        compiler_params=_cparams(("parallel",)),
        name="mix_out",
    )(*args)
    return (outs[0], None) if seq else (outs[0], outs[1])


def _pad_last(x, width):
    return jnp.pad(x, [(0, 0)] * (x.ndim - 1) + [(0, width - x.shape[-1])])


def _pad_rows(x, rows):
    return jnp.pad(x, [(0, 0)] * (x.ndim - 2) + [(0, rows - x.shape[-2]), (0, 0)])


def _pick_tile(m, pref, mult):
    t = min(m, pref)
    while m % t or t % mult:
        t -= mult
    return t


def _trunk(x, s_init, shift_init, prm, weights, *, seq):
    n_seq, seq_len, d = x.shape
    m = n_seq * seq_len
    depth = prm["depth"]
    rw, gw = prm["rw"], prm["gw"]
    lora_w = prm["lora_w"]
    n_heads = rw // HEAD
    tn = 1024
    z_blk = 3

    tm_mm = _pick_tile(m, 1024, 8)
    tm_ew = _pick_tile(m, 512, GMLP_CHUNK if seq else 8)
    assert not seq or seq_len % SEQ_ROWS == 0

    h = x.reshape(m, d)
    s_out, shift_out, vn_out = [], [], []
    prev_rkv = shift_init[:, :, :3 * rw]
    prev_lo = _pad_last(shift_init[:, :, 3 * rw:], LO_W)
    if seq:
        prev_rkv, prev_lo = prev_rkv[:, :, None, :], prev_lo[:, :, None, :]
        s0 = s_init.reshape(depth, n_seq, n_heads // 2, 2, HEAD, HEAD)
        z = jnp.zeros_like(s0[:, :, :, 0])
        s0_bd = jnp.concatenate([jnp.concatenate([s0[:, :, :, 0], z], axis=-1),
                                 jnp.concatenate([z, s0[:, :, :, 1]], axis=-1)], axis=-2)
    else:
        s_init = jnp.transpose(s_init, (0, 2, 3, 4, 1))
        s_buf = jnp.zeros_like(s_init)
    tok = (prm["mu_rkv"],) * 3 + tuple(prm[n] for n in ("mu_lo", "w0", "w2", "a0", "a2",
                                                         "k_k", "k_a"))
    gate = (prm["r_k"], prm["lnx_g"], prm["lnx_b"])
    v_first = None
    w_in_bf, w_out_bf = ([weights], []) if seq else weights
    for l in range(depth):
        p, p_lo = _in_proj(h, prm["norm_g"], w_in_bf[l], l, tm_mm, tn, rw, gw, lora_w)
        vres = (v_first, prm["v0"], prm["v1"], prm["v2"]) if l > 0 else None

        if seq:
            casts = [(prm["w_out"], l)] + ([(prm["w_in"], l + 1)] if l + 1 < depth else [])
            o_r, v, s_bd, cast = _wkv_seq(p, p_lo, prev_rkv, prev_lo, tok, gate, s0_bd, vres, l,
                                          casts, n_seq=n_seq, seq_len=seq_len, rw=rw,
                                          z_blk=z_blk, emit_v=l == 0)
            w_out_bf.append(cast[0])
            w_in_bf += cast[1:]
            s_out.append(s_bd)
            branch = (o_r,)
            ws, bias = prm["w_s"], prm["bias_seq"]
        else:
            r, lw, k, v, a, b = _rwkv_prep(p, p_lo, prev_rkv, prev_lo, tok, vres, l,
                                           tm=tm_ew, rw=rw)
            y, s_buf = _wkv_step(r, lw, k, v, a, b, s_init, s_buf, l)
            branch = (y, r, k, v, gate[1], gate[2], gate[0])
            ws, bias = prm["w_s0"], prm["bias0"]
        if l == 0:
            v_first = v

        last_row = lambda t: t.reshape(n_seq, seq_len, -1)[:, -1]
        shift_out.append(jnp.concatenate(
            [last_row(p)[:, :3 * rw], last_row(p_lo)[:, :lora_w]], axis=-1))

        h, vn = _mix_out(branch, p, h, prm["gln_g"], prm["gln_b"], ws, bias,
                         w_out_bf[l], l, prm["final_g"] if l == depth - 1 else None,
                         seq=seq, tm=tm_ew, u_blk=z_blk + 1)
        if not seq:
            vn_out.append(vn.reshape(n_seq, seq_len, gw))
    if seq:
        s_bd = jnp.stack(s_out)
        s_final = jnp.stack([s_bd[..., :HEAD, :HEAD], s_bd[..., HEAD:, HEAD:]], axis=3)
        s_final = s_final.reshape(depth, n_seq, n_heads, HEAD, HEAD)
    else:
        s_final = jnp.transpose(s_buf, (0, 4, 1, 2, 3))
    return h.reshape(n_seq, seq_len, d), s_final, shift_out, vn_out, (w_in_bf, w_out_bf)


def kernel(x_prompt, x_sample, state_wkv, state_shift, norm_g, w_in, mu_shift, w0, w2, a0, a2,
           k_k, k_a, r_k, lnx_g, lnx_b, v0, v1, v2, gln_g, gln_b, w_s, b_s, w_out, final_g):
    depth, d, _ = w_in.shape
    rw = w0.shape[1]
    gw = gln_g.shape[1]
    dl, il = w2.shape[1], a2.shape[1]
    assert rw % PAIR == 0 and gw == rw and w_s.shape[2] == GMLP_CHUNK
    assert gw // GMLP_GROUPS == LANES and dl + il <= LO_W
    sc = 3 * rw + dl + il

    row3 = lambda t: t[:, None, :]
    prm = dict(
        depth=depth, rw=rw, gw=gw, lora_w=dl + il,
        norm_g=row3(norm_g),
        w_in=jnp.swapaxes(w_in, 1, 2),
        mu_rkv=row3(mu_shift[:, :3 * rw]),
        mu_lo=row3(_pad_last(mu_shift[:, 3 * rw:], LO_W)),
        w0=row3(w0), w2=_pad_rows(w2, LO_W).astype(BF16),
        a0=row3(a0),
        a2=jnp.pad(a2, ((0, 0), (dl, LO_W - dl - il), (0, 0))).astype(BF16),
        k_k=row3(k_k), k_a=row3(k_a), r_k=row3(r_k.reshape(depth, rw)),
        lnx_g=row3(lnx_g), lnx_b=row3(lnx_b),
        v0=row3(v0), v1=_pad_last(v1, LORA_PAD).astype(BF16), v2=_pad_rows(v2, LORA_PAD).astype(BF16),
        gln_g=row3(gln_g), gln_b=row3(gln_b),
        w_s=w_s,
        bias_seq=jnp.repeat(jnp.swapaxes(b_s, 1, 2), gw // GMLP_GROUPS, axis=2),
        w_s0=row3(jnp.repeat(w_s[:, :, 0, 0], gw // GMLP_GROUPS, axis=1)),
        bias0=row3(jnp.repeat(b_s[:, :, 0], gw // GMLP_GROUPS, axis=1)),
        w_out=w_out,
        final_g=final_g[None, :],
    )

    nb = x_prompt.shape[0]
    n_heads = rw // HEAD
    s0_p = jnp.zeros((depth, nb, n_heads, HEAD, HEAD), F32)
    sh0_p = jnp.zeros((depth, nb, sc), F32)
    y_p, s_p, sh_p, _, weights = _trunk(x_prompt, s0_p, sh0_p, prm,
                                        prm["w_in"][0].astype(BF16), seq=True)
    y_s, s_s, sh_s, vn_s, _ = _trunk(x_sample, state_wkv, state_shift, prm, weights, seq=False)
    return (y_p, y_s, s_p, jnp.stack(sh_p), s_s, jnp.stack(sh_s), jnp.stack(vn_s))
```

```python
import functools

import jax
import jax.numpy as jnp
from jax import lax
from jax.experimental import pallas as pl
from jax.experimental.pallas import tpu as pltpu

F32 = jnp.float32
BF16 = jnp.bfloat16

HEAD = 64
LANES = 128
PAIR = 2 * HEAD
WKV_CHUNK = HEAD
SEQ_ROWS = 2 * WKV_CHUNK
GMLP_CHUNK = 128
GMLP_GROUPS = 8
LORA_PAD = 128
LO_W = 256
RMS_EPS = 1e-6
LN_EPS = 1e-5
LNX_EPS = 64e-5
VMEM_LIMIT = 56 * 1024 * 1024

_NT = (((1,), (1,)), ((), ()))
_TN = (((0,), (0,)), ((), ()))


def _dot(a, b):
    return jnp.dot(a, b, preferred_element_type=F32)


def _dot_nt(a, b):
    return lax.dot_general(a, b, _NT, preferred_element_type=F32)


def _dot_tn(a, b):
    return lax.dot_general(a, b, _TN, preferred_element_type=F32)


def _cparams(sem):
    return pltpu.CompilerParams(dimension_semantics=sem, vmem_limit_bytes=VMEM_LIMIT)


def _silu(z):
    return z * jax.nn.sigmoid(z)


def _head_ones():
    r = lax.broadcasted_iota(jnp.int32, (LANES, LANES), 0) // HEAD
    c = lax.broadcasted_iota(jnp.int32, (LANES, LANES), 1) // HEAD
    return jnp.where(r == c, 1.0, 0.0).astype(BF16)


def _split_hi_lo(x):
    hi = x.astype(BF16)
    return hi, (x - hi.astype(F32)).astype(BF16)


def _dot_hi_lo(hi_lo, mat, mat_left):
    hi, lo = hi_lo
    return (_dot(mat, hi) + _dot(mat, lo)) if mat_left else (_dot(hi, mat) + _dot(lo, mat))


def _head_sum_hi_lo(hi_lo, ones_bd):
    hi, lo = hi_lo
    outs = [_dot_hi_lo((hi[:, j * LANES:(j + 1) * LANES], lo[:, j * LANES:(j + 1) * LANES]),
                       ones_bd, False) for j in range(hi.shape[1] // LANES)]
    return jnp.concatenate(outs, axis=1)


def _head_sum(x, ones_bd):
    return _head_sum_hi_lo(_split_hi_lo(x), ones_bd)


def _chunk_tri(rows):
    ti = lax.broadcasted_iota(jnp.int32, (rows, rows), 0)
    si = lax.broadcasted_iota(jnp.int32, (rows, rows), 1)
    same_chunk = si // WKV_CHUNK == ti // WKV_CHUNK
    return jnp.where((si <= ti) & same_chunk, 1.0, 0.0).astype(BF16)


def _decay_and_iclr(wraw, araw):
    w = -jax.nn.softplus(-wraw) - 0.5
    return -jnp.exp(w), jax.nn.sigmoid(araw)


_NORM_ROWS = 256


def _in_proj_kernel(h_ref, g_ref, w_ref, wlo_ref, o_ref, lo_ref, xn_ref):
    @pl.when(pl.program_id(1) == 0)
    def _():
        g = g_ref[...]
        chunk = min(_NORM_ROWS, h_ref.shape[0])

        def norm_rows(c, carry):
            rows = pl.ds(pl.multiple_of(c * chunk, chunk), chunk)
            x = h_ref[rows, :]
            ms = jnp.mean(x * x, axis=-1, keepdims=True)
            xn_ref[rows, :] = (x * lax.rsqrt(ms + RMS_EPS) * g).astype(BF16)
            return carry

        lax.fori_loop(0, h_ref.shape[0] // chunk, norm_rows, 0)
        lo_ref[...] = _dot_nt(xn_ref[...], wlo_ref[...])

    o_ref[...] = _dot_nt(xn_ref[...], w_ref[...])


_ROW_ALIGN = 16


def _in_proj(h, g, wt, layer, tm, tn, rw, gw, lora_w):
    m, d = h.shape
    in_cols = 3 * rw + lora_w + rw + 3 * gw
    n_rkv = 3 * rw // tn
    n_main = n_rkv + (rw + 3 * gw) // tn
    assert (3 * rw) % tn == 0 and (rw + 3 * gw) % tn == 0 and lora_w <= LO_W
    assert in_cols % _ROW_ALIGN == 0 and tn % _ROW_ALIGN == 0 and lora_w % _ROW_ALIGN == 0
    assert wt.shape == (in_cols, d)
    step = tn // _ROW_ALIGN
    z0 = (3 * rw + lora_w) // _ROW_ALIGN
    lo0 = 3 * rw // _ROW_ALIGN

    def w_index(i, j):
        start = jnp.where(j < n_rkv, step * j, z0 + step * (j - n_rkv))
        return (start * _ROW_ALIGN, 0)

    return pl.pallas_call(
        _in_proj_kernel,
        grid=(m // tm, n_main),
        in_specs=[
            pl.BlockSpec((tm, d), lambda i, j: (i, 0)),
            _layer_spec(g, layer),
            pl.BlockSpec((pl.Element(tn), pl.Element(d)), w_index),
            pl.BlockSpec((pl.Element(LO_W), pl.Element(d)),
                         lambda i, j: (lo0 * _ROW_ALIGN, 0)),
        ],
        out_specs=[pl.BlockSpec((tm, tn), lambda i, j: (i, j)),
                   pl.BlockSpec((tm, LO_W), lambda i, j: (i, 0))],
        out_shape=[jax.ShapeDtypeStruct((m, n_main * tn), F32),
                   jax.ShapeDtypeStruct((m, LO_W), F32)],
        scratch_shapes=[pltpu.VMEM((tm, d), BF16)],
        compiler_params=_cparams(("parallel", "arbitrary")),
        name="in_proj",
    )(h, g, wt, wt)


def _rwkv_tokens(k, v, tanh_lo, lo, w0, w2, a0, a2, k_k, k_a, vres):
    lw, a = _decay_and_iclr(w0 + _dot(tanh_lo, w2[...]), a0 + _dot(lo, a2[...]))
    if vres is not None:
        v_first, v0, vv_mid, v2 = vres
        v = v + (v_first - v) * jax.nn.sigmoid(v0 + _dot(vv_mid, v2[...]))
    kk = k * k_k
    kk = kk / jnp.maximum(jnp.sqrt(_head_sum(kk * kk, _head_ones())), 1e-12)
    return lw, k * (1.0 + (a - 1.0) * k_a), v, -kk, kk * a


def _rwkv_gate(y, bonus, z, g, b):
    ones_bd = _head_ones()
    mu = _head_sum(y, ones_bd) * (1.0 / HEAD)
    d = y - mu
    var = _head_sum(d * d, ones_bd) * (1.0 / HEAD)
    return (d * lax.rsqrt(var + LNX_EPS) * g + b + bonus) * _silu(z)


def _bonus(r, k, v, rk):
    return _head_sum(r * k * rk, _head_ones()) * v


def _rwkv_prep_kernel(*refs, has_vres):
    (r_ref, k_ref, v_ref, lo_ref, pr_ref, pk_ref, pv_ref, plo_ref,
     mur_ref, muk_ref, muv_ref, mulo_ref, w0_ref, w2_ref, a0_ref, a2_ref,
     kk_ref, ka_ref) = refs[:18]
    pos = 18
    if has_vres:
        vf_ref, v0_ref, v1_ref, v2_ref = refs[pos:pos + 4]
        pos += 4
    ro_ref, lwo_ref, ko_ref, vo_ref, ao_ref, bo_ref = refs[pos:pos + 6]

    def mix(x_ref, p_ref, mu_ref):
        x = x_ref[...]
        return x + (p_ref[...] - x) * mu_ref[...]

    lo = mix(lo_ref, plo_ref, mulo_ref)
    v = mix(v_ref, pv_ref, muv_ref)
    vres = None
    if has_vres:
        vres = (vf_ref[...], v0_ref[...], _dot(v.astype(BF16), v1_ref[...]).astype(BF16), v2_ref)
    lw, k, v, a, b = _rwkv_tokens(
        mix(k_ref, pk_ref, muk_ref), v, jnp.tanh(lo).astype(BF16), lo.astype(BF16),
        w0_ref[...], w2_ref, a0_ref[...], a2_ref, kk_ref[...], ka_ref[...], vres)
    ro_ref[...] = mix(r_ref, pr_ref, mur_ref)
    lwo_ref[...] = lw
    ko_ref[...] = k
    vo_ref[...] = v
    ao_ref[...] = a
    bo_ref[...] = b


def _layer_spec(arr, layer):
    zeros = (0,) * (arr.ndim - 1)
    return pl.BlockSpec((None,) + arr.shape[1:], lambda *g: (layer,) + zeros)


def _token_param_specs(tok, rw, layer):
    return ([pl.BlockSpec((None, 1, rw), lambda *g, c=c: (layer, 0, c)) for c in range(3)]
            + [_layer_spec(t, layer) for t in tok[3:]])


def _vres_specs(vres, rows, rw, row_index, layer):
    return ([pl.BlockSpec((rows, rw), row_index)]
            + [_layer_spec(t, layer - 1) for t in vres[1:]])


def _rwkv_prep(p, p_lo, prev_rkv, prev_lo, tok, vres, layer, *, tm, rw):
    m = p.shape[0]
    row = lambda i: (i, 0)
    col = lambda c: (lambda i: (i, c))
    in_specs = [pl.BlockSpec((tm, rw), col(0)), pl.BlockSpec((tm, rw), col(1)),
                pl.BlockSpec((tm, rw), col(2)), pl.BlockSpec((tm, LO_W), row)]
    in_specs += [pl.BlockSpec((None, tm, rw), lambda i, c=c: (layer, i, c)) for c in range(3)]
    in_specs.append(pl.BlockSpec((None, tm, LO_W), lambda i: (layer, i, 0)))
    args = [p, p, p, p_lo, prev_rkv, prev_rkv, prev_rkv, prev_lo]
    in_specs += _token_param_specs(tok, rw, layer)
    args += list(tok)
    if vres is not None:
        in_specs += _vres_specs(vres, tm, rw, row, layer)
        args += list(vres)
    out = jax.ShapeDtypeStruct((m, rw), F32)
    return pl.pallas_call(
        functools.partial(_rwkv_prep_kernel, has_vres=vres is not None),
        grid=(m // tm,),
        in_specs=in_specs,
        out_specs=[pl.BlockSpec((tm, rw), row)] * 6,
        out_shape=[out] * 6,
        compiler_params=_cparams(("parallel",)),
        name="rwkv_prep",
    )(*args)


def _shift_seq(x, mu, init_row, carry_ref, sl, first):
    prev0 = jnp.where(first, init_row, carry_ref[:, sl])
    sh = pltpu.roll(x, 1, 0)
    rows = lax.broadcasted_iota(jnp.int32, x.shape, 0)
    sh = jnp.where(rows == 0, prev0, sh)
    carry_ref[:, sl] = x[x.shape[0] - 1:, :]
    return x + (sh - x) * mu


_EMPTY_FILL_SLOTS = (2, 4, 6, 8)

_CHAIN_OPERANDS = ("at", "rt", "btl", "bth", "ktl", "kth", "vl", "vh", "be", "ke")


def _chain_operands(r, lw, lp, k, v, a, b):
    rows, width = r.shape
    c_len = WKV_CHUNK
    n_sub = rows // c_len
    p_ends = [jnp.exp(lp[(ci + 1) * c_len - 1:(ci + 1) * c_len]) for ci in range(n_sub)]
    pinv = jnp.exp(-lp)
    pend = pinv * jnp.concatenate(
        [jnp.broadcast_to(e, (c_len, width)) for e in p_ends], axis=0)
    first_head = lax.broadcasted_iota(jnp.int32, (rows, width), 1) < HEAD
    halves = lambda z: (jnp.where(first_head, z, 0.0), jnp.where(first_head, 0.0, z))
    ops = ((a * jnp.exp(lp - lw), r * jnp.exp(lp)) + halves(b * pinv) + halves(k * pinv)
           + halves(v) + (b * pend, k * pend))
    return [o.astype(BF16) for o in ops], jnp.concatenate(p_ends, axis=0)


def _wkv_chain(ops, plast_ref, rd, s_ref, fill):
    _, rows, width = ops["at"].shape
    c_len = WKV_CHUNK
    n_sub = rows // c_len
    n_pairs = width // PAIR

    lane = lax.broadcasted_iota(jnp.int32, (c_len, PAIR), 1)
    rowi = lax.broadcasted_iota(jnp.int32, (c_len, PAIR), 0)
    lo_half = lane < HEAD
    scol = lane % HEAD
    strict = scol < rowi
    incl = scol <= rowi
    r2 = lax.broadcasted_iota(jnp.int32, (PAIR, PAIR), 0)
    c2 = lax.broadcasted_iota(jnp.int32, (PAIR, PAIR), 1)
    same_head = (r2 // HEAD) == (c2 // HEAD)
    eye_pair = jnp.where(scol == rowi, 1.0, 0.0).astype(F32)

    def bd(z):
        return jnp.concatenate([jnp.where(lo_half, z, 0.0), jnp.where(lo_half, 0.0, z)], axis=0)

    chains = [(ci, pi) for ci in range(n_sub) for pi in range(n_pairs)]
    rs = {c: slice(c[0] * c_len, (c[0] + 1) * c_len) for c in chains}
    ls = {c: slice(c[1] * PAIR, (c[1] + 1) * PAIR) for c in chains}
    ld = lambda name, c: ops[name][rd, rs[c], ls[c]]
    x, g = {}, {}
    for c in chains:
        x[c] = jnp.concatenate([ld("at", c), ld("rt", c)], axis=0)
        rhs = jnp.concatenate([ld("btl", c), ld("bth", c), ld("ktl", c), ld("kth", c)], axis=0)
        g[c] = _dot_nt(x[c], rhs)
    fill()
    q_b = {c: jnp.where(incl, g[c][c_len:, :PAIR], 0.0).astype(BF16) for c in chains}
    mv = {}
    for c in chains:
        gk = g[c][:, PAIR:]
        gkm = jnp.concatenate([jnp.where(strict, gk[:c_len], 0.0),
                               jnp.where(incl, gk[c_len:], 0.0)], axis=0)
        mv[c] = _dot(gkm.astype(BF16), jnp.concatenate([ld("vl", c), ld("vh", c)], axis=0))
    fill()

    pw = {c: jnp.where(strict, g[c][:c_len, :PAIR], 0.0) for c in chains}
    tinv = {c: eye_pair + pw[c] for c in chains}
    pw = {c: _dot(pw[c].astype(BF16), bd(pw[c]).astype(BF16)) for c in chains}
    fill()
    n_sq = (c_len - 1).bit_length() - 1
    for j in range(n_sq):
        for c in chains:
            pb = pw[c].astype(BF16)
            if j == n_sq - 1:
                tinv[c] = tinv[c] + _dot(pb, bd(tinv[c]).astype(BF16))
            else:
                res = _dot(pb, jnp.concatenate([bd(tinv[c]), bd(pw[c])], axis=1).astype(BF16))
                tinv[c] = tinv[c] + res[:, :PAIR]
                pw[c] = res[:, PAIR:]
        fill()

    s_cur = [s_ref[pi] for pi in range(n_pairs)]
    y_rows = []
    for ci in range(n_sub):
        cs = [(ci, pi) for pi in range(n_pairs)]
        xs = {c: _dot_nt(x[c], s_cur[c[1]].astype(BF16)) for c in cs}
        fill()
        u = {c: _dot(tinv[c].astype(BF16), bd(xs[c][:c_len] + mv[c][:c_len]).astype(BF16))
             for c in cs}
        fill()
        y_cols = []
        for c in cs:
            y_cols.append(xs[c][c_len:] + mv[c][c_len:] + _dot(q_b[c], bd(u[c]).astype(BF16)))
            uv = jnp.concatenate([u[c].astype(BF16), ld("vl", c) + ld("vh", c)], axis=0)
            bk = jnp.concatenate([ld("be", c), ld("ke", c)], axis=0)
            p_last = plast_ref[rd, ci:ci + 1, ls[c]]
            s_cur[c[1]] = s_cur[c[1]] * p_last + jnp.where(same_head, _dot_tn(uv, bk), 0.0)
        y_rows.append(jnp.concatenate(y_cols, axis=1))
        fill()
    for pi in range(n_pairs):
        s_ref[pi] = s_cur[pi]
    return jnp.concatenate(y_rows, axis=0)


def _wkv_seq_kernel(*refs, has_vres, emit_v, blocks_per_seq, cast_steps):
    (r_ref, k_ref, v_ref, lo_ref, z_ref, pr_ref, pk_ref, pv_ref, plo_ref,
     mur_ref, muk_ref, muv_ref, mulo_ref, w0_ref, w2_ref, a0_ref, a2_ref,
     kk_ref, ka_ref, rk_ref, lg_ref, lb_ref, s0_ref) = refs[:23]
    pos = 23
    if has_vres:
        vf_ref, v0_ref, v1_ref, v2_ref = refs[pos:pos + 4]
        pos += 4
    n_cast = len(cast_steps)
    cast_in = refs[pos:pos + n_cast]
    pos += n_cast
    o_ref = refs[pos]
    pos += 1
    if emit_v:
        vo_ref = refs[pos]
        pos += 1
    s_ref = refs[pos]
    cast_out = refs[pos + 1:pos + 1 + n_cast]
    pos += 1 + n_cast
    cr_ref, ck_ref, cv_ref, clo_ref = refs[pos:pos + 4]
    pos += 4
    ops = dict(zip(_CHAIN_OPERANDS, refs[pos:pos + len(_CHAIN_OPERANDS)]))
    vs_ref, bonus_ref, plast_ref = refs[pos + len(_CHAIN_OPERANDS):]

    j = pl.program_id(0)
    first = (j % blocks_per_seq) == 0
    wr = j % 2
    rd = 1 - wr

    @pl.when(j == 0)
    def _():
        for ref in list(ops.values()) + [vs_ref, bonus_ref]:
            ref[1] = jnp.zeros(ref.shape[1:], ref.dtype)
        plast_ref[1] = jnp.ones(plast_ref.shape[1:], F32)

    @pl.when((jnp.maximum(j - 1, 0) % blocks_per_seq) == 0)
    def _():
        s_ref[...] = s0_ref[...]

    rows, width = vs_ref.shape[1:]
    n_sub = rows // WKV_CHUNK
    n_lane_blocks = width // LANES
    ones_bd = _head_ones()
    tri = _chunk_tri(rows)
    shared, mid = {}, {}

    def shared_vector_part():
        lo = _shift_seq(lo_ref[...], mulo_ref[...], plo_ref[...], clo_ref, slice(None), first)
        shared["tanh_lo"] = jnp.tanh(lo).astype(BF16)
        shared["lo"] = lo.astype(BF16)
        shared["v"] = _shift_seq(v_ref[...], muv_ref[...], pv_ref[...], cv_ref, slice(None), first)

    def shared_matmul_part():
        shared["w_lora"] = _dot(shared["tanh_lo"], w2_ref[...])
        shared["a_lora"] = _dot(shared["lo"], a2_ref[...])
        if has_vres:
            shared["vv_mid"] = _dot(shared["v"].astype(BF16), v1_ref[...])

    def shared_matmul_part2():
        if has_vres:
            shared["vv"] = _dot(shared["vv_mid"].astype(BF16), v2_ref[...])

    def block_vector_part(cb):
        sl = slice(cb * LANES, (cb + 1) * LANES)
        r = _shift_seq(r_ref[:, sl], mur_ref[:, sl], pr_ref[:, sl], cr_ref, sl, first)
        k = _shift_seq(k_ref[:, sl], muk_ref[:, sl], pk_ref[:, sl], ck_ref, sl, first)
        v = shared["v"][:, sl]
        if has_vres:
            v = v + (vf_ref[:, sl] - v) * jax.nn.sigmoid(v0_ref[:, sl] + shared["vv"][:, sl])
        lw, a = _decay_and_iclr(w0_ref[:, sl] + shared["w_lora"][:, sl],
                                a0_ref[:, sl] + shared["a_lora"][:, sl])
        kk = k * kk_ref[:, sl]
        k = k * (1.0 + (a - 1.0) * ka_ref[:, sl])
        mid[cb] = dict(r=r, k=k, v=v, a=a, lw=lw, kk=kk, kk2=_split_hi_lo(kk * kk),
                       rk=_split_hi_lo(r * k * rk_ref[:, sl]), lws=_split_hi_lo(lw))

    def block_matmul_part(cb):
        sl = slice(cb * LANES, (cb + 1) * LANES)
        m = mid.pop(cb)
        kk = m["kk"] / jnp.maximum(jnp.sqrt(_dot_hi_lo(m["kk2"], ones_bd, False)), 1e-12)
        bonus = _dot_hi_lo(m["rk"], ones_bd, False) * m["v"]
        lp = _dot_hi_lo(m["lws"], tri, True)
        vals, p_last = _chain_operands(m["r"], m["lw"], lp, m["k"], m["v"], -kk, kk * m["a"])
        for name, val in zip(_CHAIN_OPERANDS, vals):
            ops[name][wr, :, sl] = val
        vs_ref[wr, :, sl] = m["v"]
        bonus_ref[wr, :, sl] = bonus
        plast_ref[wr, :n_sub, sl] = p_last

    def piece(t):
        def run():
            if 0 < t <= n_lane_blocks:
                block_matmul_part(t - 1)
            if t < n_lane_blocks:
                block_vector_part(t)
        return run

    shared_vector_part()
    shared_matmul_part()
    pending = [shared_matmul_part2] + [piece(t) for t in range(n_lane_blocks + 1)]

    slot = [0]

    def fill():
        slot[0] += 1
        if pending and slot[0] not in _EMPTY_FILL_SLOTS:
            pending.pop(0)()

    y = _wkv_chain(ops, plast_ref, rd, s_ref, fill)
    while pending:
        fill()
    o_ref[...] = _rwkv_gate(y, bonus_ref[rd], z_ref[...], lg_ref[...], lb_ref[...]
                            ).astype(o_ref.dtype)
    if emit_v:
        vo_ref[...] = vs_ref[rd]
    for src_ref, dst_ref, steps in zip(cast_in, cast_out, cast_steps):
        @pl.when(j < steps)
        def _(src_ref=src_ref, dst_ref=dst_ref):
            dst_ref[...] = src_ref[...].astype(dst_ref.dtype)


def _wkv_seq(p, p_lo, prev_rkv, prev_lo, tok, gate, s0_bd, vres, layer, casts, *, n_seq,
             seq_len, rw, z_blk, emit_v):
    m = p.shape[0]
    n_blocks = seq_len // SEQ_ROWS
    total = n_seq * n_blocks
    n_pairs = rw // PAIR
    prep_blk = lambda c: jnp.minimum(c, total - 1)
    run_blk = lambda c: jnp.maximum(c - 1, 0)
    row = lambda c: (prep_blk(c), 0)
    col = lambda j: (lambda c: (prep_blk(c), j))
    blk = lambda j: pl.BlockSpec((SEQ_ROWS, rw), col(j))
    in_specs = [blk(0), blk(1), blk(2), pl.BlockSpec((SEQ_ROWS, LO_W), row),
                pl.BlockSpec((SEQ_ROWS, rw), lambda c: (run_blk(c), z_blk))]
    args = [p, p, p, p_lo, p]
    for j in range(3):
        in_specs.append(pl.BlockSpec((None, None, 1, rw),
                                     lambda c, j=j: (layer, prep_blk(c) // n_blocks, 0, j)))
    in_specs.append(pl.BlockSpec((None, None, 1, LO_W),
                                 lambda c: (layer, prep_blk(c) // n_blocks, 0, 0)))
    args += [prev_rkv, prev_rkv, prev_rkv, prev_lo]
    in_specs += _token_param_specs(tok, rw, layer)
    args += list(tok)
    in_specs += [_layer_spec(t, layer) for t in gate]
    args += list(gate)
    in_specs.append(pl.BlockSpec((None, None, n_pairs, PAIR, PAIR),
                                 lambda c: (layer, run_blk(c) // n_blocks, 0, 0, 0)))
    args.append(s0_bd)
    if vres is not None:
        in_specs += _vres_specs(vres, SEQ_ROWS, rw, row, layer)
        args += list(vres)
    cast_steps, cast_specs, cast_shapes = [], [], []
    for w_all, w_layer in casts:
        n_rows, n_cols = w_all.shape[1:]
        rows = next(r for r in range(_ROW_ALIGN, n_rows + 1, _ROW_ALIGN)
                    if n_rows % r == 0 and n_rows // r <= total + 1)
        steps = n_rows // rows
        in_specs.append(pl.BlockSpec((None, rows, n_cols), lambda c, w_layer=w_layer, steps=steps:
                                     (w_layer, jnp.minimum(c, steps - 1), 0)))
        args.append(w_all)
        cast_specs.append(pl.BlockSpec((rows, n_cols), lambda c, steps=steps:
                                       (jnp.minimum(c, steps - 1), 0)))
        cast_shapes.append(jax.ShapeDtypeStruct((n_rows, n_cols), BF16))
        cast_steps.append(steps)
    run_rows = pl.BlockSpec((SEQ_ROWS, rw), lambda c: (run_blk(c), 0))
    out_specs = [run_rows]
    out_shape = [jax.ShapeDtypeStruct((m, rw), BF16)]
    if emit_v:
        out_specs.append(run_rows)
        out_shape.append(jax.ShapeDtypeStruct((m, rw), F32))
    out_specs.append(pl.BlockSpec((None, n_pairs, PAIR, PAIR),
                                  lambda c: (run_blk(c) // n_blocks, 0, 0, 0)))
    out_shape.append(jax.ShapeDtypeStruct((n_seq, n_pairs, PAIR, PAIR), F32))
    out_specs += cast_specs
    out_shape += cast_shapes
    outs = pl.pallas_call(
        functools.partial(_wkv_seq_kernel, has_vres=vres is not None, emit_v=emit_v,
                          blocks_per_seq=n_blocks, cast_steps=tuple(cast_steps)),
        grid=(total + 1,),
        in_specs=in_specs,
        out_specs=out_specs,
        out_shape=out_shape,
        scratch_shapes=([pltpu.VMEM((1, rw), F32)] * 3 + [pltpu.VMEM((1, LO_W), F32)]
                        + [pltpu.VMEM((2, SEQ_ROWS, rw), BF16)] * len(_CHAIN_OPERANDS)
                        + [pltpu.VMEM((2, SEQ_ROWS, rw), F32)] * 2
                        + [pltpu.VMEM((2, 8, rw), F32)]),
        compiler_params=_cparams(("arbitrary",)),
        name="wkv_seq",
    )(*args)
    outs = list(outs)
    o_r = outs.pop(0)
    v = outs.pop(0) if emit_v else None
    return o_r, v, outs[0], outs[1:]


def _wkv_step_kernel(r_ref, lw_ref, k_ref, v_ref, a_ref, b_ref, s_ref, _, y_ref, so_ref,
                     vt_ref, yt_ref):
    r_t, k_t, a_t, b_t = r_ref[...].T, k_ref[...].T, a_ref[...].T, b_ref[...].T
    w_t = jnp.exp(lw_ref[...]).T
    vt_ref[...] = v_ref[...].T
    for hh in range(PAIR // HEAD):
        sl = slice(hh * HEAD, (hh + 1) * HEAD)
        r, k, a, b, w = r_t[sl], k_t[sl], a_t[sl], b_t[sl], w_t[sl]

        def body(vi, carry, hh=hh, r=r, k=k, a=a, b=b, w=w):
            s = s_ref[hh, vi]
            sa = jnp.sum(s * a, axis=0, keepdims=True)
            s_new = s * w + sa * b + vt_ref[pl.ds(hh * HEAD + vi, 1), :] * k
            so_ref[hh, vi] = s_new
            yt_ref[pl.ds(hh * HEAD + vi, 1), :] = jnp.sum(s_new * r, axis=0, keepdims=True)
            return carry

        lax.fori_loop(0, HEAD, body, 0, unroll=8)
    y_ref[...] = yt_ref[...].T


def _wkv_step(r, lw, k, v, a, b, s_all, s_out_all, layer):
    m, rw = r.shape
    hp = PAIR // HEAD
    assert m % LANES == 0
    blk = pl.BlockSpec((LANES, PAIR), lambda pi, j: (j, pi))
    sblk = pl.BlockSpec((None, hp, HEAD, HEAD, LANES), lambda pi, j: (layer, pi, 0, 0, j))
    return pl.pallas_call(
        _wkv_step_kernel,
        grid=(rw // PAIR, m // LANES),
        in_specs=[blk] * 6 + [sblk, pl.BlockSpec(memory_space=pl.ANY)],
        out_specs=[blk, sblk],
        out_shape=[jax.ShapeDtypeStruct((m, rw), F32),
                   jax.ShapeDtypeStruct(s_out_all.shape, F32)],
        scratch_shapes=[pltpu.VMEM((PAIR, LANES), F32), pltpu.VMEM((PAIR, LANES), F32)],
        input_output_aliases={7: 1},
        compiler_params=_cparams(("parallel", "parallel")),
        name="wkv_step",
    )(r, lw, k, v, a, b, s_all, s_out_all)


def _gmlp_mix(vg, g, b, ws_ref, bias_ref, seq):
    mu = jnp.mean(vg, axis=-1, keepdims=True)
    d = vg - mu
    var = jnp.mean(d * d, axis=-1, keepdims=True)
    vn = d * lax.rsqrt(var + LN_EPS) * g + b
    if not seq:
        return vn * ws_ref[...] + bias_ref[...], vn
    tm, gw = vg.shape[0], vg.shape[1] // GMLP_GROUPS
    ri = lax.broadcasted_iota(jnp.int32, (GMLP_CHUNK, GMLP_CHUNK), 0)
    ci = lax.broadcasted_iota(jnp.int32, (GMLP_CHUNK, GMLP_CHUNK), 1)
    vb = vn.astype(BF16)
    cols = []
    for gi in range(GMLP_GROUPS):
        wg = jnp.where(ci <= ri, ws_ref[gi], 0.0).astype(BF16)
        rows = [_dot(wg, vb[c * GMLP_CHUNK:(c + 1) * GMLP_CHUNK, gi * gw:(gi + 1) * gw])
                for c in range(tm // GMLP_CHUNK)]
        cols.append(jnp.concatenate(rows, axis=0))
    bias = jnp.concatenate([bias_ref[...]] * (tm // GMLP_CHUNK), axis=0)
    return jnp.concatenate(cols, axis=1) + bias, vn


def _mix_out_kernel(*refs, seq, final):
    if seq:
        or_ref = refs[0]
        pos = 1
        o_r = or_ref[...]
    else:
        y_ref, r_ref, k_ref, v_ref, zr_ref, lg_ref, lb_ref, rk_ref = refs[:8]
        pos = 8
        v = v_ref[...]
        o_r = _rwkv_gate(y_ref[...], _bonus(r_ref[...], k_ref[...], v, rk_ref[...]),
                         zr_ref[...], lg_ref[...], lb_ref[...]).astype(BF16)
    u_ref, vg_ref, zg_ref, h_ref, gg_ref, gb_ref, ws_ref, bias_ref, w_ref = refs[pos:pos + 9]
    rest = refs[pos + 9:]
    rw = o_r.shape[1]
    mixed, vn = _gmlp_mix(vg_ref[...], gg_ref[...], gb_ref[...], ws_ref, bias_ref, seq)
    o_g = u_ref[...] * mixed * _silu(zg_ref[...])
    h = h_ref[...] + _dot(o_r, w_ref[:rw, :]) + _dot(o_g.astype(BF16), w_ref[rw:, :])
    if final:
        ms = jnp.mean(h * h, axis=-1, keepdims=True)
        h = h * lax.rsqrt(ms + RMS_EPS) * rest[0][...]
    outs = rest[1:] if final else rest
    outs[0][...] = h
    if not seq:
        outs[1][...] = vn


def _mix_out(branch, p, h, gln_g, gln_b, ws, bias, w, layer, final_g, *, seq, tm, u_blk):
    m, d = h.shape
    gw = gln_g.shape[-1]
    rw = branch[0].shape[1]
    row = pl.BlockSpec((tm, rw), lambda i: (i, 0))
    col = lambda c: pl.BlockSpec((tm, gw), lambda i, c=c: (i, c))
    hrow = pl.BlockSpec((tm, d), lambda i: (i, 0))
    lspec = lambda t: _layer_spec(t, layer)
    if seq:
        in_specs = [row]
        args = list(branch)
    else:
        y, r, k, v, lnx_g, lnx_b, r_k = branch
        in_specs = [row, row, row, row, col(u_blk - 1), lspec(lnx_g), lspec(lnx_b), lspec(r_k)]
        args = [y, r, k, v, p, lnx_g, lnx_b, r_k]
    in_specs += [col(u_blk), col(u_blk + 1), col(u_blk + 2), hrow, lspec(gln_g), lspec(gln_b),
                 lspec(ws), lspec(bias),
                 pl.BlockSpec((rw + gw, d), lambda i: (0, 0), pipeline_mode=pl.Buffered(1))]
    args += [p, p, p, h, gln_g, gln_b, ws, bias, w]
    if final_g is not None:
        in_specs.append(pl.BlockSpec((1, d), lambda i: (0, 0)))
        args.append(final_g)
    out_specs = [hrow]
    out_shape = [jax.ShapeDtypeStruct((m, d), F32)]
    if not seq:
        out_specs.append(pl.BlockSpec((tm, gw), lambda i: (i, 0)))
        out_shape.append(jax.ShapeDtypeStruct((m, gw), F32))
    outs = pl.pallas_call(
        functools.partial(_mix_out_kernel, seq=seq, final=final_g is not None),
        grid=(m // tm,),
        in_specs=in_specs,
        out_specs=out_specs,
        out_shape=out_shape,
        compiler_params=_cparams(("parallel",)),
        name="mix_out",
    )(*args)
    return (outs[0], None) if seq else (outs[0], outs[1])


def _pad_last(x, width):
    return jnp.pad(x, [(0, 0)] * (x.ndim - 1) + [(0, width - x.shape[-1])])


def _pad_rows(x, rows):
    return jnp.pad(x, [(0, 0)] * (x.ndim - 2) + [(0, rows - x.shape[-2]), (0, 0)])


def _pick_tile(m, pref, mult):
    t = min(m, pref)
    while m % t or t % mult:
        t -= mult
    return t


def _trunk(x, s_init, shift_init, prm, weights, *, seq):
    n_seq, seq_len, d = x.shape
    m = n_seq * seq_len
    depth = prm["depth"]
    rw, gw = prm["rw"], prm["gw"]
    lora_w = prm["lora_w"]
    n_heads = rw // HEAD
    tn = 1024
    z_blk = 3

    tm_mm = _pick_tile(m, 1024, 8)
    tm_ew = _pick_tile(m, 512, GMLP_CHUNK if seq else 8)
    assert not seq or seq_len % SEQ_ROWS == 0

    h = x.reshape(m, d)
    s_out, shift_out, vn_out = [], [], []
    prev_rkv = shift_init[:, :, :3 * rw]
    prev_lo = _pad_last(shift_init[:, :, 3 * rw:], LO_W)
    if seq:
        prev_rkv, prev_lo = prev_rkv[:, :, None, :], prev_lo[:, :, None, :]
        s0 = s_init.reshape(depth, n_seq, n_heads // 2, 2, HEAD, HEAD)
        z = jnp.zeros_like(s0[:, :, :, 0])
        s0_bd = jnp.concatenate([jnp.concatenate([s0[:, :, :, 0], z], axis=-1),
                                 jnp.concatenate([z, s0[:, :, :, 1]], axis=-1)], axis=-2)
    else:
        s_init = jnp.transpose(s_init, (0, 2, 3, 4, 1))
        s_buf = jnp.zeros_like(s_init)
    tok = (prm["mu_rkv"],) * 3 + tuple(prm[n] for n in ("mu_lo", "w0", "w2", "a0", "a2",
                                                         "k_k", "k_a"))
    gate = (prm["r_k"], prm["lnx_g"], prm["lnx_b"])
    v_first = None
    w_in_bf, w_out_bf = ([weights], []) if seq else weights
    for l in range(depth):
        p, p_lo = _in_proj(h, prm["norm_g"], w_in_bf[l], l, tm_mm, tn, rw, gw, lora_w)
        vres = (v_first, prm["v0"], prm["v1"], prm["v2"]) if l > 0 else None

        if seq:
            casts = [(prm["w_out"], l)] + ([(prm["w_in"], l + 1)] if l + 1 < depth else [])
            o_r, v, s_bd, cast = _wkv_seq(p, p_lo, prev_rkv, prev_lo, tok, gate, s0_bd, vres, l,
                                          casts, n_seq=n_seq, seq_len=seq_len, rw=rw,
                                          z_blk=z_blk, emit_v=l == 0)
            w_out_bf.append(cast[0])
            w_in_bf += cast[1:]
            s_out.append(s_bd)
            branch = (o_r,)
            ws, bias = prm["w_s"], prm["bias_seq"]
        else:
            r, lw, k, v, a, b = _rwkv_prep(p, p_lo, prev_rkv, prev_lo, tok, vres, l,
                                           tm=tm_ew, rw=rw)
            y, s_buf = _wkv_step(r, lw, k, v, a, b, s_init, s_buf, l)
            branch = (y, r, k, v, gate[1], gate[2], gate[0])
            ws, bias = prm["w_s0"], prm["bias0"]
        if l == 0:
            v_first = v

        last_row = lambda t: t.reshape(n_seq, seq_len, -1)[:, -1]
        shift_out.append(jnp.concatenate(
            [last_row(p)[:, :3 * rw], last_row(p_lo)[:, :lora_w]], axis=-1))

        h, vn = _mix_out(branch, p, h, prm["gln_g"], prm["gln_b"], ws, bias,
                         w_out_bf[l], l, prm["final_g"] if l == depth - 1 else None,
                         seq=seq, tm=tm_ew, u_blk=z_blk + 1)
        if not seq:
            vn_out.append(vn.reshape(n_seq, seq_len, gw))
    if seq:
        s_bd = jnp.stack(s_out)
        s_final = jnp.stack([s_bd[..., :HEAD, :HEAD], s_bd[..., HEAD:, HEAD:]], axis=3)
        s_final = s_final.reshape(depth, n_seq, n_heads, HEAD, HEAD)
    else:
        s_final = jnp.transpose(s_buf, (0, 4, 1, 2, 3))
    return h.reshape(n_seq, seq_len, d), s_final, shift_out, vn_out, (w_in_bf, w_out_bf)


def kernel(x_prompt, x_sample, state_wkv, state_shift, norm_g, w_in, mu_shift, w0, w2, a0, a2,
           k_k, k_a, r_k, lnx_g, lnx_b, v0, v1, v2, gln_g, gln_b, w_s, b_s, w_out, final_g):
    depth, d, _ = w_in.shape
    rw = w0.shape[1]
    gw = gln_g.shape[1]
    dl, il = w2.shape[1], a2.shape[1]
    assert rw % PAIR == 0 and gw == rw and w_s.shape[2] == GMLP_CHUNK
    assert gw // GMLP_GROUPS == LANES and dl + il <= LO_W
    sc = 3 * rw + dl + il

    row3 = lambda t: t[:, None, :]
    prm = dict(
        depth=depth, rw=rw, gw=gw, lora_w=dl + il,
        norm_g=row3(norm_g),
        w_in=jnp.swapaxes(w_in, 1, 2),
        mu_rkv=row3(mu_shift[:, :3 * rw]),
        mu_lo=row3(_pad_last(mu_shift[:, 3 * rw:], LO_W)),
        w0=row3(w0), w2=_pad_rows(w2, LO_W).astype(BF16),
        a0=row3(a0),
        a2=jnp.pad(a2, ((0, 0), (dl, LO_W - dl - il), (0, 0))).astype(BF16),
        k_k=row3(k_k), k_a=row3(k_a), r_k=row3(r_k.reshape(depth, rw)),
        lnx_g=row3(lnx_g), lnx_b=row3(lnx_b),
        v0=row3(v0), v1=_pad_last(v1, LORA_PAD).astype(BF16), v2=_pad_rows(v2, LORA_PAD).astype(BF16),
        gln_g=row3(gln_g), gln_b=row3(gln_b),
        w_s=w_s,
        bias_seq=jnp.repeat(jnp.swapaxes(b_s, 1, 2), gw // GMLP_GROUPS, axis=2),
        w_s0=row3(jnp.repeat(w_s[:, :, 0, 0], gw // GMLP_GROUPS, axis=1)),
        bias0=row3(jnp.repeat(b_s[:, :, 0], gw // GMLP_GROUPS, axis=1)),
        w_out=w_out,
        final_g=final_g[None, :],
    )

    nb = x_prompt.shape[0]
    n_heads = rw // HEAD
    s0_p = jnp.zeros((depth, nb, n_heads, HEAD, HEAD), F32)
    sh0_p = jnp.zeros((depth, nb, sc), F32)
    y_p, s_p, sh_p, _, weights = _trunk(x_prompt, s0_p, sh0_p, prm,
                                        prm["w_in"][0].astype(BF16), seq=True)
    y_s, s_s, sh_s, vn_s, _ = _trunk(x_sample, state_wkv, state_shift, prm, weights, seq=False)
    return (y_p, y_s, s_p, jnp.stack(sh_p), s_s, jnp.stack(sh_s), jnp.stack(vn_s))
```

```python
import functools

import jax
import jax.numpy as jnp
from jax import lax
from jax.experimental import pallas as pl
from jax.experimental.pallas import tpu as pltpu

F32 = jnp.float32
BF16 = jnp.bfloat16

HEAD = 64
LANES = 128
PAIR = 2 * HEAD
WKV_CHUNK = HEAD
SEQ_ROWS = 2 * WKV_CHUNK
GMLP_CHUNK = 128
GMLP_GROUPS = 8
LORA_PAD = 128
LO_W = 256
RMS_EPS = 1e-6
LN_EPS = 1e-5
LNX_EPS = 64e-5
VMEM_LIMIT = 56 * 1024 * 1024

_NT = (((1,), (1,)), ((), ()))
_TN = (((0,), (0,)), ((), ()))


def _dot(a, b):
    return jnp.dot(a, b, preferred_element_type=F32)


def _dot_nt(a, b):
    return lax.dot_general(a, b, _NT, preferred_element_type=F32)


def _dot_tn(a, b):
    return lax.dot_general(a, b, _TN, preferred_element_type=F32)


def _cparams(sem):
    return pltpu.CompilerParams(dimension_semantics=sem, vmem_limit_bytes=VMEM_LIMIT)


def _silu(z):
    return z * jax.nn.sigmoid(z)


def _head_ones():
    r = lax.broadcasted_iota(jnp.int32, (LANES, LANES), 0) // HEAD
    c = lax.broadcasted_iota(jnp.int32, (LANES, LANES), 1) // HEAD
    return jnp.where(r == c, 1.0, 0.0).astype(BF16)


def _split_hi_lo(x):
    hi = x.astype(BF16)
    return hi, (x - hi.astype(F32)).astype(BF16)


def _dot_hi_lo(hi_lo, mat, mat_left):
    hi, lo = hi_lo
    return (_dot(mat, hi) + _dot(mat, lo)) if mat_left else (_dot(hi, mat) + _dot(lo, mat))


def _head_sum_hi_lo(hi_lo, ones_bd):
    hi, lo = hi_lo
    outs = [_dot_hi_lo((hi[:, j * LANES:(j + 1) * LANES], lo[:, j * LANES:(j + 1) * LANES]),
                       ones_bd, False) for j in range(hi.shape[1] // LANES)]
    return jnp.concatenate(outs, axis=1)


def _head_sum(x, ones_bd):
    return _head_sum_hi_lo(_split_hi_lo(x), ones_bd)


def _chunk_tri(rows):
    ti = lax.broadcasted_iota(jnp.int32, (rows, rows), 0)
    si = lax.broadcasted_iota(jnp.int32, (rows, rows), 1)
    same_chunk = si // WKV_CHUNK == ti // WKV_CHUNK
    return jnp.where((si <= ti) & same_chunk, 1.0, 0.0).astype(BF16)


def _decay_and_iclr(wraw, araw):
    w = -jax.nn.softplus(-wraw) - 0.5
    return -jnp.exp(w), jax.nn.sigmoid(araw)


_NORM_ROWS = 256


def _in_proj_kernel(h_ref, g_ref, w_ref, wlo_ref, o_ref, lo_ref, xn_ref):
    @pl.when(pl.program_id(1) == 0)
    def _():
        g = g_ref[...]
        chunk = min(_NORM_ROWS, h_ref.shape[0])

        def norm_rows(c, carry):
            rows = pl.ds(pl.multiple_of(c * chunk, chunk), chunk)
            x = h_ref[rows, :]
            ms = jnp.mean(x * x, axis=-1, keepdims=True)
            xn_ref[rows, :] = (x * lax.rsqrt(ms + RMS_EPS) * g).astype(BF16)
            return carry

        lax.fori_loop(0, h_ref.shape[0] // chunk, norm_rows, 0)
        lo_ref[...] = _dot_nt(xn_ref[...], wlo_ref[...])

    o_ref[...] = _dot_nt(xn_ref[...], w_ref[...])


_ROW_ALIGN = 16


def _in_proj(h, g, wt, layer, tm, tn, rw, gw, lora_w):
    m, d = h.shape
    in_cols = 3 * rw + lora_w + rw + 3 * gw
    n_rkv = 3 * rw // tn
    n_main = n_rkv + (rw + 3 * gw) // tn
    assert (3 * rw) % tn == 0 and (rw + 3 * gw) % tn == 0 and lora_w <= LO_W
    assert in_cols % _ROW_ALIGN == 0 and tn % _ROW_ALIGN == 0 and lora_w % _ROW_ALIGN == 0
    assert wt.shape == (in_cols, d)
    step = tn // _ROW_ALIGN
    z0 = (3 * rw + lora_w) // _ROW_ALIGN
    lo0 = 3 * rw // _ROW_ALIGN

    def w_index(i, j):
        start = jnp.where(j < n_rkv, step * j, z0 + step * (j - n_rkv))
        return (start * _ROW_ALIGN, 0)

    return pl.pallas_call(
        _in_proj_kernel,
        grid=(m // tm, n_main),
        in_specs=[
            pl.BlockSpec((tm, d), lambda i, j: (i, 0)),
            _layer_spec(g, layer),
            pl.BlockSpec((pl.Element(tn), pl.Element(d)), w_index),
            pl.BlockSpec((pl.Element(LO_W), pl.Element(d)),
                         lambda i, j: (lo0 * _ROW_ALIGN, 0)),
        ],
        out_specs=[pl.BlockSpec((tm, tn), lambda i, j: (i, j)),
                   pl.BlockSpec((tm, LO_W), lambda i, j: (i, 0))],
        out_shape=[jax.ShapeDtypeStruct((m, n_main * tn), F32),
                   jax.ShapeDtypeStruct((m, LO_W), F32)],
        scratch_shapes=[pltpu.VMEM((tm, d), BF16)],
        compiler_params=_cparams(("parallel", "arbitrary")),
        name="in_proj",
    )(h, g, wt, wt)


def _rwkv_tokens(k, v, tanh_lo, lo, w0, w2, a0, a2, k_k, k_a, vres):
    lw, a = _decay_and_iclr(w0 + _dot(tanh_lo, w2[...]), a0 + _dot(lo, a2[...]))
    if vres is not None:
        v_first, v0, vv_mid, v2 = vres
        v = v + (v_first - v) * jax.nn.sigmoid(v0 + _dot(vv_mid, v2[...]))
    kk = k * k_k
    kk = kk / jnp.maximum(jnp.sqrt(_head_sum(kk * kk, _head_ones())), 1e-12)
    return lw, k * (1.0 + (a - 1.0) * k_a), v, -kk, kk * a


def _rwkv_gate(y, bonus, z, g, b):
    ones_bd = _head_ones()
    mu = _head_sum(y, ones_bd) * (1.0 / HEAD)
    d = y - mu
    var = _head_sum(d * d, ones_bd) * (1.0 / HEAD)
    return (d * lax.rsqrt(var + LNX_EPS) * g + b + bonus) * _silu(z)


def _bonus(r, k, v, rk):
    return _head_sum(r * k * rk, _head_ones()) * v


def _rwkv_prep_kernel(*refs, has_vres):
    (r_ref, k_ref, v_ref, lo_ref, pr_ref, pk_ref, pv_ref, plo_ref,
     mur_ref, muk_ref, muv_ref, mulo_ref, w0_ref, w2_ref, a0_ref, a2_ref,
     kk_ref, ka_ref) = refs[:18]
    pos = 18
    if has_vres:
        vf_ref, v0_ref, v1_ref, v2_ref = refs[pos:pos + 4]
        pos += 4
    ro_ref, lwo_ref, ko_ref, vo_ref, ao_ref, bo_ref = refs[pos:pos + 6]

    def mix(x_ref, p_ref, mu_ref):
        x = x_ref[...]
        return x + (p_ref[...] - x) * mu_ref[...]

    lo = mix(lo_ref, plo_ref, mulo_ref)
    v = mix(v_ref, pv_ref, muv_ref)
    vres = None
    if has_vres:
        vres = (vf_ref[...], v0_ref[...], _dot(v.astype(BF16), v1_ref[...]).astype(BF16), v2_ref)
    lw, k, v, a, b = _rwkv_tokens(
        mix(k_ref, pk_ref, muk_ref), v, jnp.tanh(lo).astype(BF16), lo.astype(BF16),
        w0_ref[...], w2_ref, a0_ref[...], a2_ref, kk_ref[...], ka_ref[...], vres)
    ro_ref[...] = mix(r_ref, pr_ref, mur_ref)
    lwo_ref[...] = lw
    ko_ref[...] = k
    vo_ref[...] = v
    ao_ref[...] = a
    bo_ref[...] = b


def _layer_spec(arr, layer):
    zeros = (0,) * (arr.ndim - 1)
    return pl.BlockSpec((None,) + arr.shape[1:], lambda *g: (layer,) + zeros)


def _token_param_specs(tok, rw, layer):
    return ([pl.BlockSpec((None, 1, rw), lambda *g, c=c: (layer, 0, c)) for c in range(3)]
            + [_layer_spec(t, layer) for t in tok[3:]])


def _vres_specs(vres, rows, rw, row_index, layer):
    return ([pl.BlockSpec((rows, rw), row_index)]
            + [_layer_spec(t, layer - 1) for t in vres[1:]])


def _rwkv_prep(p, p_lo, prev_rkv, prev_lo, tok, vres, layer, *, tm, rw):
    m = p.shape[0]
    row = lambda i: (i, 0)
    col = lambda c: (lambda i: (i, c))
    in_specs = [pl.BlockSpec((tm, rw), col(0)), pl.BlockSpec((tm, rw), col(1)),
                pl.BlockSpec((tm, rw), col(2)), pl.BlockSpec((tm, LO_W), row)]
    in_specs += [pl.BlockSpec((None, tm, rw), lambda i, c=c: (layer, i, c)) for c in range(3)]
    in_specs.append(pl.BlockSpec((None, tm, LO_W), lambda i: (layer, i, 0)))
    args = [p, p, p, p_lo, prev_rkv, prev_rkv, prev_rkv, prev_lo]
    in_specs += _token_param_specs(tok, rw, layer)
    args += list(tok)
    if vres is not None:
        in_specs += _vres_specs(vres, tm, rw, row, layer)
        args += list(vres)
    out = jax.ShapeDtypeStruct((m, rw), F32)
    return pl.pallas_call(
        functools.partial(_rwkv_prep_kernel, has_vres=vres is not None),
        grid=(m // tm,),
        in_specs=in_specs,
        out_specs=[pl.BlockSpec((tm, rw), row)] * 6,
        out_shape=[out] * 6,
        compiler_params=_cparams(("parallel",)),
        name="rwkv_prep",
    )(*args)


def _shift_seq(x, mu, init_row, carry_ref, sl, first):
    prev0 = jnp.where(first, init_row, carry_ref[:, sl])
    sh = pltpu.roll(x, 1, 0)
    rows = lax.broadcasted_iota(jnp.int32, x.shape, 0)
    sh = jnp.where(rows == 0, prev0, sh)
    carry_ref[:, sl] = x[x.shape[0] - 1:, :]
    return x + (sh - x) * mu


_EMPTY_FILL_SLOTS = (2, 4, 6, 8)

_CHAIN_OPERANDS = ("at", "rt", "btl", "bth", "ktl", "kth", "vl", "vh", "be", "ke")


def _chain_operands(r, lw, lp, k, v, a, b):
    rows, width = r.shape
    c_len = WKV_CHUNK
    n_sub = rows // c_len
    p_ends = [jnp.exp(lp[(ci + 1) * c_len - 1:(ci + 1) * c_len]) for ci in range(n_sub)]
    pinv = jnp.exp(-lp)
    pend = pinv * jnp.concatenate(
        [jnp.broadcast_to(e, (c_len, width)) for e in p_ends], axis=0)
    first_head = lax.broadcasted_iota(jnp.int32, (rows, width), 1) < HEAD
    halves = lambda z: (jnp.where(first_head, z, 0.0), jnp.where(first_head, 0.0, z))
    ops = ((a * jnp.exp(lp - lw), r * jnp.exp(lp)) + halves(b * pinv) + halves(k * pinv)
           + halves(v) + (b * pend, k * pend))
    return [o.astype(BF16) for o in ops], jnp.concatenate(p_ends, axis=0)


def _wkv_chain(ops, plast_ref, rd, s_ref, fill):
    _, rows, width = ops["at"].shape
    c_len = WKV_CHUNK
    n_sub = rows // c_len
    n_pairs = width // PAIR

    lane = lax.broadcasted_iota(jnp.int32, (c_len, PAIR), 1)
    rowi = lax.broadcasted_iota(jnp.int32, (c_len, PAIR), 0)
    lo_half = lane < HEAD
    scol = lane % HEAD
    strict = scol < rowi
    incl = scol <= rowi
    r2 = lax.broadcasted_iota(jnp.int32, (PAIR, PAIR), 0)
    c2 = lax.broadcasted_iota(jnp.int32, (PAIR, PAIR), 1)
    same_head = (r2 // HEAD) == (c2 // HEAD)
    eye_pair = jnp.where(scol == rowi, 1.0, 0.0).astype(F32)

    def bd(z):
        return jnp.concatenate([jnp.where(lo_half, z, 0.0), jnp.where(lo_half, 0.0, z)], axis=0)

    chains = [(ci, pi) for ci in range(n_sub) for pi in range(n_pairs)]
    rs = {c: slice(c[0] * c_len, (c[0] + 1) * c_len) for c in chains}
    ls = {c: slice(c[1] * PAIR, (c[1] + 1) * PAIR) for c in chains}
    ld = lambda name, c: ops[name][rd, rs[c], ls[c]]
    x, g = {}, {}
    for c in chains:
        x[c] = jnp.concatenate([ld("at", c), ld("rt", c)], axis=0)
        rhs = jnp.concatenate([ld("btl", c), ld("bth", c), ld("ktl", c), ld("kth", c)], axis=0)
        g[c] = _dot_nt(x[c], rhs)
    fill()
    q_b = {c: jnp.where(incl, g[c][c_len:, :PAIR], 0.0).astype(BF16) for c in chains}
    mv = {}
    for c in chains:
        gk = g[c][:, PAIR:]
        gkm = jnp.concatenate([jnp.where(strict, gk[:c_len], 0.0),
                               jnp.where(incl, gk[c_len:], 0.0)], axis=0)
        mv[c] = _dot(gkm.astype(BF16), jnp.concatenate([ld("vl", c), ld("vh", c)], axis=0))
    fill()

    pw = {c: jnp.where(strict, g[c][:c_len, :PAIR], 0.0) for c in chains}
    tinv = {c: eye_pair + pw[c] for c in chains}
    pw = {c: _dot(pw[c].astype(BF16), bd(pw[c]).astype(BF16)) for c in chains}
    fill()
    n_sq = (c_len - 1).bit_length() - 1
    for j in range(n_sq):
        for c in chains:
            pb = pw[c].astype(BF16)
            if j == n_sq - 1:
                tinv[c] = tinv[c] + _dot(pb, bd(tinv[c]).astype(BF16))
            else:
                res = _dot(pb, jnp.concatenate([bd(tinv[c]), bd(pw[c])], axis=1).astype(BF16))
                tinv[c] = tinv[c] + res[:, :PAIR]
                pw[c] = res[:, PAIR:]
        fill()

    s_cur = [s_ref[pi] for pi in range(n_pairs)]
    y_rows = []
    for ci in range(n_sub):
        cs = [(ci, pi) for pi in range(n_pairs)]
        xs = {c: _dot_nt(x[c], s_cur[c[1]].astype(BF16)) for c in cs}
        fill()
        u = {c: _dot(tinv[c].astype(BF16), bd(xs[c][:c_len] + mv[c][:c_len]).astype(BF16))
             for c in cs}
        fill()
        y_cols = []
        for c in cs:
            y_cols.append(xs[c][c_len:] + mv[c][c_len:] + _dot(q_b[c], bd(u[c]).astype(BF16)))
            uv = jnp.concatenate([u[c].astype(BF16), ld("vl", c) + ld("vh", c)], axis=0)
            bk = jnp.concatenate([ld("be", c), ld("ke", c)], axis=0)
            p_last = plast_ref[rd, ci:ci + 1, ls[c]]
            s_cur[c[1]] = s_cur[c[1]] * p_last + jnp.where(same_head, _dot_tn(uv, bk), 0.0)
        y_rows.append(jnp.concatenate(y_cols, axis=1))
        fill()
    for pi in range(n_pairs):
        s_ref[pi] = s_cur[pi]
    return jnp.concatenate(y_rows, axis=0)


def _wkv_seq_kernel(*refs, has_vres, emit_v, blocks_per_seq, cast_steps, zero_steps):
    (r_ref, k_ref, v_ref, lo_ref, z_ref, pr_ref, pk_ref, pv_ref, plo_ref,
     mur_ref, muk_ref, muv_ref, mulo_ref, w0_ref, w2_ref, a0_ref, a2_ref,
     kk_ref, ka_ref, rk_ref, lg_ref, lb_ref, s0_ref) = refs[:23]
    pos = 23
    if has_vres:
        vf_ref, v0_ref, v1_ref, v2_ref = refs[pos:pos + 4]
        pos += 4
    n_cast = len(cast_steps)
    cast_in = refs[pos:pos + n_cast]
    pos += n_cast
    o_ref = refs[pos]
    pos += 1
    if emit_v:
        vo_ref = refs[pos]
        pos += 1
    s_ref = refs[pos]
    cast_out = refs[pos + 1:pos + 1 + n_cast]
    pos += 1 + n_cast
    if zero_steps:
        zero_ref = refs[pos]
        pos += 1
    cr_ref, ck_ref, cv_ref, clo_ref = refs[pos:pos + 4]
    pos += 4
    ops = dict(zip(_CHAIN_OPERANDS, refs[pos:pos + len(_CHAIN_OPERANDS)]))
    vs_ref, bonus_ref, plast_ref = refs[pos + len(_CHAIN_OPERANDS):]

    j = pl.program_id(0)
    first = (j % blocks_per_seq) == 0
    wr = j % 2
    rd = 1 - wr

    @pl.when(j == 0)
    def _():
        for ref in list(ops.values()) + [vs_ref, bonus_ref]:
            ref[1] = jnp.zeros(ref.shape[1:], ref.dtype)
        plast_ref[1] = jnp.ones(plast_ref.shape[1:], F32)

    @pl.when((jnp.maximum(j - 1, 0) % blocks_per_seq) == 0)
    def _():
        s_ref[...] = s0_ref[...]

    rows, width = vs_ref.shape[1:]
    n_sub = rows // WKV_CHUNK
    n_lane_blocks = width // LANES
    ones_bd = _head_ones()
    tri = _chunk_tri(rows)
    shared, mid = {}, {}

    def shared_vector_part():
        lo = _shift_seq(lo_ref[...], mulo_ref[...], plo_ref[...], clo_ref, slice(None), first)
        shared["tanh_lo"] = jnp.tanh(lo).astype(BF16)
        shared["lo"] = lo.astype(BF16)
        shared["v"] = _shift_seq(v_ref[...], muv_ref[...], pv_ref[...], cv_ref, slice(None), first)

    def shared_matmul_part():
        shared["w_lora"] = _dot(shared["tanh_lo"], w2_ref[...])
        shared["a_lora"] = _dot(shared["lo"], a2_ref[...])
        if has_vres:
            shared["vv_mid"] = _dot(shared["v"].astype(BF16), v1_ref[...])

    def shared_matmul_part2():
        if has_vres:
            shared["vv"] = _dot(shared["vv_mid"].astype(BF16), v2_ref[...])

    def block_vector_part(cb):
        sl = slice(cb * LANES, (cb + 1) * LANES)
        r = _shift_seq(r_ref[:, sl], mur_ref[:, sl], pr_ref[:, sl], cr_ref, sl, first)
        k = _shift_seq(k_ref[:, sl], muk_ref[:, sl], pk_ref[:, sl], ck_ref, sl, first)
        v = shared["v"][:, sl]
        if has_vres:
            v = v + (vf_ref[:, sl] - v) * jax.nn.sigmoid(v0_ref[:, sl] + shared["vv"][:, sl])
        lw, a = _decay_and_iclr(w0_ref[:, sl] + shared["w_lora"][:, sl],
                                a0_ref[:, sl] + shared["a_lora"][:, sl])
        kk = k * kk_ref[:, sl]
        k = k * (1.0 + (a - 1.0) * ka_ref[:, sl])
        mid[cb] = dict(r=r, k=k, v=v, a=a, lw=lw, kk=kk, kk2=_split_hi_lo(kk * kk),
                       rk=_split_hi_lo(r * k * rk_ref[:, sl]), lws=_split_hi_lo(lw))

    def block_matmul_part(cb):
        sl = slice(cb * LANES, (cb + 1) * LANES)
        m = mid.pop(cb)
        kk = m["kk"] / jnp.maximum(jnp.sqrt(_dot_hi_lo(m["kk2"], ones_bd, False)), 1e-12)
        bonus = _dot_hi_lo(m["rk"], ones_bd, False) * m["v"]
        lp = _dot_hi_lo(m["lws"], tri, True)
        vals, p_last = _chain_operands(m["r"], m["lw"], lp, m["k"], m["v"], -kk, kk * m["a"])
        for name, val in zip(_CHAIN_OPERANDS, vals):
            ops[name][wr, :, sl] = val
        vs_ref[wr, :, sl] = m["v"]
        bonus_ref[wr, :, sl] = bonus
        plast_ref[wr, :n_sub, sl] = p_last

    def piece(t):
        def run():
            if 0 < t <= n_lane_blocks:
                block_matmul_part(t - 1)
            if t < n_lane_blocks:
                block_vector_part(t)
        return run

    shared_vector_part()
    shared_matmul_part()
    pending = [shared_matmul_part2] + [piece(t) for t in range(n_lane_blocks + 1)]

    slot = [0]

    def fill():
        slot[0] += 1
        if pending and slot[0] not in _EMPTY_FILL_SLOTS:
            pending.pop(0)()

    y = _wkv_chain(ops, plast_ref, rd, s_ref, fill)
    while pending:
        fill()
    o_ref[...] = _rwkv_gate(y, bonus_ref[rd], z_ref[...], lg_ref[...], lb_ref[...]
                            ).astype(o_ref.dtype)
    if emit_v:
        vo_ref[...] = vs_ref[rd]
    for src_ref, dst_ref, steps in zip(cast_in, cast_out, cast_steps):
        @pl.when(j < steps)
        def _(src_ref=src_ref, dst_ref=dst_ref):
            dst_ref[...] = src_ref[...].astype(dst_ref.dtype)
    if zero_steps:
        @pl.when(j < zero_steps)
        def _():
            zero_ref[...] = jnp.zeros_like(zero_ref)


def _wkv_seq(p, p_lo, prev_rkv, prev_lo, tok, gate, s0_bd, vres, layer, casts, zero_fill, *,
             n_seq, seq_len, rw, z_blk, emit_v):
    m = p.shape[0]
    n_blocks = seq_len // SEQ_ROWS
    total = n_seq * n_blocks
    n_pairs = rw // PAIR
    prep_blk = lambda c: jnp.minimum(c, total - 1)
    run_blk = lambda c: jnp.maximum(c - 1, 0)
    row = lambda c: (prep_blk(c), 0)
    col = lambda j: (lambda c: (prep_blk(c), j))
    blk = lambda j: pl.BlockSpec((SEQ_ROWS, rw), col(j))
    in_specs = [blk(0), blk(1), blk(2), pl.BlockSpec((SEQ_ROWS, LO_W), row),
                pl.BlockSpec((SEQ_ROWS, rw), lambda c: (run_blk(c), z_blk))]
    args = [p, p, p, p_lo, p]
    for j in range(3):
        in_specs.append(pl.BlockSpec((None, None, 1, rw),
                                     lambda c, j=j: (layer, prep_blk(c) // n_blocks, 0, j)))
    in_specs.append(pl.BlockSpec((None, None, 1, LO_W),
                                 lambda c: (layer, prep_blk(c) // n_blocks, 0, 0)))
    args += [prev_rkv, prev_rkv, prev_rkv, prev_lo]
    in_specs += _token_param_specs(tok, rw, layer)
    args += list(tok)
    in_specs += [_layer_spec(t, layer) for t in gate]
    args += list(gate)
    in_specs.append(pl.BlockSpec((None, None, n_pairs, PAIR, PAIR),
                                 lambda c: (layer, run_blk(c) // n_blocks, 0, 0, 0)))
    args.append(s0_bd)
    if vres is not None:
        in_specs += _vres_specs(vres, SEQ_ROWS, rw, row, layer)
        args += list(vres)
    cast_steps, cast_specs, cast_shapes = [], [], []
    for w_all, w_layer in casts:
        n_rows, n_cols = w_all.shape[1:]
        rows = next(r for r in range(_ROW_ALIGN, n_rows + 1, _ROW_ALIGN)
                    if n_rows % r == 0 and n_rows // r <= total + 1)
        steps = n_rows // rows
        in_specs.append(pl.BlockSpec((None, rows, n_cols), lambda c, w_layer=w_layer, steps=steps:
                                     (w_layer, jnp.minimum(c, steps - 1), 0)))
        args.append(w_all)
        cast_specs.append(pl.BlockSpec((rows, n_cols), lambda c, steps=steps:
                                       (jnp.minimum(c, steps - 1), 0)))
        cast_shapes.append(jax.ShapeDtypeStruct((n_rows, n_cols), BF16))
        cast_steps.append(steps)
    run_rows = pl.BlockSpec((SEQ_ROWS, rw), lambda c: (run_blk(c), 0))
    out_specs = [run_rows]
    out_shape = [jax.ShapeDtypeStruct((m, rw), BF16)]
    if emit_v:
        out_specs.append(run_rows)
        out_shape.append(jax.ShapeDtypeStruct((m, rw), F32))
    out_specs.append(pl.BlockSpec((None, n_pairs, PAIR, PAIR),
                                  lambda c: (run_blk(c) // n_blocks, 0, 0, 0)))
    out_shape.append(jax.ShapeDtypeStruct((n_seq, n_pairs, PAIR, PAIR), F32))
    out_specs += cast_specs
    out_shape += cast_shapes
    zero_steps = 0
    if zero_fill is not None:
        n_inner = zero_fill[1]
        zero_steps = zero_fill[0] * n_inner
        assert zero_steps <= total + 1
        zblk = lambda c: jnp.minimum(c, zero_steps - 1)
        out_specs.append(pl.BlockSpec((None, None) + tuple(zero_fill[2:]),
                                      lambda c: (zblk(c) // n_inner, zblk(c) % n_inner, 0, 0, 0)))
        out_shape.append(jax.ShapeDtypeStruct(tuple(zero_fill), F32))
    outs = pl.pallas_call(
        functools.partial(_wkv_seq_kernel, has_vres=vres is not None, emit_v=emit_v,
                          blocks_per_seq=n_blocks, cast_steps=tuple(cast_steps),
                          zero_steps=zero_steps),
        grid=(total + 1,),
        in_specs=in_specs,
        out_specs=out_specs,
        out_shape=out_shape,
        scratch_shapes=([pltpu.VMEM((1, rw), F32)] * 3 + [pltpu.VMEM((1, LO_W), F32)]
                        + [pltpu.VMEM((2, SEQ_ROWS, rw), BF16)] * len(_CHAIN_OPERANDS)
                        + [pltpu.VMEM((2, SEQ_ROWS, rw), F32)] * 2
                        + [pltpu.VMEM((2, 8, rw), F32)]),
        compiler_params=_cparams(("arbitrary",)),
        name="wkv_seq",
    )(*args)
    outs = list(outs)
    o_r = outs.pop(0)
    v = outs.pop(0) if emit_v else None
    return o_r, v, outs[0], outs[1:]


def _wkv_step_kernel(r_ref, lw_ref, k_ref, v_ref, a_ref, b_ref, s_ref, _, y_ref, so_ref,
                     vt_ref, yt_ref):
    r_t, k_t, a_t, b_t = r_ref[...].T, k_ref[...].T, a_ref[...].T, b_ref[...].T
    w_t = jnp.exp(lw_ref[...]).T
    vt_ref[...] = v_ref[...].T
    for hh in range(PAIR // HEAD):
        sl = slice(hh * HEAD, (hh + 1) * HEAD)
        r, k, a, b, w = r_t[sl], k_t[sl], a_t[sl], b_t[sl], w_t[sl]

        def body(vi, carry, hh=hh, r=r, k=k, a=a, b=b, w=w):
            s = s_ref[hh, vi]
            sa = jnp.sum(s * a, axis=0, keepdims=True)
            s_new = s * w + sa * b + vt_ref[pl.ds(hh * HEAD + vi, 1), :] * k
            so_ref[hh, vi] = s_new
            yt_ref[pl.ds(hh * HEAD + vi, 1), :] = jnp.sum(s_new * r, axis=0, keepdims=True)
            return carry

        lax.fori_loop(0, HEAD, body, 0, unroll=8)
    y_ref[...] = yt_ref[...].T


def _wkv_step(r, lw, k, v, a, b, s_all, s_out_all, layer):
    m, rw = r.shape
    hp = PAIR // HEAD
    assert m % LANES == 0
    blk = pl.BlockSpec((LANES, PAIR), lambda pi, j: (j, pi))
    sblk = pl.BlockSpec((None, hp, HEAD, HEAD, LANES), lambda pi, j: (layer, pi, 0, 0, j))
    return pl.pallas_call(
        _wkv_step_kernel,
        grid=(rw // PAIR, m // LANES),
        in_specs=[blk] * 6 + [sblk, pl.BlockSpec(memory_space=pl.ANY)],
        out_specs=[blk, sblk],
        out_shape=[jax.ShapeDtypeStruct((m, rw), F32),
                   jax.ShapeDtypeStruct(s_out_all.shape, F32)],
        scratch_shapes=[pltpu.VMEM((PAIR, LANES), F32), pltpu.VMEM((PAIR, LANES), F32)],
        input_output_aliases={7: 1},
        compiler_params=_cparams(("parallel", "parallel")),
        name="wkv_step",
    )(r, lw, k, v, a, b, s_all, s_out_all)


def _gmlp_mix(vg, g, b, ws_ref, bias_ref, seq):
    mu = jnp.mean(vg, axis=-1, keepdims=True)
    d = vg - mu
    var = jnp.mean(d * d, axis=-1, keepdims=True)
    vn = d * lax.rsqrt(var + LN_EPS) * g + b
    if not seq:
        return vn * ws_ref[...] + bias_ref[...], vn
    tm, gw = vg.shape[0], vg.shape[1] // GMLP_GROUPS
    ri = lax.broadcasted_iota(jnp.int32, (GMLP_CHUNK, GMLP_CHUNK), 0)
    ci = lax.broadcasted_iota(jnp.int32, (GMLP_CHUNK, GMLP_CHUNK), 1)
    vb = vn.astype(BF16)
    cols = []
    for gi in range(GMLP_GROUPS):
        wg = jnp.where(ci <= ri, ws_ref[gi], 0.0).astype(BF16)
        rows = [_dot(wg, vb[c * GMLP_CHUNK:(c + 1) * GMLP_CHUNK, gi * gw:(gi + 1) * gw])
                for c in range(tm // GMLP_CHUNK)]
        cols.append(jnp.concatenate(rows, axis=0))
    bias = jnp.concatenate([bias_ref[...]] * (tm // GMLP_CHUNK), axis=0)
    return jnp.concatenate(cols, axis=1) + bias, vn


def _mix_out_kernel(*refs, seq, final):
    if seq:
        or_ref = refs[0]
        pos = 1
        o_r = or_ref[...]
    else:
        y_ref, r_ref, k_ref, v_ref, zr_ref, lg_ref, lb_ref, rk_ref = refs[:8]
        pos = 8
        v = v_ref[...]
        o_r = _rwkv_gate(y_ref[...], _bonus(r_ref[...], k_ref[...], v, rk_ref[...]),
                         zr_ref[...], lg_ref[...], lb_ref[...]).astype(BF16)
    u_ref, vg_ref, zg_ref, h_ref, gg_ref, gb_ref, ws_ref, bias_ref, w_ref = refs[pos:pos + 9]
    rest = refs[pos + 9:]
    rw = o_r.shape[1]
    mixed, vn = _gmlp_mix(vg_ref[...], gg_ref[...], gb_ref[...], ws_ref, bias_ref, seq)
    o_g = u_ref[...] * mixed * _silu(zg_ref[...])
    h = h_ref[...] + _dot(o_r, w_ref[:rw, :]) + _dot(o_g.astype(BF16), w_ref[rw:, :])
    if final:
        ms = jnp.mean(h * h, axis=-1, keepdims=True)
        h = h * lax.rsqrt(ms + RMS_EPS) * rest[0][...]
    outs = rest[1:] if final else rest
    outs[0][...] = h
    if not seq:
        outs[1][...] = vn


def _mix_out(branch, p, h, gln_g, gln_b, ws, bias, w, layer, final_g, *, seq, tm, u_blk):
    m, d = h.shape
    gw = gln_g.shape[-1]
    rw = branch[0].shape[1]
    row = pl.BlockSpec((tm, rw), lambda i: (i, 0))
    col = lambda c: pl.BlockSpec((tm, gw), lambda i, c=c: (i, c))
    hrow = pl.BlockSpec((tm, d), lambda i: (i, 0))
    lspec = lambda t: _layer_spec(t, layer)
    if seq:
        in_specs = [row]
        args = list(branch)
    else:
        y, r, k, v, lnx_g, lnx_b, r_k = branch
        in_specs = [row, row, row, row, col(u_blk - 1), lspec(lnx_g), lspec(lnx_b), lspec(r_k)]
        args = [y, r, k, v, p, lnx_g, lnx_b, r_k]
    in_specs += [col(u_blk), col(u_blk + 1), col(u_blk + 2), hrow, lspec(gln_g), lspec(gln_b),
                 lspec(ws), lspec(bias),
                 pl.BlockSpec((rw + gw, d), lambda i: (0, 0), pipeline_mode=pl.Buffered(1))]
    args += [p, p, p, h, gln_g, gln_b, ws, bias, w]
    if final_g is not None:
        in_specs.append(pl.BlockSpec((1, d), lambda i: (0, 0)))
        args.append(final_g)
    out_specs = [hrow]
    out_shape = [jax.ShapeDtypeStruct((m, d), F32)]
    if not seq:
        out_specs.append(pl.BlockSpec((tm, gw), lambda i: (i, 0)))
        out_shape.append(jax.ShapeDtypeStruct((m, gw), F32))
    outs = pl.pallas_call(
        functools.partial(_mix_out_kernel, seq=seq, final=final_g is not None),
        grid=(m // tm,),
        in_specs=in_specs,
        out_specs=out_specs,
        out_shape=out_shape,
        compiler_params=_cparams(("parallel",)),
        name="mix_out",
    )(*args)
    return (outs[0], None) if seq else (outs[0], outs[1])


def _pad_last(x, width):
    return jnp.pad(x, [(0, 0)] * (x.ndim - 1) + [(0, width - x.shape[-1])])


def _pad_rows(x, rows):
    return jnp.pad(x, [(0, 0)] * (x.ndim - 2) + [(0, rows - x.shape[-2]), (0, 0)])


def _pick_tile(m, pref, mult):
    t = min(m, pref)
    while m % t or t % mult:
        t -= mult
    return t


def _trunk(x, s_init, shift_init, prm, weights, *, seq, zero_fill=None):
    n_seq, seq_len, d = x.shape
    m = n_seq * seq_len
    depth = prm["depth"]
    rw, gw = prm["rw"], prm["gw"]
    lora_w = prm["lora_w"]
    n_heads = rw // HEAD
    tn = 1024
    z_blk = 3

    tm_mm = _pick_tile(m, 1024, 8)
    tm_ew = _pick_tile(m, 512, GMLP_CHUNK if seq else 8)
    assert not seq or seq_len % SEQ_ROWS == 0

    h = x.reshape(m, d)
    s_out, shift_out, vn_out = [], [], []
    prev_rkv = shift_init[:, :, :3 * rw]
    prev_lo = _pad_last(shift_init[:, :, 3 * rw:], LO_W)
    if seq:
        prev_rkv, prev_lo = prev_rkv[:, :, None, :], prev_lo[:, :, None, :]
        s0 = s_init.reshape(depth, n_seq, n_heads // 2, 2, HEAD, HEAD)
        z = jnp.zeros_like(s0[:, :, :, 0])
        s0_bd = jnp.concatenate([jnp.concatenate([s0[:, :, :, 0], z], axis=-1),
                                 jnp.concatenate([z, s0[:, :, :, 1]], axis=-1)], axis=-2)
    else:
        s_init = jnp.transpose(s_init, (0, 2, 3, 4, 1))
        s_buf = weights[2] if weights[2] is not None else jnp.zeros_like(s_init)
        assert s_buf.shape == s_init.shape
    tok = (prm["mu_rkv"],) * 3 + tuple(prm[n] for n in ("mu_lo", "w0", "w2", "a0", "a2",
                                                         "k_k", "k_a"))
    gate = (prm["r_k"], prm["lnx_g"], prm["lnx_b"])
    v_first = None
    w_in_bf, w_out_bf, zero_buf = ([weights], [], None) if seq else weights
    for l in range(depth):
        p, p_lo = _in_proj(h, prm["norm_g"], w_in_bf[l], l, tm_mm, tn, rw, gw, lora_w)
        vres = (v_first, prm["v0"], prm["v1"], prm["v2"]) if l > 0 else None

        if seq:
            casts = [(prm["w_out"], l)] + ([(prm["w_in"], l + 1)] if l + 1 < depth else [])
            o_r, v, s_bd, side = _wkv_seq(p, p_lo, prev_rkv, prev_lo, tok, gate, s0_bd, vres, l,
                                          casts, zero_fill if l == 0 else None, n_seq=n_seq,
                                          seq_len=seq_len, rw=rw, z_blk=z_blk, emit_v=l == 0)
            w_out_bf.append(side[0])
            w_in_bf += side[1:len(casts)]
            if l == 0 and zero_fill is not None:
                zero_buf = side[-1]
            s_out.append(s_bd)
            branch = (o_r,)
            ws, bias = prm["w_s"], prm["bias_seq"]
        else:
            r, lw, k, v, a, b = _rwkv_prep(p, p_lo, prev_rkv, prev_lo, tok, vres, l,
                                           tm=tm_ew, rw=rw)
            y, s_buf = _wkv_step(r, lw, k, v, a, b, s_init, s_buf, l)
            branch = (y, r, k, v, gate[1], gate[2], gate[0])
            ws, bias = prm["w_s0"], prm["bias0"]
        if l == 0:
            v_first = v

        last_row = lambda t: t.reshape(n_seq, seq_len, -1)[:, -1]
        shift_out.append(jnp.concatenate(
            [last_row(p)[:, :3 * rw], last_row(p_lo)[:, :lora_w]], axis=-1))

        h, vn = _mix_out(branch, p, h, prm["gln_g"], prm["gln_b"], ws, bias,
                         w_out_bf[l], l, prm["final_g"] if l == depth - 1 else None,
                         seq=seq, tm=tm_ew, u_blk=z_blk + 1)
        if not seq:
            vn_out.append(vn.reshape(n_seq, seq_len, gw))
    if seq:
        s_bd = jnp.stack(s_out)
        s_final = jnp.stack([s_bd[..., :HEAD, :HEAD], s_bd[..., HEAD:, HEAD:]], axis=3)
        s_final = s_final.reshape(depth, n_seq, n_heads, HEAD, HEAD)
    else:
        s_final = jnp.transpose(s_buf, (0, 4, 1, 2, 3))
    return (h.reshape(n_seq, seq_len, d), s_final, shift_out, vn_out,
            (w_in_bf, w_out_bf, zero_buf))


def kernel(x_prompt, x_sample, state_wkv, state_shift, norm_g, w_in, mu_shift, w0, w2, a0, a2,
           k_k, k_a, r_k, lnx_g, lnx_b, v0, v1, v2, gln_g, gln_b, w_s, b_s, w_out, final_g):
    depth, d, _ = w_in.shape
    rw = w0.shape[1]
    gw = gln_g.shape[1]
    dl, il = w2.shape[1], a2.shape[1]
    assert rw % PAIR == 0 and gw == rw and w_s.shape[2] == GMLP_CHUNK
    assert gw // GMLP_GROUPS == LANES and dl + il <= LO_W
    sc = 3 * rw + dl + il

    row3 = lambda t: t[:, None, :]
    prm = dict(
        depth=depth, rw=rw, gw=gw, lora_w=dl + il,
        norm_g=row3(norm_g),
        w_in=jnp.swapaxes(w_in, 1, 2),
        mu_rkv=row3(mu_shift[:, :3 * rw]),
        mu_lo=row3(_pad_last(mu_shift[:, 3 * rw:], LO_W)),
        w0=row3(w0), w2=_pad_rows(w2, LO_W).astype(BF16),
        a0=row3(a0),
        a2=jnp.pad(a2, ((0, 0), (dl, LO_W - dl - il), (0, 0))).astype(BF16),
        k_k=row3(k_k), k_a=row3(k_a), r_k=row3(r_k.reshape(depth, rw)),
        lnx_g=row3(lnx_g), lnx_b=row3(lnx_b),
        v0=row3(v0), v1=_pad_last(v1, LORA_PAD).astype(BF16), v2=_pad_rows(v2, LORA_PAD).astype(BF16),
        gln_g=row3(gln_g), gln_b=row3(gln_b),
        w_s=w_s,
        bias_seq=jnp.repeat(jnp.swapaxes(b_s, 1, 2), gw // GMLP_GROUPS, axis=2),
        w_s0=row3(jnp.repeat(w_s[:, :, 0, 0], gw // GMLP_GROUPS, axis=1)),
        bias0=row3(jnp.repeat(b_s[:, :, 0], gw // GMLP_GROUPS, axis=1)),
        w_out=w_out,
        final_g=final_g[None, :],
    )

    nb = x_prompt.shape[0]
    n_heads = rw // HEAD
    s0_p = jnp.zeros((depth, nb, n_heads, HEAD, HEAD), F32)
    sh0_p = jnp.zeros((depth, nb, sc), F32)
    state_buf = (depth, n_heads, HEAD, HEAD, x_sample.shape[0] * x_sample.shape[1])
    seq_steps = nb * (x_prompt.shape[1] // SEQ_ROWS) + 1
    y_p, s_p, sh_p, _, weights = _trunk(
        x_prompt, s0_p, sh0_p, prm, prm["w_in"][0].astype(BF16), seq=True,
        zero_fill=state_buf if depth * n_heads <= seq_steps else None)
    y_s, s_s, sh_s, vn_s, _ = _trunk(x_sample, state_wkv, state_shift, prm, weights, seq=False)
    return (y_p, y_s, s_p, jnp.stack(sh_p), s_s, jnp.stack(sh_s), jnp.stack(vn_s))
```

```python
import functools

import jax
import jax.numpy as jnp
from jax import lax
from jax.experimental import pallas as pl
from jax.experimental.pallas import tpu as pltpu

F32 = jnp.float32
BF16 = jnp.bfloat16

HEAD = 64
LANES = 128
PAIR = 2 * HEAD
WKV_CHUNK = HEAD
SEQ_ROWS = 2 * WKV_CHUNK
GMLP_CHUNK = 128
GMLP_GROUPS = 8
LORA_PAD = 128
LO_W = 256
RMS_EPS = 1e-6
LN_EPS = 1e-5
LNX_EPS = 64e-5
VMEM_LIMIT = 56 * 1024 * 1024

_NT = (((1,), (1,)), ((), ()))
_TN = (((0,), (0,)), ((), ()))


def _dot(a, b):
    return jnp.dot(a, b, preferred_element_type=F32)


def _dot_nt(a, b):
    return lax.dot_general(a, b, _NT, preferred_element_type=F32)


def _dot_tn(a, b):
    return lax.dot_general(a, b, _TN, preferred_element_type=F32)


def _cparams(sem):
    return pltpu.CompilerParams(dimension_semantics=sem, vmem_limit_bytes=VMEM_LIMIT)


def _silu(z):
    return z * jax.nn.sigmoid(z)


def _head_ones():
    r = lax.broadcasted_iota(jnp.int32, (LANES, LANES), 0) // HEAD
    c = lax.broadcasted_iota(jnp.int32, (LANES, LANES), 1) // HEAD
    return jnp.where(r == c, 1.0, 0.0).astype(BF16)


def _split_hi_lo(x):
    hi = x.astype(BF16)
    return hi, (x - hi.astype(F32)).astype(BF16)


def _dot_hi_lo(hi_lo, mat, mat_left):
    hi, lo = hi_lo
    return (_dot(mat, hi) + _dot(mat, lo)) if mat_left else (_dot(hi, mat) + _dot(lo, mat))


def _head_sum_hi_lo(hi_lo, ones_bd):
    hi, lo = hi_lo
    outs = [_dot_hi_lo((hi[:, j * LANES:(j + 1) * LANES], lo[:, j * LANES:(j + 1) * LANES]),
                       ones_bd, False) for j in range(hi.shape[1] // LANES)]
    return jnp.concatenate(outs, axis=1)


def _head_sum(x, ones_bd):
    return _head_sum_hi_lo(_split_hi_lo(x), ones_bd)


def _chunk_tri(rows):
    ti = lax.broadcasted_iota(jnp.int32, (rows, rows), 0)
    si = lax.broadcasted_iota(jnp.int32, (rows, rows), 1)
    same_chunk = si // WKV_CHUNK == ti // WKV_CHUNK
    return jnp.where((si <= ti) & same_chunk, 1.0, 0.0).astype(BF16)


def _decay_and_iclr(wraw, araw):
    w = -jax.nn.softplus(-wraw) - 0.5
    return -jnp.exp(w), jax.nn.sigmoid(araw)


_NORM_ROWS = 256


def _in_proj_kernel(h_ref, g_ref, w_ref, wlo_ref, o_ref, lo_ref, xn_ref):
    @pl.when(pl.program_id(1) == 0)
    def _():
        g = g_ref[...]
        chunk = min(_NORM_ROWS, h_ref.shape[0])

        def norm_rows(c, carry):
            rows = pl.ds(pl.multiple_of(c * chunk, chunk), chunk)
            x = h_ref[rows, :]
            ms = jnp.mean(x * x, axis=-1, keepdims=True)
            xn_ref[rows, :] = (x * lax.rsqrt(ms + RMS_EPS) * g).astype(BF16)
            return carry

        lax.fori_loop(0, h_ref.shape[0] // chunk, norm_rows, 0)
        lo_ref[...] = _dot_nt(xn_ref[...], wlo_ref[...])

    o_ref[...] = _dot_nt(xn_ref[...], w_ref[...])


_ROW_ALIGN = 16


def _in_proj(h, g, wt, layer, tm, tn, rw, gw, lora_w):
    m, d = h.shape
    in_cols = 3 * rw + lora_w + rw + 3 * gw
    n_rkv = 3 * rw // tn
    n_main = n_rkv + (rw + 3 * gw) // tn
    assert (3 * rw) % tn == 0 and (rw + 3 * gw) % tn == 0 and lora_w <= LO_W
    assert in_cols % _ROW_ALIGN == 0 and tn % _ROW_ALIGN == 0 and lora_w % _ROW_ALIGN == 0
    assert wt.shape == (in_cols, d)
    step = tn // _ROW_ALIGN
    z0 = (3 * rw + lora_w) // _ROW_ALIGN
    lo0 = 3 * rw // _ROW_ALIGN

    def w_index(i, j):
        start = jnp.where(j < n_rkv, step * j, z0 + step * (j - n_rkv))
        return (start * _ROW_ALIGN, 0)

    return pl.pallas_call(
        _in_proj_kernel,
        grid=(m // tm, n_main),
        in_specs=[
            pl.BlockSpec((tm, d), lambda i, j: (i, 0)),
            _layer_spec(g, layer),
            pl.BlockSpec((pl.Element(tn), pl.Element(d)), w_index),
            pl.BlockSpec((pl.Element(LO_W), pl.Element(d)),
                         lambda i, j: (lo0 * _ROW_ALIGN, 0)),
        ],
        out_specs=[pl.BlockSpec((tm, tn), lambda i, j: (i, j)),
                   pl.BlockSpec((tm, LO_W), lambda i, j: (i, 0))],
        out_shape=[jax.ShapeDtypeStruct((m, n_main * tn), F32),
                   jax.ShapeDtypeStruct((m, LO_W), F32)],
        scratch_shapes=[pltpu.VMEM((tm, d), BF16)],
        compiler_params=_cparams(("parallel", "arbitrary")),
        name="in_proj",
    )(h, g, wt, wt)


def _rwkv_tokens(k, v, tanh_lo, lo, w0, w2, a0, a2, k_k, k_a, vres):
    lw, a = _decay_and_iclr(w0 + _dot(tanh_lo, w2[...]), a0 + _dot(lo, a2[...]))
    if vres is not None:
        v_first, v0, vv_mid, v2 = vres
        v = v + (v_first - v) * jax.nn.sigmoid(v0 + _dot(vv_mid, v2[...]))
    kk = k * k_k
    kk = kk / jnp.maximum(jnp.sqrt(_head_sum(kk * kk, _head_ones())), 1e-12)
    return lw, k * (1.0 + (a - 1.0) * k_a), v, -kk, kk * a


def _rwkv_gate(y, bonus, z, g, b):
    ones_bd = _head_ones()
    mu = _head_sum(y, ones_bd) * (1.0 / HEAD)
    d = y - mu
    var = _head_sum(d * d, ones_bd) * (1.0 / HEAD)
    return (d * lax.rsqrt(var + LNX_EPS) * g + b + bonus) * _silu(z)


def _bonus(r, k, v, rk):
    return _head_sum(r * k * rk, _head_ones()) * v


def _rwkv_prep_kernel(*refs, has_vres):
    (r_ref, k_ref, v_ref, lo_ref, pr_ref, pk_ref, pv_ref, plo_ref,
     mur_ref, muk_ref, muv_ref, mulo_ref, w0_ref, w2_ref, a0_ref, a2_ref,
     kk_ref, ka_ref) = refs[:18]
    pos = 18
    if has_vres:
        vf_ref, v0_ref, v1_ref, v2_ref = refs[pos:pos + 4]
        pos += 4
    ro_ref, lwo_ref, ko_ref, vo_ref, ao_ref, bo_ref = refs[pos:pos + 6]

    def mix(x_ref, p_ref, mu_ref):
        x = x_ref[...]
        return x + (p_ref[...] - x) * mu_ref[...]

    lo = mix(lo_ref, plo_ref, mulo_ref)
    v = mix(v_ref, pv_ref, muv_ref)
    vres = None
    if has_vres:
        vres = (vf_ref[...], v0_ref[...], _dot(v.astype(BF16), v1_ref[...]).astype(BF16), v2_ref)
    lw, k, v, a, b = _rwkv_tokens(
        mix(k_ref, pk_ref, muk_ref), v, jnp.tanh(lo).astype(BF16), lo.astype(BF16),
        w0_ref[...], w2_ref, a0_ref[...], a2_ref, kk_ref[...], ka_ref[...], vres)
    ro_ref[...] = mix(r_ref, pr_ref, mur_ref)
    lwo_ref[...] = lw
    ko_ref[...] = k
    vo_ref[...] = v
    ao_ref[...] = a
    bo_ref[...] = b


def _layer_spec(arr, layer):
    zeros = (0,) * (arr.ndim - 1)
    return pl.BlockSpec((None,) + arr.shape[1:], lambda *g: (layer,) + zeros)


def _token_param_specs(tok, rw, layer):
    return ([pl.BlockSpec((None, 1, rw), lambda *g, c=c: (layer, 0, c)) for c in range(3)]
            + [_layer_spec(t, layer) for t in tok[3:]])


def _vres_specs(vres, rows, rw, row_index, layer):
    return ([pl.BlockSpec((rows, rw), row_index)]
            + [_layer_spec(t, layer - 1) for t in vres[1:]])


def _rwkv_prep(p, p_lo, prev_rkv, prev_lo, tok, vres, layer, *, tm, rw):
    m = p.shape[0]
    row = lambda i: (i, 0)
    col = lambda c: (lambda i: (i, c))
    in_specs = [pl.BlockSpec((tm, rw), col(0)), pl.BlockSpec((tm, rw), col(1)),
                pl.BlockSpec((tm, rw), col(2)), pl.BlockSpec((tm, LO_W), row)]
    in_specs += [pl.BlockSpec((None, tm, rw), lambda i, c=c: (layer, i, c)) for c in range(3)]
    in_specs.append(pl.BlockSpec((None, tm, LO_W), lambda i: (layer, i, 0)))
    args = [p, p, p, p_lo, prev_rkv, prev_rkv, prev_rkv, prev_lo]
    in_specs += _token_param_specs(tok, rw, layer)
    args += list(tok)
    if vres is not None:
        in_specs += _vres_specs(vres, tm, rw, row, layer)
        args += list(vres)
    out = jax.ShapeDtypeStruct((m, rw), F32)
    return pl.pallas_call(
        functools.partial(_rwkv_prep_kernel, has_vres=vres is not None),
        grid=(m // tm,),
        in_specs=in_specs,
        out_specs=[pl.BlockSpec((tm, rw), row)] * 6,
        out_shape=[out] * 6,
        compiler_params=_cparams(("parallel",)),
        name="rwkv_prep",
    )(*args)


def _shift_seq(x, mu, init_row, carry_ref, sl, first):
    prev0 = jnp.where(first, init_row, carry_ref[:, sl])
    sh = pltpu.roll(x, 1, 0)
    rows = lax.broadcasted_iota(jnp.int32, x.shape, 0)
    sh = jnp.where(rows == 0, prev0, sh)
    carry_ref[:, sl] = x[x.shape[0] - 1:, :]
    return x + (sh - x) * mu


_EMPTY_FILL_SLOTS = (2, 4, 6, 8)

_CHAIN_OPERANDS = ("at", "rt", "btl", "bth", "ktl", "kth", "vl", "vh", "be", "ke")


def _chain_operands(r, lw, lp, k, v, a, b):
    rows, width = r.shape
    c_len = WKV_CHUNK
    n_sub = rows // c_len
    p_ends = [jnp.exp(lp[(ci + 1) * c_len - 1:(ci + 1) * c_len]) for ci in range(n_sub)]
    pinv = jnp.exp(-lp)
    pend = pinv * jnp.concatenate(
        [jnp.broadcast_to(e, (c_len, width)) for e in p_ends], axis=0)
    first_head = lax.broadcasted_iota(jnp.int32, (rows, width), 1) < HEAD
    halves = lambda z: (jnp.where(first_head, z, 0.0), jnp.where(first_head, 0.0, z))
    ops = ((a * jnp.exp(lp - lw), r * jnp.exp(lp)) + halves(b * pinv) + halves(k * pinv)
           + halves(v) + (b * pend, k * pend))
    return [o.astype(BF16) for o in ops], jnp.concatenate(p_ends, axis=0)


def _wkv_chain(ops, plast_ref, rd, s_ref, fill):
    _, rows, width = ops["at"].shape
    c_len = WKV_CHUNK
    n_sub = rows // c_len
    n_pairs = width // PAIR

    lane = lax.broadcasted_iota(jnp.int32, (c_len, PAIR), 1)
    rowi = lax.broadcasted_iota(jnp.int32, (c_len, PAIR), 0)
    lo_half = lane < HEAD
    scol = lane % HEAD
    strict = scol < rowi
    incl = scol <= rowi
    r2 = lax.broadcasted_iota(jnp.int32, (PAIR, PAIR), 0)
    c2 = lax.broadcasted_iota(jnp.int32, (PAIR, PAIR), 1)
    same_head = (r2 // HEAD) == (c2 // HEAD)
    eye_pair = jnp.where(scol == rowi, 1.0, 0.0).astype(F32)

    def bd(z):
        return jnp.concatenate([jnp.where(lo_half, z, 0.0), jnp.where(lo_half, 0.0, z)], axis=0)

    chains = [(ci, pi) for ci in range(n_sub) for pi in range(n_pairs)]
    rs = {c: slice(c[0] * c_len, (c[0] + 1) * c_len) for c in chains}
    ls = {c: slice(c[1] * PAIR, (c[1] + 1) * PAIR) for c in chains}
    ld = lambda name, c: ops[name][rd, rs[c], ls[c]]
    x, g = {}, {}
    for c in chains:
        x[c] = jnp.concatenate([ld("at", c), ld("rt", c)], axis=0)
        rhs = jnp.concatenate([ld("btl", c), ld("bth", c), ld("ktl", c), ld("kth", c)], axis=0)
        g[c] = _dot_nt(x[c], rhs)
    fill()
    q_b = {c: jnp.where(incl, g[c][c_len:, :PAIR], 0.0).astype(BF16) for c in chains}
    mv = {}
    for c in chains:
        gk = g[c][:, PAIR:]
        gkm = jnp.concatenate([jnp.where(strict, gk[:c_len], 0.0),
                               jnp.where(incl, gk[c_len:], 0.0)], axis=0)
        mv[c] = _dot(gkm.astype(BF16), jnp.concatenate([ld("vl", c), ld("vh", c)], axis=0))
    fill()

    pw = {c: jnp.where(strict, g[c][:c_len, :PAIR], 0.0) for c in chains}
    tinv = {c: eye_pair + pw[c] for c in chains}
    pw = {c: _dot(pw[c].astype(BF16), bd(pw[c]).astype(BF16)) for c in chains}
    fill()
    n_sq = (c_len - 1).bit_length() - 1
    for j in range(n_sq):
        for c in chains:
            pb = pw[c].astype(BF16)
            if j == n_sq - 1:
                tinv[c] = tinv[c] + _dot(pb, bd(tinv[c]).astype(BF16))
            else:
                res = _dot(pb, jnp.concatenate([bd(tinv[c]), bd(pw[c])], axis=1).astype(BF16))
                tinv[c] = tinv[c] + res[:, :PAIR]
                pw[c] = res[:, PAIR:]
        fill()

    s_cur = [s_ref[pi] for pi in range(n_pairs)]
    y_rows = []
    for ci in range(n_sub):
        cs = [(ci, pi) for pi in range(n_pairs)]
        xs = {c: _dot_nt(x[c], s_cur[c[1]].astype(BF16)) for c in cs}
        fill()
        u = {c: _dot(tinv[c].astype(BF16), bd(xs[c][:c_len] + mv[c][:c_len]).astype(BF16))
             for c in cs}
        fill()
        y_cols = []
        for c in cs:
            y_cols.append(xs[c][c_len:] + mv[c][c_len:] + _dot(q_b[c], bd(u[c]).astype(BF16)))
            uv = jnp.concatenate([u[c].astype(BF16), ld("vl", c) + ld("vh", c)], axis=0)
            bk = jnp.concatenate([ld("be", c), ld("ke", c)], axis=0)
            p_last = plast_ref[rd, ci:ci + 1, ls[c]]
            s_cur[c[1]] = s_cur[c[1]] * p_last + jnp.where(same_head, _dot_tn(uv, bk), 0.0)
        y_rows.append(jnp.concatenate(y_cols, axis=1))
        fill()
    for pi in range(n_pairs):
        s_ref[pi] = s_cur[pi]
    return jnp.concatenate(y_rows, axis=0)


def _wkv_seq_kernel(*refs, has_vres, emit_v, blocks_per_seq, cast_steps, zero_steps):
    (r_ref, k_ref, v_ref, lo_ref, z_ref, pr_ref, pk_ref, pv_ref, plo_ref,
     mur_ref, muk_ref, muv_ref, mulo_ref, w0_ref, w2_ref, a0_ref, a2_ref,
     kk_ref, ka_ref, rk_ref, lg_ref, lb_ref, s0_ref) = refs[:23]
    pos = 23
    if has_vres:
        vf_ref, v0_ref, v1_ref, v2_ref = refs[pos:pos + 4]
        pos += 4
    n_cast = len(cast_steps)
    cast_in = refs[pos:pos + n_cast]
    pos += n_cast
    o_ref = refs[pos]
    pos += 1
    if emit_v:
        vo_ref = refs[pos]
        pos += 1
    s_ref = refs[pos]
    cast_out = refs[pos + 1:pos + 1 + n_cast]
    pos += 1 + n_cast
    if zero_steps:
        zero_ref = refs[pos]
        pos += 1
    cr_ref, ck_ref, cv_ref, clo_ref = refs[pos:pos + 4]
    pos += 4
    ops = dict(zip(_CHAIN_OPERANDS, refs[pos:pos + len(_CHAIN_OPERANDS)]))
    vs_ref, bonus_ref, plast_ref = refs[pos + len(_CHAIN_OPERANDS):]

    j = pl.program_id(0)
    first = (j % blocks_per_seq) == 0
    wr = j % 2
    rd = 1 - wr

    @pl.when(j == 0)
    def _():
        for ref in list(ops.values()) + [vs_ref, bonus_ref]:
            ref[1] = jnp.zeros(ref.shape[1:], ref.dtype)
        plast_ref[1] = jnp.ones(plast_ref.shape[1:], F32)

    @pl.when((jnp.maximum(j - 1, 0) % blocks_per_seq) == 0)
    def _():
        s_ref[...] = s0_ref[...]

    rows, width = vs_ref.shape[1:]
    n_sub = rows // WKV_CHUNK
    n_lane_blocks = width // LANES
    ones_bd = _head_ones()
    tri = _chunk_tri(rows)
    shared, mid = {}, {}

    def shared_vector_part():
        lo = _shift_seq(lo_ref[...], mulo_ref[...], plo_ref[...], clo_ref, slice(None), first)
        shared["tanh_lo"] = jnp.tanh(lo).astype(BF16)
        shared["lo"] = lo.astype(BF16)
        shared["v"] = _shift_seq(v_ref[...], muv_ref[...], pv_ref[...], cv_ref, slice(None), first)

    def shared_matmul_part():
        shared["w_lora"] = _dot(shared["tanh_lo"], w2_ref[...])
        shared["a_lora"] = _dot(shared["lo"], a2_ref[...])
        if has_vres:
            shared["vv_mid"] = _dot(shared["v"].astype(BF16), v1_ref[...])

    def shared_matmul_part2():
        if has_vres:
            shared["vv"] = _dot(shared["vv_mid"].astype(BF16), v2_ref[...])

    def block_vector_part(cb):
        sl = slice(cb * LANES, (cb + 1) * LANES)
        r = _shift_seq(r_ref[:, sl], mur_ref[:, sl], pr_ref[:, sl], cr_ref, sl, first)
        k = _shift_seq(k_ref[:, sl], muk_ref[:, sl], pk_ref[:, sl], ck_ref, sl, first)
        v = shared["v"][:, sl]
        if has_vres:
            v = v + (vf_ref[:, sl] - v) * jax.nn.sigmoid(v0_ref[:, sl] + shared["vv"][:, sl])
        lw, a = _decay_and_iclr(w0_ref[:, sl] + shared["w_lora"][:, sl],
                                a0_ref[:, sl] + shared["a_lora"][:, sl])
        kk = k * kk_ref[:, sl]
        k = k * (1.0 + (a - 1.0) * ka_ref[:, sl])
        mid[cb] = dict(r=r, k=k, v=v, a=a, lw=lw, kk=kk, kk2=_split_hi_lo(kk * kk),
                       rk=_split_hi_lo(r * k * rk_ref[:, sl]), lws=_split_hi_lo(lw))

    def block_matmul_part(cb):
        sl = slice(cb * LANES, (cb + 1) * LANES)
        m = mid.pop(cb)
        kk = m["kk"] / jnp.maximum(jnp.sqrt(_dot_hi_lo(m["kk2"], ones_bd, False)), 1e-12)
        bonus = _dot_hi_lo(m["rk"], ones_bd, False) * m["v"]
        lp = _dot_hi_lo(m["lws"], tri, True)
        vals, p_last = _chain_operands(m["r"], m["lw"], lp, m["k"], m["v"], -kk, kk * m["a"])
        for name, val in zip(_CHAIN_OPERANDS, vals):
            ops[name][wr, :, sl] = val
        vs_ref[wr, :, sl] = m["v"]
        bonus_ref[wr, :, sl] = bonus
        plast_ref[wr, :n_sub, sl] = p_last

    def piece(t):
        def run():
            if 0 < t <= n_lane_blocks:
                block_matmul_part(t - 1)
            if t < n_lane_blocks:
                block_vector_part(t)
        return run

    shared_vector_part()
    shared_matmul_part()
    pending = [shared_matmul_part2] + [piece(t) for t in range(n_lane_blocks + 1)]

    slot = [0]

    def fill():
        slot[0] += 1
        if pending and slot[0] not in _EMPTY_FILL_SLOTS:
            pending.pop(0)()

    y = _wkv_chain(ops, plast_ref, rd, s_ref, fill)
    while pending:
        fill()
    o_ref[...] = _rwkv_gate(y, bonus_ref[rd], z_ref[...], lg_ref[...], lb_ref[...]
                            ).astype(o_ref.dtype)
    if emit_v:
        vo_ref[...] = vs_ref[rd]
    for src_ref, dst_ref, steps in zip(cast_in, cast_out, cast_steps):
        @pl.when(j < steps)
        def _(src_ref=src_ref, dst_ref=dst_ref):
            dst_ref[...] = src_ref[...].astype(dst_ref.dtype)
    if zero_steps:
        @pl.when(j < zero_steps)
        def _():
            zero_ref[...] = jnp.zeros_like(zero_ref)


def _wkv_seq(p, p_lo, prev_rkv, prev_lo, tok, gate, s0_bd, vres, layer, casts, zero_fill, *,
             n_seq, seq_len, rw, z_blk, emit_v):
    m = p.shape[0]
    n_blocks = seq_len // SEQ_ROWS
    total = n_seq * n_blocks
    n_pairs = rw // PAIR
    prep_blk = lambda c: jnp.minimum(c, total - 1)
    run_blk = lambda c: jnp.maximum(c - 1, 0)
    row = lambda c: (prep_blk(c), 0)
    col = lambda j: (lambda c: (prep_blk(c), j))
    blk = lambda j: pl.BlockSpec((SEQ_ROWS, rw), col(j))
    in_specs = [blk(0), blk(1), blk(2), pl.BlockSpec((SEQ_ROWS, LO_W), row),
                pl.BlockSpec((SEQ_ROWS, rw), lambda c: (run_blk(c), z_blk))]
    args = [p, p, p, p_lo, p]
    for j in range(3):
        in_specs.append(pl.BlockSpec((None, None, 1, rw),
                                     lambda c, j=j: (layer, prep_blk(c) // n_blocks, 0, j)))
    in_specs.append(pl.BlockSpec((None, None, 1, LO_W),
                                 lambda c: (layer, prep_blk(c) // n_blocks, 0, 0)))
    args += [prev_rkv, prev_rkv, prev_rkv, prev_lo]
    in_specs += _token_param_specs(tok, rw, layer)
    args += list(tok)
    in_specs += [_layer_spec(t, layer) for t in gate]
    args += list(gate)
    in_specs.append(pl.BlockSpec((None, None, n_pairs, PAIR, PAIR),
                                 lambda c: (layer, run_blk(c) // n_blocks, 0, 0, 0)))
    args.append(s0_bd)
    if vres is not None:
        in_specs += _vres_specs(vres, SEQ_ROWS, rw, row, layer)
        args += list(vres)
    cast_steps, cast_specs, cast_shapes = [], [], []
    for w_all, w_layer in casts:
        n_rows, n_cols = w_all.shape[1:]
        rows = next(r for r in range(_ROW_ALIGN, n_rows + 1, _ROW_ALIGN)
                    if n_rows % r == 0 and n_rows // r <= total + 1)
        steps = n_rows // rows
        in_specs.append(pl.BlockSpec((None, rows, n_cols), lambda c, w_layer=w_layer, steps=steps:
                                     (w_layer, jnp.minimum(c, steps - 1), 0)))
        args.append(w_all)
        cast_specs.append(pl.BlockSpec((rows, n_cols), lambda c, steps=steps:
                                       (jnp.minimum(c, steps - 1), 0)))
        cast_shapes.append(jax.ShapeDtypeStruct((n_rows, n_cols), BF16))
        cast_steps.append(steps)
    run_rows = pl.BlockSpec((SEQ_ROWS, rw), lambda c: (run_blk(c), 0))
    out_specs = [run_rows]
    out_shape = [jax.ShapeDtypeStruct((m, rw), BF16)]
    if emit_v:
        out_specs.append(run_rows)
        out_shape.append(jax.ShapeDtypeStruct((m, rw), F32))
    out_specs.append(pl.BlockSpec((None, n_pairs, PAIR, PAIR),
                                  lambda c: (run_blk(c) // n_blocks, 0, 0, 0)))
    out_shape.append(jax.ShapeDtypeStruct((n_seq, n_pairs, PAIR, PAIR), F32))
    out_specs += cast_specs
    out_shape += cast_shapes
    zero_steps = 0
    if zero_fill is not None:
        n_inner = zero_fill[1]
        zero_steps = zero_fill[0] * n_inner
        assert zero_steps <= total + 1
        zblk = lambda c: jnp.minimum(c, zero_steps - 1)
        out_specs.append(pl.BlockSpec((None, None) + tuple(zero_fill[2:]),
                                      lambda c: (zblk(c) // n_inner, zblk(c) % n_inner, 0, 0, 0)))
        out_shape.append(jax.ShapeDtypeStruct(tuple(zero_fill), F32))
    outs = pl.pallas_call(
        functools.partial(_wkv_seq_kernel, has_vres=vres is not None, emit_v=emit_v,
                          blocks_per_seq=n_blocks, cast_steps=tuple(cast_steps),
                          zero_steps=zero_steps),
        grid=(total + 1,),
        in_specs=in_specs,
        out_specs=out_specs,
        out_shape=out_shape,
        scratch_shapes=([pltpu.VMEM((1, rw), F32)] * 3 + [pltpu.VMEM((1, LO_W), F32)]
                        + [pltpu.VMEM((2, SEQ_ROWS, rw), BF16)] * len(_CHAIN_OPERANDS)
                        + [pltpu.VMEM((2, SEQ_ROWS, rw), F32)] * 2
                        + [pltpu.VMEM((2, 8, rw), F32)]),
        compiler_params=_cparams(("arbitrary",)),
        name="wkv_seq",
    )(*args)
    outs = list(outs)
    o_r = outs.pop(0)
    v = outs.pop(0) if emit_v else None
    return o_r, v, outs[0], outs[1:]


def _wkv_step_kernel(r_ref, lw_ref, k_ref, v_ref, a_ref, b_ref, s_ref, _, y_ref, so_ref,
                     vt_ref, yt_ref):
    r_t, k_t, a_t, b_t = r_ref[...].T, k_ref[...].T, a_ref[...].T, b_ref[...].T
    w_t = jnp.exp(lw_ref[...]).T
    vt_ref[...] = v_ref[...].T
    for hh in range(PAIR // HEAD):
        sl = slice(hh * HEAD, (hh + 1) * HEAD)
        r, k, a, b, w = r_t[sl], k_t[sl], a_t[sl], b_t[sl], w_t[sl]

        def body(vi, carry, hh=hh, r=r, k=k, a=a, b=b, w=w):
            s = s_ref[hh, vi]
            sa = jnp.sum(s * a, axis=0, keepdims=True)
            s_new = s * w + sa * b + vt_ref[pl.ds(hh * HEAD + vi, 1), :] * k
            so_ref[hh, vi] = s_new
            yt_ref[pl.ds(hh * HEAD + vi, 1), :] = jnp.sum(s_new * r, axis=0, keepdims=True)
            return carry

        lax.fori_loop(0, HEAD, body, 0, unroll=8)
    y_ref[...] = yt_ref[...].T


def _wkv_step(r, lw, k, v, a, b, s_all, s_out_all, layer):
    m, rw = r.shape
    hp = PAIR // HEAD
    assert m % LANES == 0
    blk = pl.BlockSpec((LANES, PAIR), lambda pi, j: (j, pi))
    sblk = pl.BlockSpec((None, hp, HEAD, HEAD, LANES), lambda pi, j: (layer, pi, 0, 0, j))
    return pl.pallas_call(
        _wkv_step_kernel,
        grid=(rw // PAIR, m // LANES),
        in_specs=[blk] * 6 + [sblk, pl.BlockSpec(memory_space=pl.ANY)],
        out_specs=[blk, sblk],
        out_shape=[jax.ShapeDtypeStruct((m, rw), F32),
                   jax.ShapeDtypeStruct(s_out_all.shape, F32)],
        scratch_shapes=[pltpu.VMEM((PAIR, LANES), F32), pltpu.VMEM((PAIR, LANES), F32)],
        input_output_aliases={7: 1},
        compiler_params=_cparams(("parallel", "parallel")),
        name="wkv_step",
    )(r, lw, k, v, a, b, s_all, s_out_all)


def _gmlp_mix(vg, g, b, ws_ref, bias_ref, seq):
    mu = jnp.mean(vg, axis=-1, keepdims=True)
    d = vg - mu
    var = jnp.mean(d * d, axis=-1, keepdims=True)
    vn = d * lax.rsqrt(var + LN_EPS) * g + b
    if not seq:
        return vn * ws_ref[...] + bias_ref[...], vn
    tm, gw = vg.shape[0], vg.shape[1] // GMLP_GROUPS
    ri = lax.broadcasted_iota(jnp.int32, (GMLP_CHUNK, GMLP_CHUNK), 0)
    ci = lax.broadcasted_iota(jnp.int32, (GMLP_CHUNK, GMLP_CHUNK), 1)
    vb = vn.astype(BF16)
    cols = []
    for gi in range(GMLP_GROUPS):
        wg = jnp.where(ci <= ri, ws_ref[gi], 0.0).astype(BF16)
        rows = [_dot(wg, vb[c * GMLP_CHUNK:(c + 1) * GMLP_CHUNK, gi * gw:(gi + 1) * gw])
                for c in range(tm // GMLP_CHUNK)]
        cols.append(jnp.concatenate(rows, axis=0))
    bias = jnp.concatenate([bias_ref[...]] * (tm // GMLP_CHUNK), axis=0)
    return jnp.concatenate(cols, axis=1) + bias, vn


def _mix_out_kernel(*refs, seq, final):
    if seq:
        or_ref = refs[0]
        pos = 1
        o_r = or_ref[...]
    else:
        y_ref, r_ref, k_ref, v_ref, zr_ref, lg_ref, lb_ref, rk_ref = refs[:8]
        pos = 8
        v = v_ref[...]
        o_r = _rwkv_gate(y_ref[...], _bonus(r_ref[...], k_ref[...], v, rk_ref[...]),
                         zr_ref[...], lg_ref[...], lb_ref[...]).astype(BF16)
    u_ref, vg_ref, zg_ref, h_ref, gg_ref, gb_ref, ws_ref, bias_ref, w_ref = refs[pos:pos + 9]
    rest = refs[pos + 9:]
    rw = o_r.shape[1]
    mixed, vn = _gmlp_mix(vg_ref[...], gg_ref[...], gb_ref[...], ws_ref, bias_ref, seq)
    o_g = u_ref[...] * mixed * _silu(zg_ref[...])
    h = h_ref[...] + _dot(jnp.concatenate([o_r, o_g.astype(BF16)], axis=1), w_ref[...])
    if final:
        ms = jnp.mean(h * h, axis=-1, keepdims=True)
        h = h * lax.rsqrt(ms + RMS_EPS) * rest[0][...]
    outs = rest[1:] if final else rest
    outs[0][...] = h
    if not seq:
        outs[1][...] = vn


def _mix_out(branch, p, h, gln_g, gln_b, ws, bias, w, layer, final_g, *, seq, tm, u_blk):
    m, d = h.shape
    gw = gln_g.shape[-1]
    rw = branch[0].shape[1]
    row = pl.BlockSpec((tm, rw), lambda i: (i, 0))
    col = lambda c: pl.BlockSpec((tm, gw), lambda i, c=c: (i, c))
    hrow = pl.BlockSpec((tm, d), lambda i: (i, 0))
    lspec = lambda t: _layer_spec(t, layer)
    if seq:
        in_specs = [row]
        args = list(branch)
    else:
        y, r, k, v, lnx_g, lnx_b, r_k = branch
        in_specs = [row, row, row, row, col(u_blk - 1), lspec(lnx_g), lspec(lnx_b), lspec(r_k)]
        args = [y, r, k, v, p, lnx_g, lnx_b, r_k]
    in_specs += [col(u_blk), col(u_blk + 1), col(u_blk + 2), hrow, lspec(gln_g), lspec(gln_b),
                 lspec(ws), lspec(bias),
                 pl.BlockSpec((rw + gw, d), lambda i: (0, 0), pipeline_mode=pl.Buffered(1))]
    args += [p, p, p, h, gln_g, gln_b, ws, bias, w]
    if final_g is not None:
        in_specs.append(pl.BlockSpec((1, d), lambda i: (0, 0)))
        args.append(final_g)
    out_specs = [hrow]
    out_shape = [jax.ShapeDtypeStruct((m, d), F32)]
    if not seq:
        out_specs.append(pl.BlockSpec((tm, gw), lambda i: (i, 0)))
        out_shape.append(jax.ShapeDtypeStruct((m, gw), F32))
    outs = pl.pallas_call(
        functools.partial(_mix_out_kernel, seq=seq, final=final_g is not None),
        grid=(m // tm,),
        in_specs=in_specs,
        out_specs=out_specs,
        out_shape=out_shape,
        compiler_params=_cparams(("parallel",)),
        name="mix_out",
    )(*args)
    return (outs[0], None) if seq else (outs[0], outs[1])


def _pad_last(x, width):
    return jnp.pad(x, [(0, 0)] * (x.ndim - 1) + [(0, width - x.shape[-1])])


def _pad_rows(x, rows):
    return jnp.pad(x, [(0, 0)] * (x.ndim - 2) + [(0, rows - x.shape[-2]), (0, 0)])


def _pick_tile(m, pref, mult):
    t = min(m, pref)
    while m % t or t % mult:
        t -= mult
    return t


def _trunk(x, s_init, shift_init, prm, weights, *, seq, zero_fill=None):
    n_seq, seq_len, d = x.shape
    m = n_seq * seq_len
    depth = prm["depth"]
    rw, gw = prm["rw"], prm["gw"]
    lora_w = prm["lora_w"]
    n_heads = rw // HEAD
    tn = 1024
    z_blk = 3

    tm_mm = _pick_tile(m, 1024, 8)
    tm_ew = _pick_tile(m, 512, GMLP_CHUNK if seq else 8)
    assert not seq or seq_len % SEQ_ROWS == 0

    h = x.reshape(m, d)
    s_out, shift_out, vn_out = [], [], []
    prev_rkv = shift_init[:, :, :3 * rw]
    prev_lo = _pad_last(shift_init[:, :, 3 * rw:], LO_W)
    if seq:
        prev_rkv, prev_lo = prev_rkv[:, :, None, :], prev_lo[:, :, None, :]
        s0 = s_init.reshape(depth, n_seq, n_heads // 2, 2, HEAD, HEAD)
        z = jnp.zeros_like(s0[:, :, :, 0])
        s0_bd = jnp.concatenate([jnp.concatenate([s0[:, :, :, 0], z], axis=-1),
                                 jnp.concatenate([z, s0[:, :, :, 1]], axis=-1)], axis=-2)
    else:
        s_init = jnp.transpose(s_init, (0, 2, 3, 4, 1))
        s_buf = weights[2] if weights[2] is not None else jnp.zeros_like(s_init)
        assert s_buf.shape == s_init.shape
    tok = (prm["mu_rkv"],) * 3 + tuple(prm[n] for n in ("mu_lo", "w0", "w2", "a0", "a2",
                                                         "k_k", "k_a"))
    gate = (prm["r_k"], prm["lnx_g"], prm["lnx_b"])
    v_first = None
    w_in_bf, w_out_bf, zero_buf = ([weights], [], None) if seq else weights
    for l in range(depth):
        p, p_lo = _in_proj(h, prm["norm_g"], w_in_bf[l], l, tm_mm, tn, rw, gw, lora_w)
        vres = (v_first, prm["v0"], prm["v1"], prm["v2"]) if l > 0 else None

        if seq:
            casts = [(prm["w_out"], l)] + ([(prm["w_in"], l + 1)] if l + 1 < depth else [])
            o_r, v, s_bd, side = _wkv_seq(p, p_lo, prev_rkv, prev_lo, tok, gate, s0_bd, vres, l,
                                          casts, zero_fill if l == 0 else None, n_seq=n_seq,
                                          seq_len=seq_len, rw=rw, z_blk=z_blk, emit_v=l == 0)
            w_out_bf.append(side[0])
            w_in_bf += side[1:len(casts)]
            if l == 0 and zero_fill is not None:
                zero_buf = side[-1]
            s_out.append(s_bd)
            branch = (o_r,)
            ws, bias = prm["w_s"], prm["bias_seq"]
        else:
            r, lw, k, v, a, b = _rwkv_prep(p, p_lo, prev_rkv, prev_lo, tok, vres, l,
                                           tm=tm_ew, rw=rw)
            y, s_buf = _wkv_step(r, lw, k, v, a, b, s_init, s_buf, l)
            branch = (y, r, k, v, gate[1], gate[2], gate[0])
            ws, bias = prm["w_s0"], prm["bias0"]
        if l == 0:
            v_first = v

        last_row = lambda t: t.reshape(n_seq, seq_len, -1)[:, -1]
        shift_out.append(jnp.concatenate(
            [last_row(p)[:, :3 * rw], last_row(p_lo)[:, :lora_w]], axis=-1))

        h, vn = _mix_out(branch, p, h, prm["gln_g"], prm["gln_b"], ws, bias,
                         w_out_bf[l], l, prm["final_g"] if l == depth - 1 else None,
                         seq=seq, tm=tm_ew, u_blk=z_blk + 1)
        if not seq:
            vn_out.append(vn.reshape(n_seq, seq_len, gw))
    if seq:
        s_bd = jnp.stack(s_out)
        s_final = jnp.stack([s_bd[..., :HEAD, :HEAD], s_bd[..., HEAD:, HEAD:]], axis=3)
        s_final = s_final.reshape(depth, n_seq, n_heads, HEAD, HEAD)
    else:
        s_final = jnp.transpose(s_buf, (0, 4, 1, 2, 3))
    return (h.reshape(n_seq, seq_len, d), s_final, shift_out, vn_out,
            (w_in_bf, w_out_bf, zero_buf))


def kernel(x_prompt, x_sample, state_wkv, state_shift, norm_g, w_in, mu_shift, w0, w2, a0, a2,
           k_k, k_a, r_k, lnx_g, lnx_b, v0, v1, v2, gln_g, gln_b, w_s, b_s, w_out, final_g):
    depth, d, _ = w_in.shape
    rw = w0.shape[1]
    gw = gln_g.shape[1]
    dl, il = w2.shape[1], a2.shape[1]
    assert rw % PAIR == 0 and gw == rw and w_s.shape[2] == GMLP_CHUNK
    assert gw // GMLP_GROUPS == LANES and dl + il <= LO_W
    sc = 3 * rw + dl + il

    row3 = lambda t: t[:, None, :]
    prm = dict(
        depth=depth, rw=rw, gw=gw, lora_w=dl + il,
        norm_g=row3(norm_g),
        w_in=jnp.swapaxes(w_in, 1, 2),
        mu_rkv=row3(mu_shift[:, :3 * rw]),
        mu_lo=row3(_pad_last(mu_shift[:, 3 * rw:], LO_W)),
        w0=row3(w0), w2=_pad_rows(w2, LO_W).astype(BF16),
        a0=row3(a0),
        a2=jnp.pad(a2, ((0, 0), (dl, LO_W - dl - il), (0, 0))).astype(BF16),
        k_k=row3(k_k), k_a=row3(k_a), r_k=row3(r_k.reshape(depth, rw)),
        lnx_g=row3(lnx_g), lnx_b=row3(lnx_b),
        v0=row3(v0), v1=_pad_last(v1, LORA_PAD).astype(BF16), v2=_pad_rows(v2, LORA_PAD).astype(BF16),
        gln_g=row3(gln_g), gln_b=row3(gln_b),
        w_s=w_s,
        bias_seq=jnp.repeat(jnp.swapaxes(b_s, 1, 2), gw // GMLP_GROUPS, axis=2),
        w_s0=row3(jnp.repeat(w_s[:, :, 0, 0], gw // GMLP_GROUPS, axis=1)),
        bias0=row3(jnp.repeat(b_s[:, :, 0], gw // GMLP_GROUPS, axis=1)),
        w_out=w_out,
        final_g=final_g[None, :],
    )

    nb = x_prompt.shape[0]
    n_heads = rw // HEAD
    s0_p = jnp.zeros((depth, nb, n_heads, HEAD, HEAD), F32)
    sh0_p = jnp.zeros((depth, nb, sc), F32)
    state_buf = (depth, n_heads, HEAD, HEAD, x_sample.shape[0] * x_sample.shape[1])
    seq_steps = nb * (x_prompt.shape[1] // SEQ_ROWS) + 1
    y_p, s_p, sh_p, _, weights = _trunk(
        x_prompt, s0_p, sh0_p, prm, prm["w_in"][0].astype(BF16), seq=True,
        zero_fill=state_buf if depth * n_heads <= seq_steps else None)
    y_s, s_s, sh_s, vn_s, _ = _trunk(x_sample, state_wkv, state_shift, prm, weights, seq=False)
    return (y_p, y_s, s_p, jnp.stack(sh_p), s_s, jnp.stack(sh_s), jnp.stack(vn_s))
```

```python
import functools

import jax
import jax.numpy as jnp
from jax import lax
from jax.experimental import pallas as pl
from jax.experimental.pallas import tpu as pltpu

F32 = jnp.float32
BF16 = jnp.bfloat16

HEAD = 64
LANES = 128
PAIR = 2 * HEAD
WKV_CHUNK = HEAD
SEQ_ROWS = 2 * WKV_CHUNK
GMLP_CHUNK = 128
GMLP_GROUPS = 8
LORA_PAD = 128
LO_W = 256
RMS_EPS = 1e-6
LN_EPS = 1e-5
LNX_EPS = 64e-5
VMEM_LIMIT = 56 * 1024 * 1024

_NT = (((1,), (1,)), ((), ()))
_TN = (((0,), (0,)), ((), ()))


def _dot(a, b):
    return jnp.dot(a, b, preferred_element_type=F32)


def _dot_nt(a, b):
    return lax.dot_general(a, b, _NT, preferred_element_type=F32)


def _dot_tn(a, b):
    return lax.dot_general(a, b, _TN, preferred_element_type=F32)


def _cparams(sem):
    return pltpu.CompilerParams(dimension_semantics=sem, vmem_limit_bytes=VMEM_LIMIT)


def _silu(z):
    return z * jax.nn.sigmoid(z)


def _head_ones():
    r = lax.broadcasted_iota(jnp.int32, (LANES, LANES), 0) // HEAD
    c = lax.broadcasted_iota(jnp.int32, (LANES, LANES), 1) // HEAD
    return jnp.where(r == c, 1.0, 0.0).astype(BF16)


def _split_hi_lo(x):
    hi = x.astype(BF16)
    return hi, (x - hi.astype(F32)).astype(BF16)


def _dot_hi_lo(hi_lo, mat, mat_left):
    hi, lo = hi_lo
    return (_dot(mat, hi) + _dot(mat, lo)) if mat_left else (_dot(hi, mat) + _dot(lo, mat))


def _head_sum_hi_lo(hi_lo, ones_bd):
    hi, lo = hi_lo
    outs = [_dot_hi_lo((hi[:, j * LANES:(j + 1) * LANES], lo[:, j * LANES:(j + 1) * LANES]),
                       ones_bd, False) for j in range(hi.shape[1] // LANES)]
    return jnp.concatenate(outs, axis=1)


def _head_sum(x, ones_bd):
    return _head_sum_hi_lo(_split_hi_lo(x), ones_bd)


def _chunk_tri(rows):
    ti = lax.broadcasted_iota(jnp.int32, (rows, rows), 0)
    si = lax.broadcasted_iota(jnp.int32, (rows, rows), 1)
    same_chunk = si // WKV_CHUNK == ti // WKV_CHUNK
    return jnp.where((si <= ti) & same_chunk, 1.0, 0.0).astype(BF16)


def _decay_and_iclr(wraw, araw):
    w = -jax.nn.softplus(-wraw) - 0.5
    return -jnp.exp(w), jax.nn.sigmoid(araw)


_NORM_ROWS = 256


def _in_proj_kernel(*refs, normed):
    if normed:
        xn_ref, w_ref, wlo_ref, o_ref, lo_ref = refs
    else:
        h_ref, g_ref, w_ref, wlo_ref, o_ref, lo_ref, xn_ref = refs

    @pl.when(pl.program_id(1) == 0)
    def _():
        if not normed:
            g = g_ref[...]
            chunk = min(_NORM_ROWS, h_ref.shape[0])

            def norm_rows(c, carry):
                rows = pl.ds(pl.multiple_of(c * chunk, chunk), chunk)
                x = h_ref[rows, :]
                ms = jnp.mean(x * x, axis=-1, keepdims=True)
                xn_ref[rows, :] = (x * lax.rsqrt(ms + RMS_EPS) * g).astype(BF16)
                return carry

            lax.fori_loop(0, h_ref.shape[0] // chunk, norm_rows, 0)
        lo_ref[...] = _dot_nt(xn_ref[...], wlo_ref[...])

    o_ref[...] = _dot_nt(xn_ref[...], w_ref[...])


_ROW_ALIGN = 16


def _in_proj(h, g, wt, layer, tm, tn, rw, gw, lora_w):
    m, d = h.shape
    in_cols = 3 * rw + lora_w + rw + 3 * gw
    n_rkv = 3 * rw // tn
    n_main = n_rkv + (rw + 3 * gw) // tn
    assert (3 * rw) % tn == 0 and (rw + 3 * gw) % tn == 0 and lora_w <= LO_W
    assert in_cols % _ROW_ALIGN == 0 and tn % _ROW_ALIGN == 0 and lora_w % _ROW_ALIGN == 0
    assert wt.shape == (in_cols, d)
    step = tn // _ROW_ALIGN
    z0 = (3 * rw + lora_w) // _ROW_ALIGN
    lo0 = 3 * rw // _ROW_ALIGN

    def w_index(i, j):
        start = jnp.where(j < n_rkv, step * j, z0 + step * (j - n_rkv))
        return (start * _ROW_ALIGN, 0)

    normed = h.dtype == BF16
    rows = pl.BlockSpec((tm, d), lambda i, j: (i, 0))
    return pl.pallas_call(
        functools.partial(_in_proj_kernel, normed=normed),
        grid=(m // tm, n_main),
        in_specs=([rows] if normed else [rows, _layer_spec(g, layer)]) + [
            pl.BlockSpec((pl.Element(tn), pl.Element(d)), w_index),
            pl.BlockSpec((pl.Element(LO_W), pl.Element(d)),
                         lambda i, j: (lo0 * _ROW_ALIGN, 0)),
        ],
        out_specs=[pl.BlockSpec((tm, tn), lambda i, j: (i, j)),
                   pl.BlockSpec((tm, LO_W), lambda i, j: (i, 0))],
        out_shape=[jax.ShapeDtypeStruct((m, n_main * tn), F32),
                   jax.ShapeDtypeStruct((m, LO_W), F32)],
        scratch_shapes=[] if normed else [pltpu.VMEM((tm, d), BF16)],
        compiler_params=_cparams(("parallel", "arbitrary")),
        name="in_proj",
    )(*((h, wt, wt) if normed else (h, g, wt, wt)))


def _rwkv_tokens(k, v, tanh_lo, lo, w0, w2, a0, a2, k_k, k_a, vres):
    lw, a = _decay_and_iclr(w0 + _dot(tanh_lo, w2[...]), a0 + _dot(lo, a2[...]))
    if vres is not None:
        v_first, v0, vv_mid, v2 = vres
        v = v + (v_first - v) * jax.nn.sigmoid(v0 + _dot(vv_mid, v2[...]))
    kk = k * k_k
    kk = kk / jnp.maximum(jnp.sqrt(_head_sum(kk * kk, _head_ones())), 1e-12)
    return lw, k * (1.0 + (a - 1.0) * k_a), v, -kk, kk * a


def _rwkv_gate(y, bonus, z, g, b):
    ones_bd = _head_ones()
    mu = _head_sum(y, ones_bd) * (1.0 / HEAD)
    d = y - mu
    var = _head_sum(d * d, ones_bd) * (1.0 / HEAD)
    return (d * lax.rsqrt(var + LNX_EPS) * g + b + bonus) * _silu(z)


def _bonus(r, k, v, rk):
    return _head_sum(r * k * rk, _head_ones()) * v


def _rwkv_prep_kernel(*refs, has_vres):
    (r_ref, k_ref, v_ref, lo_ref, pr_ref, pk_ref, pv_ref, plo_ref,
     mur_ref, muk_ref, muv_ref, mulo_ref, w0_ref, w2_ref, a0_ref, a2_ref,
     kk_ref, ka_ref) = refs[:18]
    pos = 18
    if has_vres:
        vf_ref, v0_ref, v1_ref, v2_ref = refs[pos:pos + 4]
        pos += 4
    ro_ref, lwo_ref, ko_ref, vo_ref, ao_ref, bo_ref = refs[pos:pos + 6]

    def mix(x_ref, p_ref, mu_ref):
        x = x_ref[...]
        return x + (p_ref[...] - x) * mu_ref[...]

    lo = mix(lo_ref, plo_ref, mulo_ref)
    v = mix(v_ref, pv_ref, muv_ref)
    vres = None
    if has_vres:
        vres = (vf_ref[...], v0_ref[...], _dot(v.astype(BF16), v1_ref[...]).astype(BF16), v2_ref)
    lw, k, v, a, b = _rwkv_tokens(
        mix(k_ref, pk_ref, muk_ref), v, jnp.tanh(lo).astype(BF16), lo.astype(BF16),
        w0_ref[...], w2_ref, a0_ref[...], a2_ref, kk_ref[...], ka_ref[...], vres)
    ro_ref[...] = mix(r_ref, pr_ref, mur_ref)
    lwo_ref[...] = lw
    ko_ref[...] = k
    vo_ref[...] = v
    ao_ref[...] = a
    bo_ref[...] = b


def _layer_spec(arr, layer):
    zeros = (0,) * (arr.ndim - 1)
    return pl.BlockSpec((None,) + arr.shape[1:], lambda *g: (layer,) + zeros)


def _token_param_specs(tok, rw, layer):
    return ([pl.BlockSpec((None, 1, rw), lambda *g, c=c: (layer, 0, c)) for c in range(3)]
            + [_layer_spec(t, layer) for t in tok[3:]])


def _vres_specs(vres, rows, rw, row_index, layer):
    return ([pl.BlockSpec((rows, rw), row_index)]
            + [_layer_spec(t, layer - 1) for t in vres[1:]])


def _rwkv_prep(p, p_lo, prev_rkv, prev_lo, tok, vres, layer, *, tm, rw):
    m = p.shape[0]
    row = lambda i: (i, 0)
    col = lambda c: (lambda i: (i, c))
    in_specs = [pl.BlockSpec((tm, rw), col(0)), pl.BlockSpec((tm, rw), col(1)),
                pl.BlockSpec((tm, rw), col(2)), pl.BlockSpec((tm, LO_W), row)]
    in_specs += [pl.BlockSpec((None, tm, rw), lambda i, c=c: (layer, i, c)) for c in range(3)]
    in_specs.append(pl.BlockSpec((None, tm, LO_W), lambda i: (layer, i, 0)))
    args = [p, p, p, p_lo, prev_rkv, prev_rkv, prev_rkv, prev_lo]
    in_specs += _token_param_specs(tok, rw, layer)
    args += list(tok)
    if vres is not None:
        in_specs += _vres_specs(vres, tm, rw, row, layer)
        args += list(vres)
    out = jax.ShapeDtypeStruct((m, rw), F32)
    return pl.pallas_call(
        functools.partial(_rwkv_prep_kernel, has_vres=vres is not None),
        grid=(m // tm,),
        in_specs=in_specs,
        out_specs=[pl.BlockSpec((tm, rw), row)] * 6,
        out_shape=[out] * 6,
        compiler_params=_cparams(("parallel",)),
        name="rwkv_prep",
    )(*args)


def _shift_seq(x, mu, init_row, carry_ref, sl, first):
    prev0 = jnp.where(first, init_row, carry_ref[:, sl])
    sh = pltpu.roll(x, 1, 0)
    rows = lax.broadcasted_iota(jnp.int32, x.shape, 0)
    sh = jnp.where(rows == 0, prev0, sh)
    carry_ref[:, sl] = x[x.shape[0] - 1:, :]
    return x + (sh - x) * mu


_EMPTY_FILL_SLOTS = (2, 4, 6, 8)

_CHAIN_OPERANDS = ("at", "rt", "btl", "bth", "ktl", "kth", "vl", "vh", "be", "ke")


def _chain_operands(r, lw, lp, k, v, a, b):
    rows, width = r.shape
    c_len = WKV_CHUNK
    n_sub = rows // c_len
    p_ends = [jnp.exp(lp[(ci + 1) * c_len - 1:(ci + 1) * c_len]) for ci in range(n_sub)]
    pinv = jnp.exp(-lp)
    pend = pinv * jnp.concatenate(
        [jnp.broadcast_to(e, (c_len, width)) for e in p_ends], axis=0)
    first_head = lax.broadcasted_iota(jnp.int32, (rows, width), 1) < HEAD
    halves = lambda z: (jnp.where(first_head, z, 0.0), jnp.where(first_head, 0.0, z))
    ops = ((a * jnp.exp(lp - lw), r * jnp.exp(lp)) + halves(b * pinv) + halves(k * pinv)
           + halves(v) + (b * pend, k * pend))
    return [o.astype(BF16) for o in ops], jnp.concatenate(p_ends, axis=0)


def _wkv_chain(ops, plast_ref, rd, s_ref, fill):
    _, rows, width = ops["at"].shape
    c_len = WKV_CHUNK
    n_sub = rows // c_len
    n_pairs = width // PAIR

    lane = lax.broadcasted_iota(jnp.int32, (c_len, PAIR), 1)
    rowi = lax.broadcasted_iota(jnp.int32, (c_len, PAIR), 0)
    lo_half = lane < HEAD
    scol = lane % HEAD
    strict = scol < rowi
    incl = scol <= rowi
    r2 = lax.broadcasted_iota(jnp.int32, (PAIR, PAIR), 0)
    c2 = lax.broadcasted_iota(jnp.int32, (PAIR, PAIR), 1)
    same_head = (r2 // HEAD) == (c2 // HEAD)
    eye_pair = jnp.where(scol == rowi, 1.0, 0.0).astype(F32)

    def bd(z):
        return jnp.concatenate([jnp.where(lo_half, z, 0.0), jnp.where(lo_half, 0.0, z)], axis=0)

    chains = [(ci, pi) for ci in range(n_sub) for pi in range(n_pairs)]
    rs = {c: slice(c[0] * c_len, (c[0] + 1) * c_len) for c in chains}
    ls = {c: slice(c[1] * PAIR, (c[1] + 1) * PAIR) for c in chains}
    ld = lambda name, c: ops[name][rd, rs[c], ls[c]]
    x, g = {}, {}
    for c in chains:
        x[c] = jnp.concatenate([ld("at", c), ld("rt", c)], axis=0)
        rhs = jnp.concatenate([ld("btl", c), ld("bth", c), ld("ktl", c), ld("kth", c)], axis=0)
        g[c] = _dot_nt(x[c], rhs)
    fill()
    q_b = {c: jnp.where(incl, g[c][c_len:, :PAIR], 0.0).astype(BF16) for c in chains}
    mv = {}
    for c in chains:
        gk = g[c][:, PAIR:]
        gkm = jnp.concatenate([jnp.where(strict, gk[:c_len], 0.0),
                               jnp.where(incl, gk[c_len:], 0.0)], axis=0)
        mv[c] = _dot(gkm.astype(BF16), jnp.concatenate([ld("vl", c), ld("vh", c)], axis=0))
    fill()

    pw = {c: jnp.where(strict, g[c][:c_len, :PAIR], 0.0) for c in chains}
    tinv = {c: eye_pair + pw[c] for c in chains}
    pw = {c: _dot(pw[c].astype(BF16), bd(pw[c]).astype(BF16)) for c in chains}
    fill()
    n_sq = (c_len - 1).bit_length() - 1
    for j in range(n_sq):
        for c in chains:
            pb = pw[c].astype(BF16)
            if j == n_sq - 1:
                tinv[c] = tinv[c] + _dot(pb, bd(tinv[c]).astype(BF16))
            else:
                res = _dot(pb, jnp.concatenate([bd(tinv[c]), bd(pw[c])], axis=1).astype(BF16))
                tinv[c] = tinv[c] + res[:, :PAIR]
                pw[c] = res[:, PAIR:]
        fill()

    s_cur = [s_ref[pi] for pi in range(n_pairs)]
    y_rows = []
    for ci in range(n_sub):
        cs = [(ci, pi) for pi in range(n_pairs)]
        xs = {c: _dot_nt(x[c], s_cur[c[1]].astype(BF16)) for c in cs}
        fill()
        u = {c: _dot(tinv[c].astype(BF16), bd(xs[c][:c_len] + mv[c][:c_len]).astype(BF16))
             for c in cs}
        fill()
        y_cols = []
        for c in cs:
            y_cols.append(xs[c][c_len:] + mv[c][c_len:] + _dot(q_b[c], bd(u[c]).astype(BF16)))
            uv = jnp.concatenate([u[c].astype(BF16), ld("vl", c) + ld("vh", c)], axis=0)
            bk = jnp.concatenate([ld("be", c), ld("ke", c)], axis=0)
            p_last = plast_ref[rd, ci:ci + 1, ls[c]]
            s_cur[c[1]] = s_cur[c[1]] * p_last + jnp.where(same_head, _dot_tn(uv, bk), 0.0)
        y_rows.append(jnp.concatenate(y_cols, axis=1))
        fill()
    for pi in range(n_pairs):
        s_ref[pi] = s_cur[pi]
    return jnp.concatenate(y_rows, axis=0)


def _wkv_seq_kernel(*refs, has_vres, emit_v, blocks_per_seq, cast_steps, zero_steps):
    (r_ref, k_ref, v_ref, lo_ref, z_ref, pr_ref, pk_ref, pv_ref, plo_ref,
     mur_ref, muk_ref, muv_ref, mulo_ref, w0_ref, w2_ref, a0_ref, a2_ref,
     kk_ref, ka_ref, rk_ref, lg_ref, lb_ref, s0_ref) = refs[:23]
    pos = 23
    if has_vres:
        vf_ref, v0_ref, v1_ref, v2_ref = refs[pos:pos + 4]
        pos += 4
    n_cast = len(cast_steps)
    cast_in = refs[pos:pos + n_cast]
    pos += n_cast
    o_ref = refs[pos]
    pos += 1
    if emit_v:
        vo_ref = refs[pos]
        pos += 1
    s_ref = refs[pos]
    cast_out = refs[pos + 1:pos + 1 + n_cast]
    pos += 1 + n_cast
    if zero_steps:
        zero_ref = refs[pos]
        pos += 1
    cr_ref, ck_ref, cv_ref, clo_ref = refs[pos:pos + 4]
    pos += 4
    ops = dict(zip(_CHAIN_OPERANDS, refs[pos:pos + len(_CHAIN_OPERANDS)]))
    vs_ref, bonus_ref, plast_ref = refs[pos + len(_CHAIN_OPERANDS):]

    j = pl.program_id(0)
    first = (j % blocks_per_seq) == 0
    wr = j % 2
    rd = 1 - wr

    @pl.when(j == 0)
    def _():
        for ref in list(ops.values()) + [vs_ref, bonus_ref]:
            ref[1] = jnp.zeros(ref.shape[1:], ref.dtype)
        plast_ref[1] = jnp.ones(plast_ref.shape[1:], F32)

    @pl.when((jnp.maximum(j - 1, 0) % blocks_per_seq) == 0)
    def _():
        s_ref[...] = s0_ref[...]

    rows, width = vs_ref.shape[1:]
    n_sub = rows // WKV_CHUNK
    n_lane_blocks = width // LANES
    ones_bd = _head_ones()
    tri = _chunk_tri(rows)
    shared, mid = {}, {}

    def shared_vector_part():
        lo = _shift_seq(lo_ref[...], mulo_ref[...], plo_ref[...], clo_ref, slice(None), first)
        shared["tanh_lo"] = jnp.tanh(lo).astype(BF16)
        shared["lo"] = lo.astype(BF16)
        shared["v"] = _shift_seq(v_ref[...], muv_ref[...], pv_ref[...], cv_ref, slice(None), first)

    def shared_matmul_part():
        shared["w_lora"] = _dot(shared["tanh_lo"], w2_ref[...])
        shared["a_lora"] = _dot(shared["lo"], a2_ref[...])
        if has_vres:
            shared["vv_mid"] = _dot(shared["v"].astype(BF16), v1_ref[...])

    def shared_matmul_part2():
        if has_vres:
            shared["vv"] = _dot(shared["vv_mid"].astype(BF16), v2_ref[...])

    def block_vector_part(cb):
        sl = slice(cb * LANES, (cb + 1) * LANES)
        r = _shift_seq(r_ref[:, sl], mur_ref[:, sl], pr_ref[:, sl], cr_ref, sl, first)
        k = _shift_seq(k_ref[:, sl], muk_ref[:, sl], pk_ref[:, sl], ck_ref, sl, first)
        v = shared["v"][:, sl]
        if has_vres:
            v = v + (vf_ref[:, sl] - v) * jax.nn.sigmoid(v0_ref[:, sl] + shared["vv"][:, sl])
        lw, a = _decay_and_iclr(w0_ref[:, sl] + shared["w_lora"][:, sl],
                                a0_ref[:, sl] + shared["a_lora"][:, sl])
        kk = k * kk_ref[:, sl]
        k = k * (1.0 + (a - 1.0) * ka_ref[:, sl])
        mid[cb] = dict(r=r, k=k, v=v, a=a, lw=lw, kk=kk, kk2=_split_hi_lo(kk * kk),
                       rk=_split_hi_lo(r * k * rk_ref[:, sl]), lws=_split_hi_lo(lw))

    def block_matmul_part(cb):
        sl = slice(cb * LANES, (cb + 1) * LANES)
        m = mid.pop(cb)
        kk = m["kk"] / jnp.maximum(jnp.sqrt(_dot_hi_lo(m["kk2"], ones_bd, False)), 1e-12)
        bonus = _dot_hi_lo(m["rk"], ones_bd, False) * m["v"]
        lp = _dot_hi_lo(m["lws"], tri, True)
        vals, p_last = _chain_operands(m["r"], m["lw"], lp, m["k"], m["v"], -kk, kk * m["a"])
        for name, val in zip(_CHAIN_OPERANDS, vals):
            ops[name][wr, :, sl] = val
        vs_ref[wr, :, sl] = m["v"]
        bonus_ref[wr, :, sl] = bonus
        plast_ref[wr, :n_sub, sl] = p_last

    def piece(t):
        def run():
            if 0 < t <= n_lane_blocks:
                block_matmul_part(t - 1)
            if t < n_lane_blocks:
                block_vector_part(t)
        return run

    shared_vector_part()
    shared_matmul_part()
    pending = [shared_matmul_part2] + [piece(t) for t in range(n_lane_blocks + 1)]

    slot = [0]

    def fill():
        slot[0] += 1
        if pending and slot[0] not in _EMPTY_FILL_SLOTS:
            pending.pop(0)()

    y = _wkv_chain(ops, plast_ref, rd, s_ref, fill)
    while pending:
        fill()
    o_ref[...] = _rwkv_gate(y, bonus_ref[rd], z_ref[...], lg_ref[...], lb_ref[...]
                            ).astype(o_ref.dtype)
    if emit_v:
        vo_ref[...] = vs_ref[rd]
    for src_ref, dst_ref, steps in zip(cast_in, cast_out, cast_steps):
        @pl.when(j < steps)
        def _(src_ref=src_ref, dst_ref=dst_ref):
            dst_ref[...] = src_ref[...].astype(dst_ref.dtype)
    if zero_steps:
        @pl.when(j < zero_steps)
        def _():
            zero_ref[...] = jnp.zeros_like(zero_ref)


def _wkv_seq(p, p_lo, prev_rkv, prev_lo, tok, gate, s0_bd, vres, layer, casts, zero_fill, *,
             n_seq, seq_len, rw, z_blk, emit_v):
    m = p.shape[0]
    n_blocks = seq_len // SEQ_ROWS
    total = n_seq * n_blocks
    n_pairs = rw // PAIR
    prep_blk = lambda c: jnp.minimum(c, total - 1)
    run_blk = lambda c: jnp.maximum(c - 1, 0)
    row = lambda c: (prep_blk(c), 0)
    col = lambda j: (lambda c: (prep_blk(c), j))
    blk = lambda j: pl.BlockSpec((SEQ_ROWS, rw), col(j))
    in_specs = [blk(0), blk(1), blk(2), pl.BlockSpec((SEQ_ROWS, LO_W), row),
                pl.BlockSpec((SEQ_ROWS, rw), lambda c: (run_blk(c), z_blk))]
    args = [p, p, p, p_lo, p]
    for j in range(3):
        in_specs.append(pl.BlockSpec((None, None, 1, rw),
                                     lambda c, j=j: (layer, prep_blk(c) // n_blocks, 0, j)))
    in_specs.append(pl.BlockSpec((None, None, 1, LO_W),
                                 lambda c: (layer, prep_blk(c) // n_blocks, 0, 0)))
    args += [prev_rkv, prev_rkv, prev_rkv, prev_lo]
    in_specs += _token_param_specs(tok, rw, layer)
    args += list(tok)
    in_specs += [_layer_spec(t, layer) for t in gate]
    args += list(gate)
    in_specs.append(pl.BlockSpec((None, None, n_pairs, PAIR, PAIR),
                                 lambda c: (layer, run_blk(c) // n_blocks, 0, 0, 0)))
    args.append(s0_bd)
    if vres is not None:
        in_specs += _vres_specs(vres, SEQ_ROWS, rw, row, layer)
        args += list(vres)
    cast_steps, cast_specs, cast_shapes = [], [], []
    for w_all, w_layer in casts:
        n_rows, n_cols = w_all.shape[1:]
        rows = next(r for r in range(_ROW_ALIGN, n_rows + 1, _ROW_ALIGN)
                    if n_rows % r == 0 and n_rows // r <= total + 1)
        steps = n_rows // rows
        in_specs.append(pl.BlockSpec((None, rows, n_cols), lambda c, w_layer=w_layer, steps=steps:
                                     (w_layer, jnp.minimum(c, steps - 1), 0)))
        args.append(w_all)
        cast_specs.append(pl.BlockSpec((rows, n_cols), lambda c, steps=steps:
                                       (jnp.minimum(c, steps - 1), 0)))
        cast_shapes.append(jax.ShapeDtypeStruct((n_rows, n_cols), BF16))
        cast_steps.append(steps)
    run_rows = pl.BlockSpec((SEQ_ROWS, rw), lambda c: (run_blk(c), 0))
    out_specs = [run_rows]
    out_shape = [jax.ShapeDtypeStruct((m, rw), BF16)]
    if emit_v:
        out_specs.append(run_rows)
        out_shape.append(jax.ShapeDtypeStruct((m, rw), F32))
    out_specs.append(pl.BlockSpec((None, n_pairs, PAIR, PAIR),
                                  lambda c: (run_blk(c) // n_blocks, 0, 0, 0)))
    out_shape.append(jax.ShapeDtypeStruct((n_seq, n_pairs, PAIR, PAIR), F32))
    out_specs += cast_specs
    out_shape += cast_shapes
    zero_steps = 0
    if zero_fill is not None:
        n_inner = zero_fill[1]
        zero_steps = zero_fill[0] * n_inner
        assert zero_steps <= total + 1
        zblk = lambda c: jnp.minimum(c, zero_steps - 1)
        out_specs.append(pl.BlockSpec((None, None) + tuple(zero_fill[2:]),
                                      lambda c: (zblk(c) // n_inner, zblk(c) % n_inner, 0, 0, 0)))
        out_shape.append(jax.ShapeDtypeStruct(tuple(zero_fill), F32))
    outs = pl.pallas_call(
        functools.partial(_wkv_seq_kernel, has_vres=vres is not None, emit_v=emit_v,
                          blocks_per_seq=n_blocks, cast_steps=tuple(cast_steps),
                          zero_steps=zero_steps),
        grid=(total + 1,),
        in_specs=in_specs,
        out_specs=out_specs,
        out_shape=out_shape,
        scratch_shapes=([pltpu.VMEM((1, rw), F32)] * 3 + [pltpu.VMEM((1, LO_W), F32)]
                        + [pltpu.VMEM((2, SEQ_ROWS, rw), BF16)] * len(_CHAIN_OPERANDS)
                        + [pltpu.VMEM((2, SEQ_ROWS, rw), F32)] * 2
                        + [pltpu.VMEM((2, 8, rw), F32)]),
        compiler_params=_cparams(("arbitrary",)),
        name="wkv_seq",
    )(*args)
    outs = list(outs)
    o_r = outs.pop(0)
    v = outs.pop(0) if emit_v else None
    return o_r, v, outs[0], outs[1:]


def _wkv_step_kernel(r_ref, lw_ref, k_ref, v_ref, a_ref, b_ref, s_ref, _, y_ref, so_ref,
                     vt_ref, yt_ref):
    r_t, k_t, a_t, b_t = r_ref[...].T, k_ref[...].T, a_ref[...].T, b_ref[...].T
    w_t = jnp.exp(lw_ref[...]).T
    vt_ref[...] = v_ref[...].T
    for hh in range(PAIR // HEAD):
        sl = slice(hh * HEAD, (hh + 1) * HEAD)
        r, k, a, b, w = r_t[sl], k_t[sl], a_t[sl], b_t[sl], w_t[sl]

        def body(vi, carry, hh=hh, r=r, k=k, a=a, b=b, w=w):
            s = s_ref[hh, vi]
            sa = jnp.sum(s * a, axis=0, keepdims=True)
            s_new = s * w + sa * b + vt_ref[pl.ds(hh * HEAD + vi, 1), :] * k
            so_ref[hh, vi] = s_new
            yt_ref[pl.ds(hh * HEAD + vi, 1), :] = jnp.sum(s_new * r, axis=0, keepdims=True)
            return carry

        lax.fori_loop(0, HEAD, body, 0, unroll=8)
    y_ref[...] = yt_ref[...].T


def _wkv_step(r, lw, k, v, a, b, s_all, s_out_all, layer):
    m, rw = r.shape
    hp = PAIR // HEAD
    assert m % LANES == 0
    blk = pl.BlockSpec((LANES, PAIR), lambda pi, j: (j, pi))
    sblk = pl.BlockSpec((None, hp, HEAD, HEAD, LANES), lambda pi, j: (layer, pi, 0, 0, j))
    return pl.pallas_call(
        _wkv_step_kernel,
        grid=(rw // PAIR, m // LANES),
        in_specs=[blk] * 6 + [sblk, pl.BlockSpec(memory_space=pl.ANY)],
        out_specs=[blk, sblk],
        out_shape=[jax.ShapeDtypeStruct((m, rw), F32),
                   jax.ShapeDtypeStruct(s_out_all.shape, F32)],
        scratch_shapes=[pltpu.VMEM((PAIR, LANES), F32), pltpu.VMEM((PAIR, LANES), F32)],
        input_output_aliases={7: 1},
        compiler_params=_cparams(("parallel", "parallel")),
        name="wkv_step",
    )(r, lw, k, v, a, b, s_all, s_out_all)


def _gmlp_mix(vg, g, b, ws_ref, bias_ref, seq):
    mu = jnp.mean(vg, axis=-1, keepdims=True)
    d = vg - mu
    var = jnp.mean(d * d, axis=-1, keepdims=True)
    vn = d * lax.rsqrt(var + LN_EPS) * g + b
    if not seq:
        return vn * ws_ref[...] + bias_ref[...], vn
    tm, gw = vg.shape[0], vg.shape[1] // GMLP_GROUPS
    ri = lax.broadcasted_iota(jnp.int32, (GMLP_CHUNK, GMLP_CHUNK), 0)
    ci = lax.broadcasted_iota(jnp.int32, (GMLP_CHUNK, GMLP_CHUNK), 1)
    vb = vn.astype(BF16)
    cols = []
    for gi in range(GMLP_GROUPS):
        wg = jnp.where(ci <= ri, ws_ref[gi], 0.0).astype(BF16)
        rows = [_dot(wg, vb[c * GMLP_CHUNK:(c + 1) * GMLP_CHUNK, gi * gw:(gi + 1) * gw])
                for c in range(tm // GMLP_CHUNK)]
        cols.append(jnp.concatenate(rows, axis=0))
    bias = jnp.concatenate([bias_ref[...]] * (tm // GMLP_CHUNK), axis=0)
    return jnp.concatenate(cols, axis=1) + bias, vn


def _mix_out_kernel(*refs, seq, final):
    if seq:
        or_ref = refs[0]
        pos = 1
        o_r = or_ref[...]
    else:
        y_ref, r_ref, k_ref, v_ref, zr_ref, lg_ref, lb_ref, rk_ref = refs[:8]
        pos = 8
        v = v_ref[...]
        o_r = _rwkv_gate(y_ref[...], _bonus(r_ref[...], k_ref[...], v, rk_ref[...]),
                         zr_ref[...], lg_ref[...], lb_ref[...]).astype(BF16)
    u_ref, vg_ref, zg_ref, h_ref, gg_ref, gb_ref, ws_ref, bias_ref, w_ref = refs[pos:pos + 9]
    rest = refs[pos + 9:]
    rw = o_r.shape[1]
    mixed, vn = _gmlp_mix(vg_ref[...], gg_ref[...], gb_ref[...], ws_ref, bias_ref, seq)
    o_g = u_ref[...] * mixed * _silu(zg_ref[...])
    h = h_ref[...] + _dot(jnp.concatenate([o_r, o_g.astype(BF16)], axis=1), w_ref[...])
    ms = jnp.mean(h * h, axis=-1, keepdims=True)
    hn = h * lax.rsqrt(ms + RMS_EPS) * rest[0][...]
    outs = list(rest[1:])
    if final:
        outs[0][...] = hn
    else:
        outs[0][...] = h
        outs.pop(1)[...] = hn.astype(BF16)
    if not seq:
        outs[1][...] = vn


def _mix_out(branch, p, h, gln_g, gln_b, ws, bias, w, layer, next_g, final_g, *, seq, tm,
             u_blk):
    m, d = h.shape
    gw = gln_g.shape[-1]
    rw = branch[0].shape[1]
    row = pl.BlockSpec((tm, rw), lambda i: (i, 0))
    col = lambda c: pl.BlockSpec((tm, gw), lambda i, c=c: (i, c))
    hrow = pl.BlockSpec((tm, d), lambda i: (i, 0))
    lspec = lambda t: _layer_spec(t, layer)
    if seq:
        in_specs = [row]
        args = list(branch)
    else:
        y, r, k, v, lnx_g, lnx_b, r_k = branch
        in_specs = [row, row, row, row, col(u_blk - 1), lspec(lnx_g), lspec(lnx_b), lspec(r_k)]
        args = [y, r, k, v, p, lnx_g, lnx_b, r_k]
    in_specs += [col(u_blk), col(u_blk + 1), col(u_blk + 2), hrow, lspec(gln_g), lspec(gln_b),
                 lspec(ws), lspec(bias),
                 pl.BlockSpec((rw + gw, d), lambda i: (0, 0), pipeline_mode=pl.Buffered(1))]
    args += [p, p, p, h, gln_g, gln_b, ws, bias, w]
    final = final_g is not None
    in_specs.append(pl.BlockSpec((1, d), lambda i: (0, 0)) if final
                    else _layer_spec(next_g, layer + 1))
    args.append(final_g if final else next_g)
    out_specs = [hrow]
    out_shape = [jax.ShapeDtypeStruct((m, d), F32)]
    if not final:
        out_specs.append(hrow)
        out_shape.append(jax.ShapeDtypeStruct((m, d), BF16))
    if not seq:
        out_specs.append(pl.BlockSpec((tm, gw), lambda i: (i, 0)))
        out_shape.append(jax.ShapeDtypeStruct((m, gw), F32))
    outs = list(pl.pallas_call(
        functools.partial(_mix_out_kernel, seq=seq, final=final),
        grid=(m // tm,),
        in_specs=in_specs,
        out_specs=out_specs,
        out_shape=out_shape,
        compiler_params=_cparams(("parallel",)),
        name="mix_out",
    )(*args))
    h_out = outs.pop(0)
    xn = None if final else outs.pop(0)
    return h_out, xn, (None if seq else outs[0])


def _pad_last(x, width):
    return jnp.pad(x, [(0, 0)] * (x.ndim - 1) + [(0, width - x.shape[-1])])


def _pad_rows(x, rows):
    return jnp.pad(x, [(0, 0)] * (x.ndim - 2) + [(0, rows - x.shape[-2]), (0, 0)])


def _pick_tile(m, pref, mult):
    t = min(m, pref)
    while m % t or t % mult:
        t -= mult
    return t


def _trunk(x, s_init, shift_init, prm, weights, *, seq, zero_fill=None):
    n_seq, seq_len, d = x.shape
    m = n_seq * seq_len
    depth = prm["depth"]
    rw, gw = prm["rw"], prm["gw"]
    lora_w = prm["lora_w"]
    n_heads = rw // HEAD
    tn = 1024
    z_blk = 3

    tm_mm = _pick_tile(m, 1024, 8)
    tm_ew = _pick_tile(m, 512, GMLP_CHUNK if seq else 8)
    assert not seq or seq_len % SEQ_ROWS == 0

    h = x.reshape(m, d)
    s_out, shift_out, vn_out = [], [], []
    prev_rkv = shift_init[:, :, :3 * rw]
    prev_lo = _pad_last(shift_init[:, :, 3 * rw:], LO_W)
    if seq:
        prev_rkv, prev_lo = prev_rkv[:, :, None, :], prev_lo[:, :, None, :]
        s0 = s_init.reshape(depth, n_seq, n_heads // 2, 2, HEAD, HEAD)
        z = jnp.zeros_like(s0[:, :, :, 0])
        s0_bd = jnp.concatenate([jnp.concatenate([s0[:, :, :, 0], z], axis=-1),
                                 jnp.concatenate([z, s0[:, :, :, 1]], axis=-1)], axis=-2)
    else:
        s_init = jnp.transpose(s_init, (0, 2, 3, 4, 1))
        s_buf = weights[2] if weights[2] is not None else jnp.zeros_like(s_init)
        assert s_buf.shape == s_init.shape
    tok = (prm["mu_rkv"],) * 3 + tuple(prm[n] for n in ("mu_lo", "w0", "w2", "a0", "a2",
                                                         "k_k", "k_a"))
    gate = (prm["r_k"], prm["lnx_g"], prm["lnx_b"])
    v_first = None
    w_in_bf, w_out_bf, zero_buf = ([weights], [], None) if seq else weights
    for l in range(depth):
        p, p_lo = _in_proj(h if l == 0 else xn, prm["norm_g"], w_in_bf[l], l, tm_mm, tn, rw,
                           gw, lora_w)
        vres = (v_first, prm["v0"], prm["v1"], prm["v2"]) if l > 0 else None

        if seq:
            casts = [(prm["w_out"], l)] + ([(prm["w_in"], l + 1)] if l + 1 < depth else [])
            o_r, v, s_bd, side = _wkv_seq(p, p_lo, prev_rkv, prev_lo, tok, gate, s0_bd, vres, l,
                                          casts, zero_fill if l == 0 else None, n_seq=n_seq,
                                          seq_len=seq_len, rw=rw, z_blk=z_blk, emit_v=l == 0)
            w_out_bf.append(side[0])
            w_in_bf += side[1:len(casts)]
            if l == 0 and zero_fill is not None:
                zero_buf = side[-1]
            s_out.append(s_bd)
            branch = (o_r,)
            ws, bias = prm["w_s"], prm["bias_seq"]
        else:
            r, lw, k, v, a, b = _rwkv_prep(p, p_lo, prev_rkv, prev_lo, tok, vres, l,
                                           tm=tm_ew, rw=rw)
            y, s_buf = _wkv_step(r, lw, k, v, a, b, s_init, s_buf, l)
            branch = (y, r, k, v, gate[1], gate[2], gate[0])
            ws, bias = prm["w_s0"], prm["bias0"]
        if l == 0:
            v_first = v

        last_row = lambda t: t.reshape(n_seq, seq_len, -1)[:, -1]
        shift_out.append(jnp.concatenate(
            [last_row(p)[:, :3 * rw], last_row(p_lo)[:, :lora_w]], axis=-1))

        h, xn, vn = _mix_out(branch, p, h, prm["gln_g"], prm["gln_b"], ws, bias,
                             w_out_bf[l], l, prm["norm_g"],
                             prm["final_g"] if l == depth - 1 else None,
                             seq=seq, tm=tm_ew, u_blk=z_blk + 1)
        if not seq:
            vn_out.append(vn.reshape(n_seq, seq_len, gw))
    if seq:
        s_bd = jnp.stack(s_out)
        s_final = jnp.stack([s_bd[..., :HEAD, :HEAD], s_bd[..., HEAD:, HEAD:]], axis=3)
        s_final = s_final.reshape(depth, n_seq, n_heads, HEAD, HEAD)
    else:
        s_final = jnp.transpose(s_buf, (0, 4, 1, 2, 3))
    return (h.reshape(n_seq, seq_len, d), s_final, shift_out, vn_out,
            (w_in_bf, w_out_bf, zero_buf))


def kernel(x_prompt, x_sample, state_wkv, state_shift, norm_g, w_in, mu_shift, w0, w2, a0, a2,
           k_k, k_a, r_k, lnx_g, lnx_b, v0, v1, v2, gln_g, gln_b, w_s, b_s, w_out, final_g):
    depth, d, _ = w_in.shape
    rw = w0.shape[1]
    gw = gln_g.shape[1]
    dl, il = w2.shape[1], a2.shape[1]
    assert rw % PAIR == 0 and gw == rw and w_s.shape[2] == GMLP_CHUNK
    assert gw // GMLP_GROUPS == LANES and dl + il <= LO_W
    sc = 3 * rw + dl + il

    row3 = lambda t: t[:, None, :]
    prm = dict(
        depth=depth, rw=rw, gw=gw, lora_w=dl + il,
        norm_g=row3(norm_g),
        w_in=jnp.swapaxes(w_in, 1, 2),
        mu_rkv=row3(mu_shift[:, :3 * rw]),
        mu_lo=row3(_pad_last(mu_shift[:, 3 * rw:], LO_W)),
        w0=row3(w0), w2=_pad_rows(w2, LO_W).astype(BF16),
        a0=row3(a0),
        a2=jnp.pad(a2, ((0, 0), (dl, LO_W - dl - il), (0, 0))).astype(BF16),
        k_k=row3(k_k), k_a=row3(k_a), r_k=row3(r_k.reshape(depth, rw)),
        lnx_g=row3(lnx_g), lnx_b=row3(lnx_b),
        v0=row3(v0), v1=_pad_last(v1, LORA_PAD).astype(BF16), v2=_pad_rows(v2, LORA_PAD).astype(BF16),
        gln_g=row3(gln_g), gln_b=row3(gln_b),
        w_s=w_s,
        bias_seq=jnp.repeat(jnp.swapaxes(b_s, 1, 2), gw // GMLP_GROUPS, axis=2),
        w_s0=row3(jnp.repeat(w_s[:, :, 0, 0], gw // GMLP_GROUPS, axis=1)),
        bias0=row3(jnp.repeat(b_s[:, :, 0], gw // GMLP_GROUPS, axis=1)),
        w_out=w_out,
        final_g=final_g[None, :],
    )

    nb = x_prompt.shape[0]
    n_heads = rw // HEAD
    s0_p = jnp.zeros((depth, nb, n_heads, HEAD, HEAD), F32)
    sh0_p = jnp.zeros((depth, nb, sc), F32)
    state_buf = (depth, n_heads, HEAD, HEAD, x_sample.shape[0] * x_sample.shape[1])
    seq_steps = nb * (x_prompt.shape[1] // SEQ_ROWS) + 1
    y_p, s_p, sh_p, _, weights = _trunk(
        x_prompt, s0_p, sh0_p, prm, prm["w_in"][0].astype(BF16), seq=True,
        zero_fill=state_buf if depth * n_heads <= seq_steps else None)
    y_s, s_s, sh_s, vn_s, _ = _trunk(x_sample, state_wkv, state_shift, prm, weights, seq=False)
    return (y_p, y_s, s_p, jnp.stack(sh_p), s_s, jnp.stack(sh_s), jnp.stack(vn_s))
```

```python
import functools

import jax
import jax.numpy as jnp
from jax import lax
from jax.experimental import pallas as pl
from jax.experimental.pallas import tpu as pltpu

F32 = jnp.float32
BF16 = jnp.bfloat16

HEAD = 64
LANES = 128
PAIR = 2 * HEAD
WKV_CHUNK = HEAD
SEQ_ROWS = 2 * WKV_CHUNK
GMLP_CHUNK = 128
GMLP_GROUPS = 8
LORA_PAD = 128
LO_W = 256
RMS_EPS = 1e-6
LN_EPS = 1e-5
LNX_EPS = 64e-5
VMEM_LIMIT = 56 * 1024 * 1024

_NT = (((1,), (1,)), ((), ()))
_TN = (((0,), (0,)), ((), ()))


def _dot(a, b):
    return jnp.dot(a, b, preferred_element_type=F32)


def _dot_nt(a, b):
    return lax.dot_general(a, b, _NT, preferred_element_type=F32)


def _dot_tn(a, b):
    return lax.dot_general(a, b, _TN, preferred_element_type=F32)


def _cparams(sem):
    return pltpu.CompilerParams(dimension_semantics=sem, vmem_limit_bytes=VMEM_LIMIT)


def _silu(z):
    return z * jax.nn.sigmoid(z)


def _head_ones():
    r = lax.broadcasted_iota(jnp.int32, (LANES, LANES), 0) // HEAD
    c = lax.broadcasted_iota(jnp.int32, (LANES, LANES), 1) // HEAD
    return jnp.where(r == c, 1.0, 0.0).astype(BF16)


def _split_hi_lo(x):
    hi = x.astype(BF16)
    return hi, (x - hi.astype(F32)).astype(BF16)


def _dot_hi_lo(hi_lo, mat, mat_left):
    hi, lo = hi_lo
    return (_dot(mat, hi) + _dot(mat, lo)) if mat_left else (_dot(hi, mat) + _dot(lo, mat))


def _head_sum_hi_lo(hi_lo, ones_bd):
    hi, lo = hi_lo
    outs = [_dot_hi_lo((hi[:, j * LANES:(j + 1) * LANES], lo[:, j * LANES:(j + 1) * LANES]),
                       ones_bd, False) for j in range(hi.shape[1] // LANES)]
    return jnp.concatenate(outs, axis=1)


def _head_sum(x, ones_bd):
    return _head_sum_hi_lo(_split_hi_lo(x), ones_bd)


def _chunk_tri(rows):
    ti = lax.broadcasted_iota(jnp.int32, (rows, rows), 0)
    si = lax.broadcasted_iota(jnp.int32, (rows, rows), 1)
    same_chunk = si // WKV_CHUNK == ti // WKV_CHUNK
    return jnp.where((si <= ti) & same_chunk, 1.0, 0.0).astype(BF16)


def _decay_and_iclr(wraw, araw):
    w = -jax.nn.softplus(-wraw) - 0.5
    return -jnp.exp(w), jax.nn.sigmoid(araw)


_NORM_ROWS = 256


def _in_proj_kernel(*refs, normed):
    if normed:
        xn_ref, w_ref, wlo_ref, o_ref, lo_ref = refs
    else:
        h_ref, g_ref, w_ref, wlo_ref, o_ref, lo_ref, xn_ref = refs

    @pl.when(pl.program_id(1) == 0)
    def _():
        if not normed:
            g = g_ref[...]
            chunk = min(_NORM_ROWS, h_ref.shape[0])

            def norm_rows(c, carry):
                rows = pl.ds(pl.multiple_of(c * chunk, chunk), chunk)
                x = h_ref[rows, :]
                ms = jnp.mean(x * x, axis=-1, keepdims=True)
                xn_ref[rows, :] = (x * lax.rsqrt(ms + RMS_EPS) * g).astype(BF16)
                return carry

            lax.fori_loop(0, h_ref.shape[0] // chunk, norm_rows, 0)
        lo_ref[...] = _dot_nt(xn_ref[...], wlo_ref[...])

    chunk = min(_DOT_ROWS, o_ref.shape[0])
    for r in range(0, o_ref.shape[0], chunk):
        o_ref[r:r + chunk, :] = _dot_nt(xn_ref[r:r + chunk, :], w_ref[...])


_ROW_ALIGN = 16
_DOT_ROWS = 1024


def _in_proj(h, g, wt, layer, tm, tn, rw, gw, lora_w):
    m, d = h.shape
    in_cols = 3 * rw + lora_w + rw + 3 * gw
    n_rkv = 3 * rw // tn
    n_main = n_rkv + (rw + 3 * gw) // tn
    assert (3 * rw) % tn == 0 and (rw + 3 * gw) % tn == 0 and lora_w <= LO_W
    assert in_cols % _ROW_ALIGN == 0 and tn % _ROW_ALIGN == 0 and lora_w % _ROW_ALIGN == 0
    assert wt.shape == (in_cols, d)
    step = tn // _ROW_ALIGN
    z0 = (3 * rw + lora_w) // _ROW_ALIGN
    lo0 = 3 * rw // _ROW_ALIGN

    def w_index(i, j):
        start = jnp.where(j < n_rkv, step * j, z0 + step * (j - n_rkv))
        return (start * _ROW_ALIGN, 0)

    normed = h.dtype == BF16
    rows = pl.BlockSpec((tm, d), lambda i, j: (i, 0))
    return pl.pallas_call(
        functools.partial(_in_proj_kernel, normed=normed),
        grid=(m // tm, n_main),
        in_specs=([rows] if normed else [rows, _layer_spec(g, layer)]) + [
            pl.BlockSpec((pl.Element(tn), pl.Element(d)), w_index),
            pl.BlockSpec((pl.Element(LO_W), pl.Element(d)),
                         lambda i, j: (lo0 * _ROW_ALIGN, 0)),
        ],
        out_specs=[pl.BlockSpec((tm, tn), lambda i, j: (i, j)),
                   pl.BlockSpec((tm, LO_W), lambda i, j: (i, 0))],
        out_shape=[jax.ShapeDtypeStruct((m, n_main * tn), F32),
                   jax.ShapeDtypeStruct((m, LO_W), F32)],
        scratch_shapes=[] if normed else [pltpu.VMEM((tm, d), BF16)],
        compiler_params=_cparams(("parallel", "arbitrary")),
        name="in_proj",
    )(*((h, wt, wt) if normed else (h, g, wt, wt)))


def _rwkv_tokens(k, v, tanh_lo, lo, w0, w2, a0, a2, k_k, k_a, vres):
    lw, a = _decay_and_iclr(w0 + _dot(tanh_lo, w2[...]), a0 + _dot(lo, a2[...]))
    if vres is not None:
        v_first, v0, vv_mid, v2 = vres
        v = v + (v_first - v) * jax.nn.sigmoid(v0 + _dot(vv_mid, v2[...]))
    kk = k * k_k
    kk = kk / jnp.maximum(jnp.sqrt(_head_sum(kk * kk, _head_ones())), 1e-12)
    return lw, k * (1.0 + (a - 1.0) * k_a), v, -kk, kk * a


def _rwkv_gate(y, bonus, z, g, b):
    ones_bd = _head_ones()
    mu = _head_sum(y, ones_bd) * (1.0 / HEAD)
    d = y - mu
    var = _head_sum(d * d, ones_bd) * (1.0 / HEAD)
    return (d * lax.rsqrt(var + LNX_EPS) * g + b + bonus) * _silu(z)


def _bonus(r, k, v, rk):
    return _head_sum(r * k * rk, _head_ones()) * v


def _rwkv_prep_kernel(*refs, has_vres):
    (r_ref, k_ref, v_ref, lo_ref, pr_ref, pk_ref, pv_ref, plo_ref,
     mur_ref, muk_ref, muv_ref, mulo_ref, w0_ref, w2_ref, a0_ref, a2_ref,
     kk_ref, ka_ref) = refs[:18]
    pos = 18
    if has_vres:
        vf_ref, v0_ref, v1_ref, v2_ref = refs[pos:pos + 4]
        pos += 4
    ro_ref, lwo_ref, ko_ref, vo_ref, ao_ref, bo_ref = refs[pos:pos + 6]

    def mix(x_ref, p_ref, mu_ref):
        x = x_ref[...]
        return x + (p_ref[...] - x) * mu_ref[...]

    lo = mix(lo_ref, plo_ref, mulo_ref)
    v = mix(v_ref, pv_ref, muv_ref)
    vres = None
    if has_vres:
        vres = (vf_ref[...], v0_ref[...], _dot(v.astype(BF16), v1_ref[...]).astype(BF16), v2_ref)
    lw, k, v, a, b = _rwkv_tokens(
        mix(k_ref, pk_ref, muk_ref), v, jnp.tanh(lo).astype(BF16), lo.astype(BF16),
        w0_ref[...], w2_ref, a0_ref[...], a2_ref, kk_ref[...], ka_ref[...], vres)
    ro_ref[...] = mix(r_ref, pr_ref, mur_ref)
    lwo_ref[...] = lw
    ko_ref[...] = k
    vo_ref[...] = v
    ao_ref[...] = a
    bo_ref[...] = b


def _layer_spec(arr, layer):
    zeros = (0,) * (arr.ndim - 1)
    return pl.BlockSpec((None,) + arr.shape[1:], lambda *g: (layer,) + zeros)


def _token_param_specs(tok, rw, layer):
    return ([pl.BlockSpec((None, 1, rw), lambda *g, c=c: (layer, 0, c)) for c in range(3)]
            + [_layer_spec(t, layer) for t in tok[3:]])


def _vres_specs(vres, rows, rw, row_index, layer):
    return ([pl.BlockSpec((rows, rw), row_index)]
            + [_layer_spec(t, layer - 1) for t in vres[1:]])


def _rwkv_prep(p, p_lo, prev_rkv, prev_lo, tok, vres, layer, *, tm, rw):
    m = p.shape[0]
    row = lambda i: (i, 0)
    col = lambda c: (lambda i: (i, c))
    in_specs = [pl.BlockSpec((tm, rw), col(0)), pl.BlockSpec((tm, rw), col(1)),
                pl.BlockSpec((tm, rw), col(2)), pl.BlockSpec((tm, LO_W), row)]
    in_specs += [pl.BlockSpec((None, tm, rw), lambda i, c=c: (layer, i, c)) for c in range(3)]
    in_specs.append(pl.BlockSpec((None, tm, LO_W), lambda i: (layer, i, 0)))
    args = [p, p, p, p_lo, prev_rkv, prev_rkv, prev_rkv, prev_lo]
    in_specs += _token_param_specs(tok, rw, layer)
    args += list(tok)
    if vres is not None:
        in_specs += _vres_specs(vres, tm, rw, row, layer)
        args += list(vres)
    out = jax.ShapeDtypeStruct((m, rw), F32)
    return pl.pallas_call(
        functools.partial(_rwkv_prep_kernel, has_vres=vres is not None),
        grid=(m // tm,),
        in_specs=in_specs,
        out_specs=[pl.BlockSpec((tm, rw), row)] * 6,
        out_shape=[out] * 6,
        compiler_params=_cparams(("parallel",)),
        name="rwkv_prep",
    )(*args)


def _shift_seq(x, mu, init_row, carry_ref, sl, first):
    prev0 = jnp.where(first, init_row, carry_ref[:, sl])
    sh = pltpu.roll(x, 1, 0)
    rows = lax.broadcasted_iota(jnp.int32, x.shape, 0)
    sh = jnp.where(rows == 0, prev0, sh)
    carry_ref[:, sl] = x[x.shape[0] - 1:, :]
    return x + (sh - x) * mu


_EMPTY_FILL_SLOTS = (2, 4, 6, 8)

_CHAIN_OPERANDS = ("at", "rt", "btl", "bth", "ktl", "kth", "vl", "vh", "be", "ke")


def _chain_operands(r, lw, lp, k, v, a, b):
    rows, width = r.shape
    c_len = WKV_CHUNK
    n_sub = rows // c_len
    p_ends = [jnp.exp(lp[(ci + 1) * c_len - 1:(ci + 1) * c_len]) for ci in range(n_sub)]
    pinv = jnp.exp(-lp)
    pend = pinv * jnp.concatenate(
        [jnp.broadcast_to(e, (c_len, width)) for e in p_ends], axis=0)
    first_head = lax.broadcasted_iota(jnp.int32, (rows, width), 1) < HEAD
    halves = lambda z: (jnp.where(first_head, z, 0.0), jnp.where(first_head, 0.0, z))
    ops = ((a * jnp.exp(lp - lw), r * jnp.exp(lp)) + halves(b * pinv) + halves(k * pinv)
           + halves(v) + (b * pend, k * pend))
    return [o.astype(BF16) for o in ops], jnp.concatenate(p_ends, axis=0)


def _wkv_chain(ops, plast_ref, rd, s_ref, fill):
    _, rows, width = ops["at"].shape
    c_len = WKV_CHUNK
    n_sub = rows // c_len
    n_pairs = width // PAIR

    lane = lax.broadcasted_iota(jnp.int32, (c_len, PAIR), 1)
    rowi = lax.broadcasted_iota(jnp.int32, (c_len, PAIR), 0)
    lo_half = lane < HEAD
    scol = lane % HEAD
    strict = scol < rowi
    incl = scol <= rowi
    r2 = lax.broadcasted_iota(jnp.int32, (PAIR, PAIR), 0)
    c2 = lax.broadcasted_iota(jnp.int32, (PAIR, PAIR), 1)
    same_head = (r2 // HEAD) == (c2 // HEAD)
    eye_pair = jnp.where(scol == rowi, 1.0, 0.0).astype(F32)

    def bd(z):
        return jnp.concatenate([jnp.where(lo_half, z, 0.0), jnp.where(lo_half, 0.0, z)], axis=0)

    chains = [(ci, pi) for ci in range(n_sub) for pi in range(n_pairs)]
    rs = {c: slice(c[0] * c_len, (c[0] + 1) * c_len) for c in chains}
    ls = {c: slice(c[1] * PAIR, (c[1] + 1) * PAIR) for c in chains}
    ld = lambda name, c: ops[name][rd, rs[c], ls[c]]
    x, g = {}, {}
    for c in chains:
        x[c] = jnp.concatenate([ld("at", c), ld("rt", c)], axis=0)
        rhs = jnp.concatenate([ld("btl", c), ld("bth", c), ld("ktl", c), ld("kth", c)], axis=0)
        g[c] = _dot_nt(x[c], rhs)
    fill()
    q_b = {c: jnp.where(incl, g[c][c_len:, :PAIR], 0.0).astype(BF16) for c in chains}
    mv = {}
    for c in chains:
        gk = g[c][:, PAIR:]
        gkm = jnp.concatenate([jnp.where(strict, gk[:c_len], 0.0),
                               jnp.where(incl, gk[c_len:], 0.0)], axis=0)
        mv[c] = _dot(gkm.astype(BF16), jnp.concatenate([ld("vl", c), ld("vh", c)], axis=0))
    fill()

    pw = {c: jnp.where(strict, g[c][:c_len, :PAIR], 0.0) for c in chains}
    tinv = {c: eye_pair + pw[c] for c in chains}
    pw = {c: _dot(pw[c].astype(BF16), bd(pw[c]).astype(BF16)) for c in chains}
    fill()
    n_sq = (c_len - 1).bit_length() - 1
    for j in range(n_sq):
        for c in chains:
            pb = pw[c].astype(BF16)
            if j == n_sq - 1:
                tinv[c] = tinv[c] + _dot(pb, bd(tinv[c]).astype(BF16))
            else:
                res = _dot(pb, jnp.concatenate([bd(tinv[c]), bd(pw[c])], axis=1).astype(BF16))
                tinv[c] = tinv[c] + res[:, :PAIR]
                pw[c] = res[:, PAIR:]
        fill()

    s_cur = [s_ref[pi] for pi in range(n_pairs)]
    y_rows = []
    for ci in range(n_sub):
        cs = [(ci, pi) for pi in range(n_pairs)]
        xs = {c: _dot_nt(x[c], s_cur[c[1]].astype(BF16)) for c in cs}
        fill()
        u = {c: _dot(tinv[c].astype(BF16), bd(xs[c][:c_len] + mv[c][:c_len]).astype(BF16))
             for c in cs}
        fill()
        y_cols = []
        for c in cs:
            y_cols.append(xs[c][c_len:] + mv[c][c_len:] + _dot(q_b[c], bd(u[c]).astype(BF16)))
            uv = jnp.concatenate([u[c].astype(BF16), ld("vl", c) + ld("vh", c)], axis=0)
            bk = jnp.concatenate([ld("be", c), ld("ke", c)], axis=0)
            p_last = plast_ref[rd, ci:ci + 1, ls[c]]
            s_cur[c[1]] = s_cur[c[1]] * p_last + jnp.where(same_head, _dot_tn(uv, bk), 0.0)
        y_rows.append(jnp.concatenate(y_cols, axis=1))
        fill()
    for pi in range(n_pairs):
        s_ref[pi] = s_cur[pi]
    return jnp.concatenate(y_rows, axis=0)


def _wkv_seq_kernel(*refs, has_vres, emit_v, blocks_per_seq, cast_steps, zero_steps):
    (r_ref, k_ref, v_ref, lo_ref, z_ref, pr_ref, pk_ref, pv_ref, plo_ref,
     mur_ref, muk_ref, muv_ref, mulo_ref, w0_ref, w2_ref, a0_ref, a2_ref,
     kk_ref, ka_ref, rk_ref, lg_ref, lb_ref, s0_ref) = refs[:23]
    pos = 23
    if has_vres:
        vf_ref, v0_ref, v1_ref, v2_ref = refs[pos:pos + 4]
        pos += 4
    n_cast = len(cast_steps)
    cast_in = refs[pos:pos + n_cast]
    pos += n_cast
    o_ref = refs[pos]
    pos += 1
    if emit_v:
        vo_ref = refs[pos]
        pos += 1
    s_ref = refs[pos]
    cast_out = refs[pos + 1:pos + 1 + n_cast]
    pos += 1 + n_cast
    if zero_steps:
        zero_ref = refs[pos]
        pos += 1
    cr_ref, ck_ref, cv_ref, clo_ref = refs[pos:pos + 4]
    pos += 4
    ops = dict(zip(_CHAIN_OPERANDS, refs[pos:pos + len(_CHAIN_OPERANDS)]))
    vs_ref, bonus_ref, plast_ref = refs[pos + len(_CHAIN_OPERANDS):]

    j = pl.program_id(0)
    first = (j % blocks_per_seq) == 0
    wr = j % 2
    rd = 1 - wr

    @pl.when(j == 0)
    def _():
        for ref in list(ops.values()) + [vs_ref, bonus_ref]:
            ref[1] = jnp.zeros(ref.shape[1:], ref.dtype)
        plast_ref[1] = jnp.ones(plast_ref.shape[1:], F32)

    @pl.when((jnp.maximum(j - 1, 0) % blocks_per_seq) == 0)
    def _():
        s_ref[...] = s0_ref[...]

    rows, width = vs_ref.shape[1:]
    n_sub = rows // WKV_CHUNK
    n_lane_blocks = width // LANES
    ones_bd = _head_ones()
    tri = _chunk_tri(rows)
    shared, mid = {}, {}

    def shared_vector_part():
        lo = _shift_seq(lo_ref[...], mulo_ref[...], plo_ref[...], clo_ref, slice(None), first)
        shared["tanh_lo"] = jnp.tanh(lo).astype(BF16)
        shared["lo"] = lo.astype(BF16)
        shared["v"] = _shift_seq(v_ref[...], muv_ref[...], pv_ref[...], cv_ref, slice(None), first)

    def shared_matmul_part():
        shared["w_lora"] = _dot(shared["tanh_lo"], w2_ref[...])
        shared["a_lora"] = _dot(shared["lo"], a2_ref[...])
        if has_vres:
            shared["vv_mid"] = _dot(shared["v"].astype(BF16), v1_ref[...])

    def shared_matmul_part2():
        if has_vres:
            shared["vv"] = _dot(shared["vv_mid"].astype(BF16), v2_ref[...])

    def block_vector_part(cb):
        sl = slice(cb * LANES, (cb + 1) * LANES)
        r = _shift_seq(r_ref[:, sl], mur_ref[:, sl], pr_ref[:, sl], cr_ref, sl, first)
        k = _shift_seq(k_ref[:, sl], muk_ref[:, sl], pk_ref[:, sl], ck_ref, sl, first)
        v = shared["v"][:, sl]
        if has_vres:
            v = v + (vf_ref[:, sl] - v) * jax.nn.sigmoid(v0_ref[:, sl] + shared["vv"][:, sl])
        lw, a = _decay_and_iclr(w0_ref[:, sl] + shared["w_lora"][:, sl],
                                a0_ref[:, sl] + shared["a_lora"][:, sl])
        kk = k * kk_ref[:, sl]
        k = k * (1.0 + (a - 1.0) * ka_ref[:, sl])
        mid[cb] = dict(r=r, k=k, v=v, a=a, lw=lw, kk=kk, kk2=_split_hi_lo(kk * kk),
                       rk=_split_hi_lo(r * k * rk_ref[:, sl]), lws=_split_hi_lo(lw))

    def block_matmul_part(cb):
        sl = slice(cb * LANES, (cb + 1) * LANES)
        m = mid.pop(cb)
        kk = m["kk"] / jnp.maximum(jnp.sqrt(_dot_hi_lo(m["kk2"], ones_bd, False)), 1e-12)
        bonus = _dot_hi_lo(m["rk"], ones_bd, False) * m["v"]
        lp = _dot_hi_lo(m["lws"], tri, True)
        vals, p_last = _chain_operands(m["r"], m["lw"], lp, m["k"], m["v"], -kk, kk * m["a"])
        for name, val in zip(_CHAIN_OPERANDS, vals):
            ops[name][wr, :, sl] = val
        vs_ref[wr, :, sl] = m["v"]
        bonus_ref[wr, :, sl] = bonus
        plast_ref[wr, :n_sub, sl] = p_last

    def piece(t):
        def run():
            if 0 < t <= n_lane_blocks:
                block_matmul_part(t - 1)
            if t < n_lane_blocks:
                block_vector_part(t)
        return run

    shared_vector_part()
    shared_matmul_part()
    pending = [shared_matmul_part2] + [piece(t) for t in range(n_lane_blocks + 1)]

    slot = [0]

    def fill():
        slot[0] += 1
        if pending and slot[0] not in _EMPTY_FILL_SLOTS:
            pending.pop(0)()

    y = _wkv_chain(ops, plast_ref, rd, s_ref, fill)
    while pending:
        fill()
    o_ref[...] = _rwkv_gate(y, bonus_ref[rd], z_ref[...], lg_ref[...], lb_ref[...]
                            ).astype(o_ref.dtype)
    if emit_v:
        vo_ref[...] = vs_ref[rd]
    for src_ref, dst_ref, steps in zip(cast_in, cast_out, cast_steps):
        @pl.when(j < steps)
        def _(src_ref=src_ref, dst_ref=dst_ref):
            dst_ref[...] = src_ref[...].astype(dst_ref.dtype)
    if zero_steps:
        @pl.when(j < zero_steps)
        def _():
            zero_ref[...] = jnp.zeros_like(zero_ref)


def _wkv_seq(p, p_lo, prev_rkv, prev_lo, tok, gate, s0_bd, vres, layer, casts, zero_fill, *,
             n_seq, seq_len, rw, z_blk, emit_v):
    m = p.shape[0]
    n_blocks = seq_len // SEQ_ROWS
    total = n_seq * n_blocks
    n_pairs = rw // PAIR
    prep_blk = lambda c: jnp.minimum(c, total - 1)
    run_blk = lambda c: jnp.maximum(c - 1, 0)
    row = lambda c: (prep_blk(c), 0)
    col = lambda j: (lambda c: (prep_blk(c), j))
    blk = lambda j: pl.BlockSpec((SEQ_ROWS, rw), col(j))
    in_specs = [blk(0), blk(1), blk(2), pl.BlockSpec((SEQ_ROWS, LO_W), row),
                pl.BlockSpec((SEQ_ROWS, rw), lambda c: (run_blk(c), z_blk))]
    args = [p, p, p, p_lo, p]
    for j in range(3):
        in_specs.append(pl.BlockSpec((None, None, 1, rw),
                                     lambda c, j=j: (layer, prep_blk(c) // n_blocks, 0, j)))
    in_specs.append(pl.BlockSpec((None, None, 1, LO_W),
                                 lambda c: (layer, prep_blk(c) // n_blocks, 0, 0)))
    args += [prev_rkv, prev_rkv, prev_rkv, prev_lo]
    in_specs += _token_param_specs(tok, rw, layer)
    args += list(tok)
    in_specs += [_layer_spec(t, layer) for t in gate]
    args += list(gate)
    in_specs.append(pl.BlockSpec((None, None, n_pairs, PAIR, PAIR),
                                 lambda c: (layer, run_blk(c) // n_blocks, 0, 0, 0)))
    args.append(s0_bd)
    if vres is not None:
        in_specs += _vres_specs(vres, SEQ_ROWS, rw, row, layer)
        args += list(vres)
    cast_steps, cast_specs, cast_shapes = [], [], []
    for w_all, w_layer in casts:
        n_rows, n_cols = w_all.shape[1:]
        rows = next(r for r in range(_ROW_ALIGN, n_rows + 1, _ROW_ALIGN)
                    if n_rows % r == 0 and n_rows // r <= total + 1)
        steps = n_rows // rows
        in_specs.append(pl.BlockSpec((None, rows, n_cols), lambda c, w_layer=w_layer, steps=steps:
                                     (w_layer, jnp.minimum(c, steps - 1), 0)))
        args.append(w_all)
        cast_specs.append(pl.BlockSpec((rows, n_cols), lambda c, steps=steps:
                                       (jnp.minimum(c, steps - 1), 0)))
        cast_shapes.append(jax.ShapeDtypeStruct((n_rows, n_cols), BF16))
        cast_steps.append(steps)
    run_rows = pl.BlockSpec((SEQ_ROWS, rw), lambda c: (run_blk(c), 0))
    out_specs = [run_rows]
    out_shape = [jax.ShapeDtypeStruct((m, rw), BF16)]
    if emit_v:
        out_specs.append(run_rows)
        out_shape.append(jax.ShapeDtypeStruct((m, rw), F32))
    out_specs.append(pl.BlockSpec((None, n_pairs, PAIR, PAIR),
                                  lambda c: (run_blk(c) // n_blocks, 0, 0, 0)))
    out_shape.append(jax.ShapeDtypeStruct((n_seq, n_pairs, PAIR, PAIR), F32))
    out_specs += cast_specs
    out_shape += cast_shapes
    zero_steps = 0
    if zero_fill is not None:
        n_inner = zero_fill[1]
        zero_steps = zero_fill[0] * n_inner
        assert zero_steps <= total + 1
        zblk = lambda c: jnp.minimum(c, zero_steps - 1)
        out_specs.append(pl.BlockSpec((None, None) + tuple(zero_fill[2:]),
                                      lambda c: (zblk(c) // n_inner, zblk(c) % n_inner, 0, 0, 0)))
        out_shape.append(jax.ShapeDtypeStruct(tuple(zero_fill), F32))
    outs = pl.pallas_call(
        functools.partial(_wkv_seq_kernel, has_vres=vres is not None, emit_v=emit_v,
                          blocks_per_seq=n_blocks, cast_steps=tuple(cast_steps),
                          zero_steps=zero_steps),
        grid=(total + 1,),
        in_specs=in_specs,
        out_specs=out_specs,
        out_shape=out_shape,
        scratch_shapes=([pltpu.VMEM((1, rw), F32)] * 3 + [pltpu.VMEM((1, LO_W), F32)]
                        + [pltpu.VMEM((2, SEQ_ROWS, rw), BF16)] * len(_CHAIN_OPERANDS)
                        + [pltpu.VMEM((2, SEQ_ROWS, rw), F32)] * 2
                        + [pltpu.VMEM((2, 8, rw), F32)]),
        compiler_params=_cparams(("arbitrary",)),
        name="wkv_seq",
    )(*args)
    outs = list(outs)
    o_r = outs.pop(0)
    v = outs.pop(0) if emit_v else None
    return o_r, v, outs[0], outs[1:]


def _wkv_step_kernel(r_ref, lw_ref, k_ref, v_ref, a_ref, b_ref, s_ref, _, y_ref, so_ref,
                     vt_ref, yt_ref):
    r_t, k_t, a_t, b_t = r_ref[...].T, k_ref[...].T, a_ref[...].T, b_ref[...].T
    w_t = jnp.exp(lw_ref[...]).T
    vt_ref[...] = v_ref[...].T
    for hh in range(PAIR // HEAD):
        sl = slice(hh * HEAD, (hh + 1) * HEAD)
        r, k, a, b, w = r_t[sl], k_t[sl], a_t[sl], b_t[sl], w_t[sl]

        def body(vi, carry, hh=hh, r=r, k=k, a=a, b=b, w=w):
            s = s_ref[hh, vi]
            sa = jnp.sum(s * a, axis=0, keepdims=True)
            s_new = s * w + sa * b + vt_ref[pl.ds(hh * HEAD + vi, 1), :] * k
            so_ref[hh, vi] = s_new
            yt_ref[pl.ds(hh * HEAD + vi, 1), :] = jnp.sum(s_new * r, axis=0, keepdims=True)
            return carry

        lax.fori_loop(0, HEAD, body, 0, unroll=8)
    y_ref[...] = yt_ref[...].T


def _wkv_step(r, lw, k, v, a, b, s_all, s_out_all, layer):
    m, rw = r.shape
    hp = PAIR // HEAD
    assert m % LANES == 0
    blk = pl.BlockSpec((LANES, PAIR), lambda pi, j: (j, pi))
    sblk = pl.BlockSpec((None, hp, HEAD, HEAD, LANES), lambda pi, j: (layer, pi, 0, 0, j))
    return pl.pallas_call(
        _wkv_step_kernel,
        grid=(rw // PAIR, m // LANES),
        in_specs=[blk] * 6 + [sblk, pl.BlockSpec(memory_space=pl.ANY)],
        out_specs=[blk, sblk],
        out_shape=[jax.ShapeDtypeStruct((m, rw), F32),
                   jax.ShapeDtypeStruct(s_out_all.shape, F32)],
        scratch_shapes=[pltpu.VMEM((PAIR, LANES), F32), pltpu.VMEM((PAIR, LANES), F32)],
        input_output_aliases={7: 1},
        compiler_params=_cparams(("parallel", "parallel")),
        name="wkv_step",
    )(r, lw, k, v, a, b, s_all, s_out_all)


def _gmlp_mix(vg, g, b, ws_ref, bias_ref, seq):
    mu = jnp.mean(vg, axis=-1, keepdims=True)
    d = vg - mu
    var = jnp.mean(d * d, axis=-1, keepdims=True)
    vn = d * lax.rsqrt(var + LN_EPS) * g + b
    if not seq:
        return vn * ws_ref[...] + bias_ref[...], vn
    tm, gw = vg.shape[0], vg.shape[1] // GMLP_GROUPS
    ri = lax.broadcasted_iota(jnp.int32, (GMLP_CHUNK, GMLP_CHUNK), 0)
    ci = lax.broadcasted_iota(jnp.int32, (GMLP_CHUNK, GMLP_CHUNK), 1)
    vb = vn.astype(BF16)
    cols = []
    for gi in range(GMLP_GROUPS):
        wg = jnp.where(ci <= ri, ws_ref[gi], 0.0).astype(BF16)
        rows = [_dot(wg, vb[c * GMLP_CHUNK:(c + 1) * GMLP_CHUNK, gi * gw:(gi + 1) * gw])
                for c in range(tm // GMLP_CHUNK)]
        cols.append(jnp.concatenate(rows, axis=0))
    bias = jnp.concatenate([bias_ref[...]] * (tm // GMLP_CHUNK), axis=0)
    return jnp.concatenate(cols, axis=1) + bias, vn


def _mix_out_kernel(*refs, seq, final):
    if seq:
        or_ref = refs[0]
        pos = 1
        o_r = or_ref[...]
    else:
        y_ref, r_ref, k_ref, v_ref, zr_ref, lg_ref, lb_ref, rk_ref = refs[:8]
        pos = 8
        v = v_ref[...]
        o_r = _rwkv_gate(y_ref[...], _bonus(r_ref[...], k_ref[...], v, rk_ref[...]),
                         zr_ref[...], lg_ref[...], lb_ref[...]).astype(BF16)
    u_ref, vg_ref, zg_ref, h_ref, gg_ref, gb_ref, ws_ref, bias_ref, w_ref = refs[pos:pos + 9]
    rest = refs[pos + 9:]
    rw = o_r.shape[1]
    mixed, vn = _gmlp_mix(vg_ref[...], gg_ref[...], gb_ref[...], ws_ref, bias_ref, seq)
    o_g = u_ref[...] * mixed * _silu(zg_ref[...])
    h = h_ref[...] + _dot(jnp.concatenate([o_r, o_g.astype(BF16)], axis=1), w_ref[...])
    ms = jnp.mean(h * h, axis=-1, keepdims=True)
    hn = h * lax.rsqrt(ms + RMS_EPS) * rest[0][...]
    outs = list(rest[1:])
    if final:
        outs[0][...] = hn
    else:
        outs[0][...] = h
        outs.pop(1)[...] = hn.astype(BF16)
    if not seq:
        outs[1][...] = vn


def _mix_out(branch, p, h, gln_g, gln_b, ws, bias, w, layer, next_g, final_g, *, seq, tm,
             u_blk):
    m, d = h.shape
    gw = gln_g.shape[-1]
    rw = branch[0].shape[1]
    row = pl.BlockSpec((tm, rw), lambda i: (i, 0))
    col = lambda c: pl.BlockSpec((tm, gw), lambda i, c=c: (i, c))
    hrow = pl.BlockSpec((tm, d), lambda i: (i, 0))
    lspec = lambda t: _layer_spec(t, layer)
    if seq:
        in_specs = [row]
        args = list(branch)
    else:
        y, r, k, v, lnx_g, lnx_b, r_k = branch
        in_specs = [row, row, row, row, col(u_blk - 1), lspec(lnx_g), lspec(lnx_b), lspec(r_k)]
        args = [y, r, k, v, p, lnx_g, lnx_b, r_k]
    in_specs += [col(u_blk), col(u_blk + 1), col(u_blk + 2), hrow, lspec(gln_g), lspec(gln_b),
                 lspec(ws), lspec(bias),
                 pl.BlockSpec((rw + gw, d), lambda i: (0, 0), pipeline_mode=pl.Buffered(1))]
    args += [p, p, p, h, gln_g, gln_b, ws, bias, w]
    final = final_g is not None
    in_specs.append(pl.BlockSpec((1, d), lambda i: (0, 0)) if final
                    else _layer_spec(next_g, layer + 1))
    args.append(final_g if final else next_g)
    out_specs = [hrow]
    out_shape = [jax.ShapeDtypeStruct((m, d), F32)]
    if not final:
        out_specs.append(hrow)
        out_shape.append(jax.ShapeDtypeStruct((m, d), BF16))
    if not seq:
        out_specs.append(pl.BlockSpec((tm, gw), lambda i: (i, 0)))
        out_shape.append(jax.ShapeDtypeStruct((m, gw), F32))
    outs = list(pl.pallas_call(
        functools.partial(_mix_out_kernel, seq=seq, final=final),
        grid=(m // tm,),
        in_specs=in_specs,
        out_specs=out_specs,
        out_shape=out_shape,
        compiler_params=_cparams(("parallel",)),
        name="mix_out",
    )(*args))
    h_out = outs.pop(0)
    xn = None if final else outs.pop(0)
    return h_out, xn, (None if seq else outs[0])


def _pad_last(x, width):
    return jnp.pad(x, [(0, 0)] * (x.ndim - 1) + [(0, width - x.shape[-1])])


def _pad_rows(x, rows):
    return jnp.pad(x, [(0, 0)] * (x.ndim - 2) + [(0, rows - x.shape[-2]), (0, 0)])


def _pick_tile(m, pref, mult):
    t = min(m, pref)
    while m % t or t % mult:
        t -= mult
    return t


def _trunk(x, s_init, shift_init, prm, weights, *, seq, zero_fill=None):
    n_seq, seq_len, d = x.shape
    m = n_seq * seq_len
    depth = prm["depth"]
    rw, gw = prm["rw"], prm["gw"]
    lora_w = prm["lora_w"]
    n_heads = rw // HEAD
    tn = 1024
    z_blk = 3

    tm_mm = _pick_tile(m, 1024, 8)
    tm_ew = _pick_tile(m, 512, GMLP_CHUNK if seq else 8)
    assert not seq or seq_len % SEQ_ROWS == 0

    h = x.reshape(m, d)
    s_out, shift_out, vn_out = [], [], []
    prev_rkv = shift_init[:, :, :3 * rw]
    prev_lo = _pad_last(shift_init[:, :, 3 * rw:], LO_W)
    if seq:
        prev_rkv, prev_lo = prev_rkv[:, :, None, :], prev_lo[:, :, None, :]
        s0 = s_init.reshape(depth, n_seq, n_heads // 2, 2, HEAD, HEAD)
        z = jnp.zeros_like(s0[:, :, :, 0])
        s0_bd = jnp.concatenate([jnp.concatenate([s0[:, :, :, 0], z], axis=-1),
                                 jnp.concatenate([z, s0[:, :, :, 1]], axis=-1)], axis=-2)
    else:
        s_init = jnp.transpose(s_init, (0, 2, 3, 4, 1))
        s_buf = weights[2] if weights[2] is not None else jnp.zeros_like(s_init)
        assert s_buf.shape == s_init.shape
    tok = (prm["mu_rkv"],) * 3 + tuple(prm[n] for n in ("mu_lo", "w0", "w2", "a0", "a2",
                                                         "k_k", "k_a"))
    gate = (prm["r_k"], prm["lnx_g"], prm["lnx_b"])
    v_first = None
    w_in_bf, w_out_bf, zero_buf = ([weights], [], None) if seq else weights
    for l in range(depth):
        p, p_lo = _in_proj(h if l == 0 else xn, prm["norm_g"], w_in_bf[l], l,
                           tm_mm if l == 0 else _pick_tile(m, 2 * tm_mm, _ROW_ALIGN), tn, rw,
                           gw, lora_w)
        vres = (v_first, prm["v0"], prm["v1"], prm["v2"]) if l > 0 else None

        if seq:
            casts = [(prm["w_out"], l)] + ([(prm["w_in"], l + 1)] if l + 1 < depth else [])
            o_r, v, s_bd, side = _wkv_seq(p, p_lo, prev_rkv, prev_lo, tok, gate, s0_bd, vres, l,
                                          casts, zero_fill if l == 0 else None, n_seq=n_seq,
                                          seq_len=seq_len, rw=rw, z_blk=z_blk, emit_v=l == 0)
            w_out_bf.append(side[0])
            w_in_bf += side[1:len(casts)]
            if l == 0 and zero_fill is not None:
                zero_buf = side[-1]
            s_out.append(s_bd)
            branch = (o_r,)
            ws, bias = prm["w_s"], prm["bias_seq"]
        else:
            r, lw, k, v, a, b = _rwkv_prep(p, p_lo, prev_rkv, prev_lo, tok, vres, l,
                                           tm=tm_ew, rw=rw)
            y, s_buf = _wkv_step(r, lw, k, v, a, b, s_init, s_buf, l)
            branch = (y, r, k, v, gate[1], gate[2], gate[0])
            ws, bias = prm["w_s0"], prm["bias0"]
        if l == 0:
            v_first = v

        last_row = lambda t: t.reshape(n_seq, seq_len, -1)[:, -1]
        shift_out.append(jnp.concatenate(
            [last_row(p)[:, :3 * rw], last_row(p_lo)[:, :lora_w]], axis=-1))

        h, xn, vn = _mix_out(branch, p, h, prm["gln_g"], prm["gln_b"], ws, bias,
                             w_out_bf[l], l, prm["norm_g"],
                             prm["final_g"] if l == depth - 1 else None,
                             seq=seq, tm=tm_ew, u_blk=z_blk + 1)
        if not seq:
            vn_out.append(vn.reshape(n_seq, seq_len, gw))
    if seq:
        s_bd = jnp.stack(s_out)
        s_final = jnp.stack([s_bd[..., :HEAD, :HEAD], s_bd[..., HEAD:, HEAD:]], axis=3)
        s_final = s_final.reshape(depth, n_seq, n_heads, HEAD, HEAD)
    else:
        s_final = jnp.transpose(s_buf, (0, 4, 1, 2, 3))
    return (h.reshape(n_seq, seq_len, d), s_final, shift_out, vn_out,
            (w_in_bf, w_out_bf, zero_buf))


def kernel(x_prompt, x_sample, state_wkv, state_shift, norm_g, w_in, mu_shift, w0, w2, a0, a2,
           k_k, k_a, r_k, lnx_g, lnx_b, v0, v1, v2, gln_g, gln_b, w_s, b_s, w_out, final_g):
    depth, d, _ = w_in.shape
    rw = w0.shape[1]
    gw = gln_g.shape[1]
    dl, il = w2.shape[1], a2.shape[1]
    assert rw % PAIR == 0 and gw == rw and w_s.shape[2] == GMLP_CHUNK
    assert gw // GMLP_GROUPS == LANES and dl + il <= LO_W
    sc = 3 * rw + dl + il

    row3 = lambda t: t[:, None, :]
    prm = dict(
        depth=depth, rw=rw, gw=gw, lora_w=dl + il,
        norm_g=row3(norm_g),
        w_in=jnp.swapaxes(w_in, 1, 2),
        mu_rkv=row3(mu_shift[:, :3 * rw]),
        mu_lo=row3(_pad_last(mu_shift[:, 3 * rw:], LO_W)),
        w0=row3(w0), w2=_pad_rows(w2, LO_W).astype(BF16),
        a0=row3(a0),
        a2=jnp.pad(a2, ((0, 0), (dl, LO_W - dl - il), (0, 0))).astype(BF16),
        k_k=row3(k_k), k_a=row3(k_a), r_k=row3(r_k.reshape(depth, rw)),
        lnx_g=row3(lnx_g), lnx_b=row3(lnx_b),
        v0=row3(v0), v1=_pad_last(v1, LORA_PAD).astype(BF16), v2=_pad_rows(v2, LORA_PAD).astype(BF16),
        gln_g=row3(gln_g), gln_b=row3(gln_b),
        w_s=w_s,
        bias_seq=jnp.repeat(jnp.swapaxes(b_s, 1, 2), gw // GMLP_GROUPS, axis=2),
        w_s0=row3(jnp.repeat(w_s[:, :, 0, 0], gw // GMLP_GROUPS, axis=1)),
        bias0=row3(jnp.repeat(b_s[:, :, 0], gw // GMLP_GROUPS, axis=1)),
        w_out=w_out,
        final_g=final_g[None, :],
    )

    nb = x_prompt.shape[0]
    n_heads = rw // HEAD
    s0_p = jnp.zeros((depth, nb, n_heads, HEAD, HEAD), F32)
    sh0_p = jnp.zeros((depth, nb, sc), F32)
    state_buf = (depth, n_heads, HEAD, HEAD, x_sample.shape[0] * x_sample.shape[1])
    seq_steps = nb * (x_prompt.shape[1] // SEQ_ROWS) + 1
    y_p, s_p, sh_p, _, weights = _trunk(
        x_prompt, s0_p, sh0_p, prm, prm["w_in"][0].astype(BF16), seq=True,
        zero_fill=state_buf if depth * n_heads <= seq_steps else None)
    y_s, s_s, sh_s, vn_s, _ = _trunk(x_sample, state_wkv, state_shift, prm, weights, seq=False)
    return (y_p, y_s, s_p, jnp.stack(sh_p), s_s, jnp.stack(sh_s), jnp.stack(vn_s))
```
